```python
import math
import jax, jax.numpy as jnp
from jax import lax
import numpy as np

D_MODEL = 1024
BATCH = 8
SEQ = 2048
DEPTH = 2
DEC_BATCH = 32
DEC_SEQ = 8
PAST_LEN = 8192
PAGE_SIZE = 128

DA_HEADS = 8
DA_QK_DIM = 64
DA_V_DIM = 2 * DA_QK_DIM
DA_MAPS = 2 * DA_HEADS
DA_QK_W = DA_MAPS * DA_QK_DIM
DA_V_W = DA_HEADS * DA_V_DIM
SSM_INNER = 2 * D_MODEL
SSM_HEAD_DIM = 64
SSM_HEADS = SSM_INNER // SSM_HEAD_DIM
SSM_GROUPS = 8
SSM_STATE = 128
SSM_CONV = 4
SSM_GN = SSM_GROUPS * SSM_STATE
SSM_CONV_CH = SSM_INNER + 2 * SSM_GN
SSM_CHUNK = 128
MEM_LEN = 256
CA_HEADS = 4
CA_HEAD_DIM = D_MODEL // CA_HEADS
CA_W = CA_HEADS * CA_HEAD_DIM
REL_BUCKETS = 32
REL_MAX_DIST = 128
D_FF = 4 * D_MODEL
N_BRANCH = 3
Q_BLOCK = 128
EPS = 1e-5

OFF_Q = 0
OFF_K = OFF_Q + DA_QK_W
OFF_V = OFF_K + DA_QK_W
OFF_Z = OFF_V + DA_V_W
OFF_XBC = OFF_Z + SSM_INNER
OFF_DT = OFF_XBC + SSM_CONV_CH
OFF_CQ = OFF_DT + SSM_HEADS
OFF_G = OFF_CQ + CA_W
IN_W = OFF_G + N_BRANCH * D_MODEL

kernel_name = "hybrid_diffattn_ssd_memxattn_step"


def rmsnorm(x, g):
    xf = x.astype(jnp.float32)
    y = xf * lax.rsqrt(jnp.mean(xf * xf, axis=-1, keepdims=True) + EPS)
    return (y * g.astype(jnp.float32)).astype(x.dtype)


def lambda_init(layer):
    return 0.8 - 0.6 * math.exp(-0.3 * layer)


def rel_bias(q_pos, k_pos, table):
    n = jnp.maximum(q_pos[:, None] - k_pos[None, :], 0)
    max_exact = REL_BUCKETS // 2
    nf = jnp.maximum(n, 1).astype(jnp.float32)
    large = max_exact + (jnp.log(nf / max_exact) / math.log(REL_MAX_DIST / max_exact)
                         * (REL_BUCKETS - max_exact)).astype(jnp.int32)
    bucket = jnp.where(n < max_exact, n, jnp.minimum(large, REL_BUCKETS - 1))
    return jnp.transpose(table[bucket], (2, 0, 1)).astype(jnp.float32)


def diff_attention(q, q_pos, k_parts, v_parts, kpos_parts, lam, table):
    B, T = q.shape[0], q.shape[1]
    qb = min(Q_BLOCK, T)
    nb = T // qb
    q_blocks = q.reshape(B, nb, qb, DA_MAPS, DA_QK_DIM).swapaxes(0, 1)
    pos_blocks = q_pos.reshape(nb, qb)
    scale = DA_QK_DIM ** -0.5

    def one_block(args):
        qblk, qp = args
        scores = []
        for k, kp in zip(k_parts, kpos_parts):
            s = jnp.einsum('bqmd,bkmd->bmqk', qblk, k, preferred_element_type=jnp.float32) * scale
            s = s + rel_bias(qp, kp, table)[None]
            s = jnp.where((kp[None, :] <= qp[:, None])[None, None], s, -jnp.inf)
            scores.append(s)
        prob = jax.nn.softmax(jnp.concatenate(scores, axis=-1), axis=-1)
        prob = prob.reshape(B, DA_HEADS, 2, qb, prob.shape[-1])
        w = prob[:, :, 0] - lam * prob[:, :, 1]
        outs = []
        start = 0
        for v in v_parts:
            n = v.shape[1]
            outs.append(jnp.einsum('bhqk,bkhd->bqhd', w[..., start:start + n].astype(v.dtype), v,
                                   preferred_element_type=jnp.float32))
            start += n
        out = outs[0]
        for o in outs[1:]:
            out = out + o
        return out

    out = lax.map(one_block, (q_blocks, pos_blocks))
    return out.swapaxes(0, 1).reshape(B, T, DA_HEADS, DA_V_DIM)


def causal_conv(u, state, w, b):
    full = jnp.concatenate([state.astype(u.dtype), u], axis=1)
    T = u.shape[1]
    y = b + full[:, 0:T] * w[0]
    for k in range(1, SSM_CONV):
        y = y + full[:, k:k + T] * w[k]
    return jax.nn.silu(y), full[:, -(SSM_CONV - 1):]


def ssd_scan(x, dt, A, Bm, Cm, h0):
    Bsz, T = x.shape[0], x.shape[1]
    L = min(SSM_CHUNK, T)
    nc = T // L
    R = SSM_HEADS // SSM_GROUPS

    def chunkify(a):
        return a.reshape((Bsz, nc, L) + a.shape[2:]).swapaxes(0, 1)

    xs = (chunkify(x.reshape(Bsz, T, SSM_GROUPS, R, SSM_HEAD_DIM)),
          chunkify(dt.reshape(Bsz, T, SSM_GROUPS, R)), chunkify(Bm), chunkify(Cm))
    Ag = A.reshape(SSM_GROUPS, R)
    causal = jnp.tril(jnp.ones((L, L), dtype=bool))

    def step(h, inp):
        xc, dtc, bc, cc = inp
        a_cum = jnp.cumsum(dtc * Ag, axis=1)
        seg = a_cum[:, :, None] - a_cum[:, None, :]
        decay = jnp.exp(jnp.where(causal[None, :, :, None, None], seg, -jnp.inf))
        cb = jnp.einsum('btgn,bsgn->btsg', cc, bc)
        w = cb[..., None] * decay * dtc[:, None]
        y = jnp.einsum('btsgr,bsgrp->btgrp', w, xc)
        y = y + jnp.einsum('btgn,bgrpn->btgrp', cc, h) * jnp.exp(a_cum)[..., None]
        to_end = jnp.exp(a_cum[:, -1:] - a_cum) * dtc
        h = h * jnp.exp(a_cum[:, -1])[..., None, None] + jnp.einsum('bsgr,bsgn,bsgrp->bgrpn', to_end, bc, xc)
        return h, y

    hg = h0.astype(jnp.float32).reshape(Bsz, SSM_GROUPS, R, SSM_HEAD_DIM, SSM_STATE)
    h, ys = lax.scan(step, hg, xs)
    y = ys.swapaxes(0, 1).reshape(Bsz, T, SSM_HEADS, SSM_HEAD_DIM)
    return y, h.reshape(Bsz, SSM_HEADS, SSM_HEAD_DIM, SSM_STATE)


def memory_kv(mem, l, p):
    B, M = mem.shape[0], mem.shape[1]
    m = rmsnorm(mem, p['g_mem'][l])
    kv = m @ p['w_mem_kv'][l]
    mk = rmsnorm(kv[..., :CA_W].reshape(B, M, CA_HEADS, CA_HEAD_DIM), p['g_ca_k'][l])
    mv = kv[..., CA_W:].reshape(B, M, CA_HEADS, CA_HEAD_DIM)
    return mk, mv


def layer(x, l, p, table, mem_k, mem_v, ssm_h0, conv0, pos0, k_past, v_past):
    B, T = x.shape[0], x.shape[1]
    f32 = jnp.float32
    h = rmsnorm(x, p['g_mix'][l])
    u = h @ p['w_in'][l]
    q = rmsnorm(u[..., OFF_Q:OFF_K].reshape(B, T, DA_MAPS, DA_QK_DIM), p['g_da_q'][l])
    k = rmsnorm(u[..., OFF_K:OFF_V].reshape(B, T, DA_MAPS, DA_QK_DIM), p['g_da_k'][l])
    v = u[..., OFF_V:OFF_Z].reshape(B, T, DA_HEADS, DA_V_DIM)
    z = u[..., OFF_Z:OFF_XBC]
    xbc = u[..., OFF_XBC:OFF_DT]
    dt_raw = u[..., OFF_DT:OFF_CQ]
    cq = rmsnorm(u[..., OFF_CQ:OFF_G].reshape(B, T, CA_HEADS, CA_HEAD_DIM), p['g_ca_q'][l])
    gates = jax.nn.sigmoid(u[..., OFF_G:].astype(f32)).reshape(B, T, N_BRANCH, D_MODEL)

    lam0 = lambda_init(l)
    lam = (jnp.exp(jnp.sum(p['lambda_q1'][l].astype(f32) * p['lambda_k1'][l].astype(f32)))
           - jnp.exp(jnp.sum(p['lambda_q2'][l].astype(f32) * p['lambda_k2'][l].astype(f32))) + lam0)
    q_pos = pos0 + jnp.arange(T, dtype=jnp.int32)
    if k_past is None:
        k_parts, v_parts, kpos_parts = [k], [v], [q_pos]
    else:
        past_pos = jnp.arange(k_past.shape[1], dtype=jnp.int32)
        k_parts, v_parts, kpos_parts = [k_past, k], [v_past, v], [past_pos, q_pos]
    o_a = diff_attention(q, q_pos, k_parts, v_parts, kpos_parts, lam, table)
    o_a = rmsnorm(o_a, p['g_da_out'][l]) * (1.0 - lam0)
    br_a = o_a.reshape(B, T, DA_V_W).astype(x.dtype) @ p['w_br_attn'][l]

    xbc_c, conv_new = causal_conv(xbc, conv0, p['conv_w'][l], p['conv_b'][l])
    xs = xbc_c[..., :SSM_INNER].reshape(B, T, SSM_HEADS, SSM_HEAD_DIM).astype(f32)
    Bm = xbc_c[..., SSM_INNER:SSM_INNER + SSM_GN].reshape(B, T, SSM_GROUPS, SSM_STATE).astype(f32)
    Cm = xbc_c[..., SSM_INNER + SSM_GN:].reshape(B, T, SSM_GROUPS, SSM_STATE).astype(f32)
    dt = jax.nn.softplus(dt_raw.astype(f32) + p['dt_bias'][l].astype(f32))
    A = -jnp.exp(p['a_log'][l].astype(f32))
    y, h_new = ssd_scan(xs, dt, A, Bm, Cm, ssm_h0)
    y = y + p['d_skip'][l].astype(f32)[:, None] * xs
    y = y.reshape(B, T, SSM_INNER) * jax.nn.silu(z.astype(f32))
    y = rmsnorm(y.reshape(B, T, SSM_GROUPS, SSM_INNER // SSM_GROUPS),
                p['g_ssm_out'][l].reshape(SSM_GROUPS, SSM_INNER // SSM_GROUPS)).reshape(B, T, SSM_INNER)
    br_s = y.astype(x.dtype) @ p['w_br_ssm'][l]

    s = jnp.einsum('bqhd,bkhd->bhqk', cq, mem_k, preferred_element_type=f32) * (CA_HEAD_DIM ** -0.5)
    pc = jax.nn.softmax(s, axis=-1)
    o_c = jnp.einsum('bhqk,bkhd->bqhd', pc.astype(mem_v.dtype), mem_v, preferred_element_type=f32)
    br_c = o_c.reshape(B, T, CA_W).astype(x.dtype) @ p['w_br_cross'][l]

    merged = gates[:, :, 0] * br_a + gates[:, :, 1] * br_s + gates[:, :, 2] * br_c
    x = x + merged.astype(x.dtype) @ p['w_out'][l]

    h2 = rmsnorm(x, p['g_mlp'][l])
    x = x + jnp.square(jax.nn.relu(h2 @ p['w_up'][l])) @ p['w_down'][l]
    return x, k, v, h_new, conv_new


def setup_inputs(seed: int = 0) -> dict:
    key = jax.random.key(seed)
    keys = iter(jax.random.split(key, 64))
    f32 = jnp.float32
    n_pages = PAST_LEN // PAGE_SIZE
    n_used = DEC_BATCH * n_pages
    n_phys = n_used + n_used // 4

    def nrm(shape, scale):
        return jax.random.normal(next(keys), shape, f32) * scale

    def gain(shape):
        return 1.0 + nrm(shape, 0.05)

    u_dt = jax.random.uniform(next(keys), (DEPTH, SSM_HEADS), f32)
    dt0 = jnp.exp(u_dt * (math.log(0.1) - math.log(0.001)) + math.log(0.001))
    dt_bias = dt0 + jnp.log(-jnp.expm1(-dt0))
    a_log = jnp.log(jax.random.uniform(next(keys), (DEPTH, SSM_HEADS), f32, 1.0, 16.0))
    page_table = jax.random.permutation(next(keys), n_phys)[:n_used].reshape(DEC_BATCH, n_pages).astype(jnp.int32)

    return {
        'x_prompt': nrm((BATCH, SEQ, D_MODEL), 1.0),
        'x_sample': nrm((DEC_BATCH, DEC_SEQ, D_MODEL), 1.0),
        'cache_k': nrm((DEPTH, n_phys, PAGE_SIZE, DA_MAPS, DA_QK_DIM), 1.0),
        'cache_v': nrm((DEPTH, n_phys, PAGE_SIZE, DA_HEADS, DA_V_DIM), 1.0),
        'cache_mem_k': nrm((DEPTH, DEC_BATCH, MEM_LEN, CA_HEADS, CA_HEAD_DIM), 1.0),
        'cache_mem_v': nrm((DEPTH, DEC_BATCH, MEM_LEN, CA_HEADS, CA_HEAD_DIM), 1.0),
        'state_ssm': nrm((DEPTH, DEC_BATCH, SSM_HEADS, SSM_HEAD_DIM, SSM_STATE), 0.5),
        'state_conv': nrm((DEPTH, DEC_BATCH, SSM_CONV - 1, SSM_CONV_CH), 1.0),
        'page_table': page_table,
        'mem_prompt': nrm((BATCH, MEM_LEN, D_MODEL), 1.0),
        'rel_bias_table': nrm((REL_BUCKETS, DA_MAPS), 0.5),
        'g_mix': gain((DEPTH, D_MODEL)),
        'w_in': nrm((DEPTH, D_MODEL, IN_W), D_MODEL ** -0.5),
        'g_da_q': gain((DEPTH, DA_QK_DIM)),
        'g_da_k': gain((DEPTH, DA_QK_DIM)),
        'lambda_q1': nrm((DEPTH, DA_QK_DIM), 0.1),
        'lambda_k1': nrm((DEPTH, DA_QK_DIM), 0.1),
        'lambda_q2': nrm((DEPTH, DA_QK_DIM), 0.1),
        'lambda_k2': nrm((DEPTH, DA_QK_DIM), 0.1),
        'g_da_out': gain((DEPTH, DA_V_DIM)),
        'w_br_attn': nrm((DEPTH, DA_V_W, D_MODEL), DA_V_W ** -0.5),
        'conv_w': nrm((DEPTH, SSM_CONV, SSM_CONV_CH), SSM_CONV ** -0.5),
        'conv_b': nrm((DEPTH, SSM_CONV_CH), 0.02),
        'dt_bias': dt_bias,
        'a_log': a_log,
        'd_skip': gain((DEPTH, SSM_HEADS)),
        'g_ssm_out': gain((DEPTH, SSM_INNER)),
        'w_br_ssm': nrm((DEPTH, SSM_INNER, D_MODEL), SSM_INNER ** -0.5),
        'g_mem': gain((DEPTH, D_MODEL)),
        'w_mem_kv': nrm((DEPTH, D_MODEL, 2 * CA_W), D_MODEL ** -0.5),
        'g_ca_q': gain((DEPTH, CA_HEAD_DIM)),
        'g_ca_k': gain((DEPTH, CA_HEAD_DIM)),
        'w_br_cross': nrm((DEPTH, CA_W, D_MODEL), CA_W ** -0.5),
        'w_out': nrm((DEPTH, D_MODEL, D_MODEL), D_MODEL ** -0.5),
        'g_mlp': gain((DEPTH, D_MODEL)),
        'w_up': nrm((DEPTH, D_MODEL, D_FF), D_MODEL ** -0.5),
        'w_down': nrm((DEPTH, D_FF, D_MODEL), D_FF ** -0.5),
    }


def reference(x_prompt, x_sample, cache_k, cache_v, cache_mem_k, cache_mem_v, state_ssm, state_conv,
              page_table, mem_prompt, rel_bias_table, g_mix, w_in, g_da_q, g_da_k, lambda_q1, lambda_k1,
              lambda_q2, lambda_k2, g_da_out, w_br_attn, conv_w, conv_b, dt_bias, a_log, d_skip, g_ssm_out,
              w_br_ssm, g_mem, w_mem_kv, g_ca_q, g_ca_k, w_br_cross, w_out, g_mlp, w_up, w_down):
    p = dict(g_mix=g_mix, w_in=w_in, g_da_q=g_da_q, g_da_k=g_da_k, lambda_q1=lambda_q1, lambda_k1=lambda_k1,
             lambda_q2=lambda_q2, lambda_k2=lambda_k2, g_da_out=g_da_out, w_br_attn=w_br_attn, conv_w=conv_w,
             conv_b=conv_b, dt_bias=dt_bias, a_log=a_log, d_skip=d_skip, g_ssm_out=g_ssm_out, w_br_ssm=w_br_ssm,
             g_mem=g_mem, w_mem_kv=w_mem_kv, g_ca_q=g_ca_q, g_ca_k=g_ca_k, w_br_cross=w_br_cross, w_out=w_out,
             g_mlp=g_mlp, w_up=w_up, w_down=w_down)

    Bp = x_prompt.shape[0]
    xp = x_prompt
    kp_l, vp_l, mkp_l, mvp_l, hp_l, cp_l = [], [], [], [], [], []
    for l in range(DEPTH):
        mk, mv = memory_kv(mem_prompt, l, p)
        h0 = jnp.zeros((Bp, SSM_HEADS, SSM_HEAD_DIM, SSM_STATE), jnp.float32)
        c0 = jnp.zeros((Bp, SSM_CONV - 1, SSM_CONV_CH), xp.dtype)
        xp, k, v, hn, cn = layer(xp, l, p, rel_bias_table, mk, mv, h0, c0, 0, None, None)
        kp_l.append(k); vp_l.append(v); mkp_l.append(mk); mvp_l.append(mv); hp_l.append(hn); cp_l.append(cn)

    Bs = x_sample.shape[0]
    n_pages = page_table.shape[1]
    past_len = n_pages * PAGE_SIZE
    xs = x_sample
    ks_l, vs_l, hs_l, cs_l = [], [], [], []
    for l in range(DEPTH):
        k_past = cache_k[l, page_table].reshape(Bs, past_len, DA_MAPS, DA_QK_DIM)
        v_past = cache_v[l, page_table].reshape(Bs, past_len, DA_HEADS, DA_V_DIM)
        xs, k, v, hn, cn = layer(xs, l, p, rel_bias_table, cache_mem_k[l], cache_mem_v[l], state_ssm[l],
                                 state_conv[l], past_len, k_past, v_past)
        ks_l.append(k); vs_l.append(v); hs_l.append(hn); cs_l.append(cn)

    return (xp, xs, jnp.stack(kp_l), jnp.stack(vp_l), jnp.stack(mkp_l), jnp.stack(mvp_l), jnp.stack(hp_l),
            jnp.stack(cp_l), jnp.stack(ks_l), jnp.stack(vs_l), jnp.stack(hs_l), jnp.stack(cs_l))
```

```python
import functools
import math

import jax
import jax.numpy as jnp
from jax import lax
from jax.experimental import pallas as pl
from jax.experimental.pallas import tpu as pltpu

F32, BF16 = jnp.float32, jnp.bfloat16
EPS = 1e-5
NEG = -1e30

D = 1024
N_MAPS, DK = 16, 64
N_HEADS, DV = 8, 128
SSM_INNER, SSM_P, SSM_N, SSM_G = 2048, 64, 128, 8
SSM_HEADS = SSM_INNER // SSM_P
N_PAIRS = SSM_HEADS // 2
CONV_K, CONV_CH = 4, 4096
MEM_LEN, CA_HEADS, CA_D = 256, 4, 256
D_FF = 4096
PAGE = 128
REL_BUCKETS, REL_MAX_DIST = 32, 128
SSM_L = 128

OFF_Q, OFF_K, OFF_V, OFF_Z, OFF_XBC = 0, 1024, 2048, 3072, 5120
OFF_DT = OFF_XBC + CONV_CH
OFF_CQ = OFF_DT + SSM_HEADS
OFF_G = OFF_CQ + D

TN = 512
PB_Q, PB_CQ, PB_K, PB_V, PB_XBC, PB_Z, PB_G, PB_END = 0, 2, 4, 6, 8, 16, 20, 26
VMEM_LIMIT = 48 << 20


def _params(n_axes, vmem=VMEM_LIMIT):
    return pltpu.CompilerParams(dimension_semantics=("arbitrary",) * n_axes, vmem_limit_bytes=vmem)


def _rms(x, gain):
    ms = jnp.mean(x * x, axis=-1, keepdims=True)
    return x * lax.rsqrt(ms + EPS) * gain


def _sigmoid(x):
    return 1.0 / (1.0 + jnp.exp(-x))


def _softplus(x):
    return jnp.maximum(x, 0.0) + jnp.log1p(jnp.exp(-jnp.abs(x)))


def _dot(a, b):
    return jnp.dot(a, b, preferred_element_type=F32)


def _dot_nt(a, b):
    return lax.dot_general(a, b, (((1,), (1,)), ((), ())), preferred_element_type=F32)


def _dot_tn(a, b):
    return lax.dot_general(a, b, (((0,), (0,)), ((), ())), preferred_element_type=F32)


def _dot_split3(a01, x):
    hi = x.astype(BF16)
    r = x - hi.astype(F32)
    mid = r.astype(BF16)
    lo = (r - mid.astype(F32)).astype(BF16)
    return _dot(a01, hi) + _dot(a01, mid) + _dot(a01, lo)


def _bucket_thresholds():
    max_exact = REL_BUCKETS // 2
    ratio = REL_MAX_DIST / max_exact
    return [math.ceil(max_exact * ratio ** (k / (REL_BUCKETS - max_exact))) for k in range(1, REL_BUCKETS - max_exact)]


def _bias_kernel(tab_ref, o_ref, *, off, shape):
    m = pl.program_id(0)
    r = lax.broadcasted_iota(jnp.int32, shape, 0)
    c = lax.broadcasted_iota(jnp.int32, shape, 1)
    n = off + r - c
    nn = jnp.maximum(n, 0)
    max_exact = REL_BUCKETS // 2
    large = jnp.full(shape, max_exact, jnp.int32)
    for t in _bucket_thresholds():
        large = large + (nn >= t).astype(jnp.int32)
    bucket = jnp.where(nn < max_exact, nn, large)
    val = jnp.zeros(shape, F32)
    for b in range(REL_BUCKETS):
        val = jnp.where(bucket == b, tab_ref[b, m], val)
    val = val - tab_ref[REL_BUCKETS - 1, m]
    o_ref[0] = jnp.where(n >= 0, val, NEG)


def _bias_tiles(table, off, shape):
    return pl.pallas_call(
        functools.partial(_bias_kernel, off=off, shape=shape),
        out_shape=jax.ShapeDtypeStruct((N_MAPS,) + shape, F32),
        grid=(N_MAPS,),
        in_specs=[pl.BlockSpec(memory_space=pltpu.SMEM)],
        out_specs=pl.BlockSpec((1,) + shape, lambda m: (m, 0, 0)),
        compiler_params=_params(1),
        name="rel_bias_tiles",
    )(table)


def _proj_kernel(x_ref, g_ref, w_ref, wdt_ref, dtb_ref, gq_ref, gk_ref, gcq_ref, s64_ref,
                 ub_ref, k32_ref, v32_ref, uf_ref, dt_ref, h_scr):
    j = pl.program_id(1)

    @pl.when(j == 0)
    def _():
        h = _rms(x_ref[...], g_ref[...]).astype(BF16)
        h_scr[...] = h
        dt_ref[...] = _softplus(_dot(h, wdt_ref[...]) + dtb_ref[...])

    u = _dot(h_scr[...], w_ref[...])

    def seg64_norm(gain):
        ms = _dot((u * u).astype(BF16), s64_ref[...])
        return u * lax.rsqrt(ms + EPS) * gain

    @pl.when(j < PB_CQ)
    def _():
        ub_ref[...] = (seg64_norm(gq_ref[...]) * (DK ** -0.5)).astype(BF16)

    @pl.when(jnp.logical_and(j >= PB_CQ, j < PB_K))
    def _():
        for s in range(TN // CA_D):
            seg = u[:, s * CA_D:(s + 1) * CA_D]
            ub_ref[:, s * CA_D:(s + 1) * CA_D] = (_rms(seg, gcq_ref[...]) * (CA_D ** -0.5)).astype(BF16)

    @pl.when(jnp.logical_and(j >= PB_K, j < PB_V))
    def _():
        kn = seg64_norm(gk_ref[...])
        k32_ref[...] = kn
        ub_ref[...] = kn.astype(BF16)

    @pl.when(jnp.logical_and(j >= PB_V, j < PB_XBC))
    def _():
        v32_ref[...] = u
        ub_ref[...] = u.astype(BF16)

    @pl.when(jnp.logical_and(j >= PB_XBC, j < PB_G))
    def _():
        uf_ref[...] = u

    @pl.when(j >= PB_G)
    def _():
        uf_ref[...] = _sigmoid(u)


def _proj(x, g, w, wdt, dtb, gq, gk, gcq, s64):
    n = x.shape[0]
    tm = min(1024, n)
    nb_b = PB_XBC
    nb_f = PB_END - PB_XBC
    row = lambda i, j: (i, 0)
    return pl.pallas_call(
        _proj_kernel,
        out_shape=(jax.ShapeDtypeStruct((n, nb_b * TN), BF16),
                   jax.ShapeDtypeStruct((n, D), F32),
                   jax.ShapeDtypeStruct((n, D), F32),
                   jax.ShapeDtypeStruct((n, nb_f * TN), F32),
                   jax.ShapeDtypeStruct((n, 128), F32)),
        grid=(n // tm, PB_END),
        in_specs=[pl.BlockSpec((tm, D), row),
                  pl.BlockSpec((1, D), lambda i, j: (0, 0)),
                  pl.BlockSpec((D, TN), lambda i, j: (0, j)),
                  pl.BlockSpec((D, 128), lambda i, j: (0, 0)),
                  pl.BlockSpec((1, 128), lambda i, j: (0, 0)),
                  pl.BlockSpec((1, TN), lambda i, j: (0, 0)),
                  pl.BlockSpec((1, TN), lambda i, j: (0, 0)),
                  pl.BlockSpec((1, CA_D), lambda i, j: (0, 0)),
                  pl.BlockSpec((TN, TN), lambda i, j: (0, 0))],
        out_specs=(pl.BlockSpec((tm, TN), lambda i, j: (i, jnp.minimum(j, nb_b - 1))),
                   pl.BlockSpec((tm, TN), lambda i, j: (i, jnp.clip(j - PB_K, 0, 1))),
                   pl.BlockSpec((tm, TN), lambda i, j: (i, jnp.clip(j - PB_V, 0, 1))),
                   pl.BlockSpec((tm, TN), lambda i, j: (i, jnp.clip(j - PB_XBC, 0, nb_f - 1))),
                   pl.BlockSpec((tm, 128), row)),
        scratch_shapes=[pltpu.VMEM((tm, D), BF16)],
        compiler_params=_params(2),
        name="in_proj",
    )(x, g, w, wdt, dtb, gq, gk, gcq, s64)


def _memkv_kernel(m_ref, g_ref, w_ref, gk_ref, mk_ref, mv_ref, h_scr):
    j = pl.program_id(1)

    @pl.when(j == 0)
    def _():
        h_scr[...] = _rms(m_ref[...], g_ref[...]).astype(BF16)

    u = _dot(h_scr[...], w_ref[...])

    @pl.when(j == 0)
    def _():
        for s in range(CA_HEADS):
            mk_ref[:, s * CA_D:(s + 1) * CA_D] = _rms(u[:, s * CA_D:(s + 1) * CA_D], gk_ref[...])

    @pl.when(j == 1)
    def _():
        mv_ref[...] = u


def _memkv(mem, g, w, gk):
    n = mem.shape[0]
    tm = min(512, n)
    return pl.pallas_call(
        _memkv_kernel,
        out_shape=(jax.ShapeDtypeStruct((n, D), F32), jax.ShapeDtypeStruct((n, D), F32)),
        grid=(n // tm, 2),
        in_specs=[pl.BlockSpec((tm, D), lambda i, j: (i, 0)),
                  pl.BlockSpec((1, D), lambda i, j: (0, 0)),
                  pl.BlockSpec((D, D), lambda i, j: (0, j)),
                  pl.BlockSpec((1, CA_D), lambda i, j: (0, 0))],
        out_specs=(pl.BlockSpec((tm, D), lambda i, j: (i, 0)), pl.BlockSpec((tm, D), lambda i, j: (i, 0))),
        scratch_shapes=[pltpu.VMEM((tm, D), BF16)],
        compiler_params=_params(2),
        name="mem_kv",
    )(mem, g, w, gk)


def _lambda(lam_ref, lam0):
    lv = lam_ref[...]
    s1 = jnp.sum(lv[0:1] * lv[1:2], axis=-1, keepdims=True)
    s2 = jnp.sum(lv[2:3] * lv[3:4], axis=-1, keepdims=True)
    return jnp.exp(s1) - jnp.exp(s2) + lam0


def _softmax_step(state, s, v_bf16):
    m, l, acc = state
    m_new = jnp.maximum(m, jnp.max(s, axis=-1, keepdims=True))
    alpha = jnp.exp(m - m_new)
    p = jnp.exp(s - m_new)
    l = alpha * l + jnp.sum(p, axis=-1, keepdims=True)
    acc = alpha * acc + _dot(p.astype(BF16), v_bf16)
    return m_new, l, acc


def _attn_prompt_kernel(lam_ref, q_ref, k_ref, v_ref, bias_ref, gout_ref, o_ref, *, tq, lam0):
    qi = pl.program_id(2)
    q = q_ref[...]
    lane = lax.broadcasted_iota(jnp.int32, q.shape, 1)
    qm = (jnp.where(lane < DK, q, jnp.zeros_like(q)), jnp.where(lane >= DK, q, jnp.zeros_like(q)))

    def block(kb, states, bias_idx):
        start = pl.multiple_of(kb * tq, tq)
        k = k_ref[pl.ds(start, tq), :]
        v = v_ref[pl.ds(start, tq), :]
        out = []
        for mi in range(2):
            s = _dot_nt(qm[mi], k)
            if bias_idx is not None:
                s = s + bias_ref[mi, bias_idx]
            out.append(_softmax_step(states[mi], s, v))
        return tuple(out)

    init = tuple((jnp.full((tq, 1), NEG, F32), jnp.zeros((tq, 1), F32), jnp.zeros((tq, DV), F32))
                 for _ in range(2))
    states = lax.fori_loop(0, qi - 1, lambda kb, st: block(kb, st, None), init)
    states = lax.fori_loop(jnp.maximum(qi - 1, 0), qi, lambda kb, st: block(kb, st, 1), states)
    states = block(qi, states, 0)

    lam = _lambda(lam_ref, lam0)
    (_, l0, a0), (_, l1, a1) = states
    o = a0 / l0 - lam * (a1 / l1)
    o_ref[...] = (_rms(o, gout_ref[...]) * (1.0 - lam0)).astype(o_ref.dtype)


def _attn_prompt(ub3, lam_vecs, bias, gout, *, lam0, tq):
    b, t, _ = ub3.shape
    kcol, vcol = (PB_K * TN) // DV, (PB_V * TN) // DV
    return pl.pallas_call(
        functools.partial(_attn_prompt_kernel, tq=tq, lam0=lam0),
        out_shape=jax.ShapeDtypeStruct((b, t, D), BF16),
        grid=(b, N_HEADS, t // tq),
        in_specs=[pl.BlockSpec((4, DK), lambda b, h, i: (0, 0)),
                  pl.BlockSpec((None, tq, DV), lambda b, h, i: (b, i, h)),
                  pl.BlockSpec((None, t, DV), lambda b, h, i: (b, 0, kcol + h)),
                  pl.BlockSpec((None, t, DV), lambda b, h, i: (b, 0, vcol + h)),
                  pl.BlockSpec((2, 2, tq, tq), lambda b, h, i: (h, 0, 0, 0)),
                  pl.BlockSpec((1, DV), lambda b, h, i: (0, 0))],
        out_specs=pl.BlockSpec((None, tq, DV), lambda b, h, i: (b, i, h)),
        compiler_params=_params(3),
        name="diff_attn_prompt",
    )(lam_vecs, ub3, ub3, ub3, bias, gout)


def _head_queries(q, h):
    qh = q[:, h * DV:(h + 1) * DV]
    lane = lax.broadcasted_iota(jnp.int32, qh.shape, 1)
    return jnp.concatenate([jnp.where(lane < DK, qh, 0.0), jnp.where(lane >= DK, qh, 0.0)], axis=0).astype(BF16)


def _attn_sample_kernel(pt_ref, q_ref, kn_ref, vn_ref, bpast_ref, bnew_ref, lam_ref, gout_ref, *rest, npg, lam0):
    k_refs, v_refs = rest[:npg], rest[npg:2 * npg]
    o_ref, m_scr, l_scr, acc_scr = rest[2 * npg:]
    c = pl.program_id(1)
    last = c == pl.num_programs(1) - 1
    tdec = q_ref.shape[0]

    @pl.when(c == 0)
    def _():
        m_scr[...] = jnp.full(m_scr.shape, NEG, F32)
        l_scr[...] = jnp.zeros(l_scr.shape, F32)
        acc_scr[...] = jnp.zeros(acc_scr.shape, F32)

    def update(h, s, pv_fn):
        m_old = m_scr[h]
        m_new = jnp.maximum(m_old, jnp.max(s, axis=-1, keepdims=True))
        alpha = jnp.exp(m_old - m_new)
        p = jnp.exp(s - m_new[:, 0:1])
        l_scr[h] = alpha * l_scr[h] + jnp.sum(p, axis=-1, keepdims=True)
        acc_scr[h] = alpha * acc_scr[h] + pv_fn(p)
        m_scr[h] = m_new

    q = q_ref[...]
    near = last.astype(F32)
    for h in range(N_HEADS):
        qm = _head_queries(q, h)
        cols = slice(h * DV, (h + 1) * DV)
        s_parts = []
        for i in range(npg):
            s = _dot_nt(qm, k_refs[i][:, cols].astype(BF16))
            if i == npg - 1:
                s = s + near * bpast_ref[h]
            s_parts.append(s)
        s = jnp.concatenate(s_parts, axis=1)

        def pv(p, cols=cols):
            out = _dot(p[:, 0:PAGE].astype(BF16), v_refs[0][:, cols].astype(BF16))
            for i in range(1, npg):
                out = out + _dot(p[:, i * PAGE:(i + 1) * PAGE].astype(BF16), v_refs[i][:, cols].astype(BF16))
            return out

        update(h, s, pv)

    @pl.when(last)
    def _():
        pad = jnp.zeros((PAGE - tdec, D), F32)
        kn = jnp.concatenate([kn_ref[...], pad], axis=0)
        vn = jnp.concatenate([vn_ref[...], pad], axis=0)
        lam = _lambda(lam_ref, lam0)
        for h in range(N_HEADS):
            cols = slice(h * DV, (h + 1) * DV)
            s = _dot_nt(_head_queries(q, h), kn[:, cols].astype(BF16)) + bnew_ref[h]
            update(h, s, lambda p, cols=cols: _dot(p.astype(BF16), vn[:, cols].astype(BF16)))
            a, l = acc_scr[h], l_scr[h]
            o = a[0:tdec] / l[0:tdec] - lam * (a[tdec:2 * tdec] / l[tdec:2 * tdec])
            o_ref[:, cols] = _rms(o, gout_ref[...]) * (1.0 - lam0)


def _attn_sample(page_table, q3, kn3, vn3, cache_k4, cache_v4, layer, bpast, bnew, lam_vecs, gout, *, lam0):
    bs, tdec, _ = q3.shape
    n_pages = page_table.shape[1]
    npg = math.gcd(n_pages, 8)
    pt = page_table.reshape(-1)

    def page_spec(i):
        return pl.BlockSpec((None, None, PAGE, D),
                            lambda b, c, pt_ref: (layer, pt_ref[b * n_pages + c * npg + i], 0, 0))

    per_b = pl.BlockSpec((None, tdec, D), lambda b, c, pt_ref: (b, 0, 0))
    whole = lambda shape: pl.BlockSpec(shape, lambda b, c, pt_ref: (0,) * len(shape))
    stat = pltpu.VMEM((N_HEADS, 2 * tdec, DV), F32)
    return pl.pallas_call(
        functools.partial(_attn_sample_kernel, npg=npg, lam0=lam0),
        out_shape=jax.ShapeDtypeStruct((bs, tdec, D), F32),
        grid_spec=pltpu.PrefetchScalarGridSpec(
            num_scalar_prefetch=1,
            grid=(bs, n_pages // npg),
            in_specs=[per_b, per_b, per_b,
                      whole((N_HEADS, 2 * tdec, PAGE)), whole((N_HEADS, 2 * tdec, PAGE)),
                      whole((4, DK)), whole((1, DV))]
                     + [page_spec(i) for i in range(npg)] + [page_spec(i) for i in range(npg)],
            out_specs=per_b,
            scratch_shapes=[stat, stat, stat]),
        compiler_params=_params(2),
        name="diff_attn_sample",
    )(pt, q3, kn3, vn3, bpast, bnew, lam_vecs, gout, *([cache_k4] * npg), *([cache_v4] * npg))


def _cross_kernel(cq_ref, mk_ref, mv_ref, o_ref):
    rows = cq_ref.shape[0]
    for h in range(CA_HEADS):
        cols = slice(h * CA_D, (h + 1) * CA_D)
        q = cq_ref[:, cols]
        if rows % 16:
            q = jnp.concatenate([q.astype(F32), jnp.zeros((16 - rows % 16, CA_D), F32)], axis=0)
        s = _dot_nt(q.astype(BF16), mk_ref[:, cols].astype(BF16))
        p = jnp.exp(s - jnp.max(s, axis=-1, keepdims=True))
        o = _dot(p.astype(BF16), mv_ref[:, cols].astype(BF16)) / jnp.sum(p, axis=-1, keepdims=True)
        o_ref[:, cols] = o[0:rows].astype(o_ref.dtype)


def _cross(cq3, col_block, mk3, mv3, out_dtype):
    b, t, _ = cq3.shape
    tm = min(512, t)
    return pl.pallas_call(
        _cross_kernel,
        out_shape=jax.ShapeDtypeStruct((b, t, D), out_dtype),
        grid=(b, t // tm),
        in_specs=[pl.BlockSpec((None, tm, D), lambda b, i: (b, i, col_block)),
                  pl.BlockSpec((None, MEM_LEN, D), lambda b, i: (b, 0, 0)),
                  pl.BlockSpec((None, MEM_LEN, D), lambda b, i: (b, 0, 0))],
        out_specs=pl.BlockSpec((None, tm, D), lambda b, i: (b, i, 0)),
        compiler_params=_params(2),
        name="mem_cross_attn",
    )(cq3, mk3, mv3)


def _ssd_kernel(xbc_ref, z_ref, dt_ref, h0_ref, c0_ref, cw_ref, cb_ref, alog_ref, dskip_ref, gout_ref,
                y_ref, hn_ref, cn_ref, buf, xc_scr, dt_scr, yb, h_scr, *, tv):
    L = SSM_L
    c = pl.program_id(1)
    head = 8

    @pl.when(c == 0)
    def _():
        buf[0:head, :] = jnp.zeros((head, CONV_CH), F32)
        buf[head - (CONV_K - 1):head, :] = c0_ref[...]
        h_scr[...] = h0_ref[...]
        if tv < L:
            buf[head + tv:, :] = jnp.zeros((L - tv, CONV_CH), F32)
            dt_scr[...] = jnp.zeros(dt_scr.shape, F32)

    buf[head:head + tv, :] = xbc_ref[...]
    cn_ref[...] = buf[head + tv - (CONV_K - 1):head + tv, :]

    cw = cw_ref[...]
    for j in range(CONV_CH // 512):
        cols = slice(j * 512, (j + 1) * 512)
        acc = cb_ref[:, cols] + cw[CONV_K - 1:CONV_K, cols] * buf[head:head + L, cols]
        for k in range(CONV_K - 1):
            acc = acc + cw[k:k + 1, cols] * buf[head - (CONV_K - 1) + k:head - (CONV_K - 1) + k + L, cols]
        xc_scr[:, cols] = acc * _sigmoid(acc)

    if tv == L:
        buf[0:head, :] = buf[L:L + head, :]
        dt = dt_ref[...]
    else:
        dt_scr[0:tv, :] = dt_ref[...]
        dt = dt_scr[...]

    a_neg = -jnp.exp(alog_ref[...])
    d_a = dt * a_neg
    row = lax.broadcasted_iota(jnp.int32, (L, L), 0)
    col = lax.broadcasted_iota(jnp.int32, (L, L), 1)
    causal = col <= row
    a_cum = _dot_split3(causal.astype(BF16), d_a)
    a_cum_t = a_cum.T
    dt_t = dt.T
    a_last = a_cum[L - 1:L, :]
    e_cum = jnp.exp(a_cum)
    to_end = jnp.exp(a_last - a_cum) * dt
    e_last = jnp.exp(a_last)
    lane = lax.broadcasted_iota(jnp.int32, (L, 128), 1)
    first = lane < SSM_P
    row_first = lax.broadcasted_iota(jnp.int32, (128, 128), 0) < SSM_P

    for g in range(SSM_G):
        b_g = xc_scr[:, SSM_INNER + g * SSM_N:SSM_INNER + (g + 1) * SSM_N].astype(BF16)
        c_g = xc_scr[:, SSM_INNER + SSM_G * SSM_N + g * SSM_N:SSM_INNER + SSM_G * SSM_N + (g + 1) * SSM_N].astype(BF16)
        cb = _dot_nt(c_g, b_g)
        for jj in (2 * g, 2 * g + 1):
            cols = slice(jj * 128, (jj + 1) * 128)
            x_pair = xc_scr[:, cols]
            x_bf = x_pair.astype(BF16)
            yw = []
            for hh in (2 * jj, 2 * jj + 1):
                seg = a_cum[:, hh:hh + 1] - a_cum_t[hh:hh + 1, :]
                w = cb * jnp.exp(jnp.where(causal, seg, NEG)) * dt_t[hh:hh + 1, :]
                yw.append(_dot(w.astype(BF16), x_bf))
            ha, hb = 2 * jj, 2 * jj + 1
            hp = h_scr[jj]
            y = jnp.where(first, yw[0], yw[1])
            y = y + _dot_nt(c_g, hp.astype(BF16)) * jnp.where(first, e_cum[:, ha:ha + 1], e_cum[:, hb:hb + 1])
            y = y + dskip_ref[:, cols] * x_pair
            te = jnp.where(first, to_end[:, ha:ha + 1], to_end[:, hb:hb + 1])
            upd = _dot_tn((x_pair * te).astype(BF16), b_g)
            h_scr[jj] = hp * jnp.where(row_first, e_last[:, ha:ha + 1], e_last[:, hb:hb + 1]) + upd
            yb[:, cols] = y[0:tv] * (z_ref[:, cols] * _sigmoid(z_ref[:, cols]))

    gw = SSM_INNER // SSM_G
    for g in range(SSM_G):
        cols = slice(g * gw, (g + 1) * gw)
        y_ref[:, cols] = _rms(yb[:, cols], gout_ref[:, cols]).astype(y_ref.dtype)
    hn_ref[...] = h_scr[...]


def _ssd(xbc_src, xbc_blk, z_src, z_blk, dt3, h0, c0, cw, cb, alog, dskip, gout, out_dtype):
    b, t, _ = dt3.shape
    tv = min(SSM_L, t)
    nc = t // tv
    seq = lambda width, blk: pl.BlockSpec((None, tv, width), lambda b, c: (b, c, blk))
    per_b = lambda shape: pl.BlockSpec((None,) + shape, lambda b, c: (b,) + (0,) * len(shape))
    whole = lambda shape: pl.BlockSpec(shape, lambda b, c: (0,) * len(shape))
    return pl.pallas_call(
        functools.partial(_ssd_kernel, tv=tv),
        out_shape=(jax.ShapeDtypeStruct((b, t, SSM_INNER), out_dtype),
                   jax.ShapeDtypeStruct((b, N_PAIRS, 128, SSM_N), F32),
                   jax.ShapeDtypeStruct((b, CONV_K - 1, CONV_CH), F32)),
        grid=(b, nc),
        in_specs=[seq(CONV_CH, xbc_blk), seq(SSM_INNER, z_blk), seq(128, 0),
                  per_b((N_PAIRS, 128, SSM_N)), per_b((CONV_K - 1, CONV_CH)),
                  whole((CONV_K, CONV_CH)), whole((1, CONV_CH)), whole((1, 128)),
                  whole((1, SSM_INNER)), whole((1, SSM_INNER))],
        out_specs=(seq(SSM_INNER, 0), per_b((N_PAIRS, 128, SSM_N)), per_b((CONV_K - 1, CONV_CH))),
        scratch_shapes=[pltpu.VMEM((SSM_L + 8, CONV_CH), F32), pltpu.VMEM((SSM_L, CONV_CH), F32),
                        pltpu.VMEM((SSM_L, 128), F32), pltpu.VMEM((tv, SSM_INNER), F32),
                        pltpu.VMEM((N_PAIRS, 128, SSM_N), F32)],
        compiler_params=_params(2),
        name="conv_ssd",
    )(xbc_src, z_src, dt3, h0, c0, cw, cb, alog, dskip, gout)


def _merge_kernel(x_ref, oa_ref, ys_ref, oc_ref, g_ref, wa_ref, ws_ref, wc_ref, wo_ref, o_ref):
    ba = _dot(oa_ref[...].astype(BF16), wa_ref[...])
    bs = _dot(ys_ref[...].astype(BF16), ws_ref[...])
    bc = _dot(oc_ref[...].astype(BF16), wc_ref[...])
    merged = g_ref[:, 0:D] * ba + g_ref[:, D:2 * D] * bs + g_ref[:, 2 * D:3 * D] * bc
    o_ref[...] = x_ref[...] + _dot(merged.astype(BF16), wo_ref[...])


def _merge(x, oa, ys, oc, uf, wa, ws, wc, wo):
    n = x.shape[0]
    tm = min(256, n)
    gate_blk = ((PB_G - PB_XBC) * TN) // (3 * D)
    row = lambda width: pl.BlockSpec((tm, width), lambda i: (i, 0))
    whole = lambda shape: pl.BlockSpec(shape, lambda i: (0, 0))
    return pl.pallas_call(
        _merge_kernel,
        out_shape=jax.ShapeDtypeStruct((n, D), F32),
        grid=(n // tm,),
        in_specs=[row(D), row(D), row(SSM_INNER), row(D),
                  pl.BlockSpec((tm, 3 * D), lambda i: (i, gate_blk)),
                  whole((D, D)), whole((SSM_INNER, D)), whole((D, D)), whole((D, D))],
        out_specs=row(D),
        compiler_params=_params(1),
        name="branch_merge",
    )(x, oa, ys, oc, uf, wa, ws, wc, wo)


def _mlp_kernel(x_ref, g_ref, wu_ref, wd_ref, o_ref, h_scr, acc):
    f = pl.program_id(1)

    @pl.when(f == 0)
    def _():
        x = x_ref[...]
        h_scr[...] = _rms(x, g_ref[...]).astype(BF16)
        acc[...] = x

    a = jnp.maximum(_dot(h_scr[...], wu_ref[...]), 0.0)
    acc[...] += _dot((a * a).astype(BF16), wd_ref[...])

    @pl.when(f == pl.num_programs(1) - 1)
    def _():
        o_ref[...] = acc[...]


def _mlp(x, g, wu, wd):
    n = x.shape[0]
    tm = min(1024, n)
    tf = 1024
    return pl.pallas_call(
        _mlp_kernel,
        out_shape=jax.ShapeDtypeStruct((n, D), F32),
        grid=(n // tm, D_FF // tf),
        in_specs=[pl.BlockSpec((tm, D), lambda i, f: (i, 0)),
                  pl.BlockSpec((1, D), lambda i, f: (0, 0)),
                  pl.BlockSpec((D, tf), lambda i, f: (0, f)),
                  pl.BlockSpec((tf, D), lambda i, f: (f, 0))],
        out_specs=pl.BlockSpec((tm, D), lambda i, f: (i, 0)),
        scratch_shapes=[pltpu.VMEM((tm, D), BF16), pltpu.VMEM((tm, D), F32)],
        compiler_params=_params(2),
        name="relu2_mlp",
    )(x, g, wu, wd)


def _lambda_init(layer):
    return 0.8 - 0.6 * math.exp(-0.3 * layer)


def _layer_weights(l, p):
    w_in = p['w_in'][l]
    cols = lambda off, width: w_in[:, off:off + width]
    w_main = jnp.concatenate([cols(OFF_Q, D), cols(OFF_CQ, D), cols(OFF_K, D), cols(OFF_V, D),
                              cols(OFF_XBC, CONV_CH), cols(OFF_Z, SSM_INNER), cols(OFF_G, 3 * D)],
                             axis=1).astype(BF16)
    w_dt = jnp.pad(cols(OFF_DT, SSM_HEADS), ((0, 0), (0, 128 - SSM_HEADS))).astype(BF16)
    row = lambda v: v.reshape(1, -1).astype(F32)
    return dict(
        g_mix=row(p['g_mix'][l]), w_main=w_main, w_dt=w_dt,
        dt_bias=jnp.pad(row(p['dt_bias'][l]), ((0, 0), (0, 128 - SSM_HEADS))),
        gq=row(jnp.tile(p['g_da_q'][l], TN // DK)), gk=row(jnp.tile(p['g_da_k'][l], TN // DK)),
        gcq=row(p['g_ca_q'][l]),
        lam=jnp.stack([p['lambda_q1'][l], p['lambda_k1'][l], p['lambda_q2'][l], p['lambda_k2'][l]]).astype(F32),
        g_da_out=row(p['g_da_out'][l]),
        conv_w=p['conv_w'][l].astype(F32), conv_b=row(p['conv_b'][l]),
        a_log=jnp.pad(row(p['a_log'][l]), ((0, 0), (0, 128 - SSM_HEADS))),
        d_skip=row(jnp.repeat(p['d_skip'][l], SSM_P)), g_ssm_out=row(p['g_ssm_out'][l]),
        w_br_attn=p['w_br_attn'][l].astype(BF16), w_br_ssm=p['w_br_ssm'][l].astype(BF16),
        w_br_cross=p['w_br_cross'][l].astype(BF16), w_out=p['w_out'][l].astype(BF16),
        g_mlp=row(p['g_mlp'][l]), w_up=p['w_up'][l].astype(BF16), w_down=p['w_down'][l].astype(BF16),
        g_mem=row(p['g_mem'][l]), w_mem_kv=p['w_mem_kv'][l].astype(BF16), g_ca_k=row(p['g_ca_k'][l]),
    )


def _seg64_matrix():
    i = jnp.arange(TN) // DK
    return jnp.where(i[:, None] == i[None, :], 1.0 / DK, 0.0).astype(BF16)


def _layer(x, bsz, w, s64, attn_fn, mem_k3, mem_v3, h0, c0, act_dtype):
    n = x.shape[0]
    t = n // bsz
    ub, k32, v32, uf, dt = _proj(x, w['g_mix'], w['w_main'], w['w_dt'], w['dt_bias'],
                                 w['gq'], w['gk'], w['gcq'], s64)
    ub3 = ub.reshape(bsz, t, -1)
    o_a = attn_fn(ub3, k32, v32)
    uf3 = uf.reshape(bsz, t, -1)
    y_n, h_new, c_new = _ssd(uf3, 0, uf3, ((PB_Z - PB_XBC) * TN) // SSM_INNER, dt.reshape(bsz, t, 128), h0, c0,
                             w['conv_w'], w['conv_b'], w['a_log'], w['d_skip'], w['g_ssm_out'], act_dtype)
    if act_dtype == BF16:
        cq3, cq_blk = ub3, (PB_CQ * TN) // D
    else:
        cq3, cq_blk = ub3[:, :, PB_CQ * TN:PB_K * TN].astype(F32), 0
    o_c = _cross(cq3, cq_blk, mem_k3, mem_v3, act_dtype)
    x = _merge(x, o_a.reshape(n, D), y_n.reshape(n, SSM_INNER), o_c.reshape(n, D), uf,
               w['w_br_attn'], w['w_br_ssm'], w['w_br_cross'], w['w_out'])
    x = _mlp(x, w['g_mlp'], w['w_up'], w['w_down'])
    return x, k32, v32, h_new, c_new


def kernel(x_prompt, x_sample, cache_k, cache_v, cache_mem_k, cache_mem_v, state_ssm, state_conv, page_table, mem_prompt, rel_bias_table, g_mix, w_in, g_da_q, g_da_k, lambda_q1, lambda_k1, lambda_q2, lambda_k2, g_da_out, w_br_attn, conv_w, conv_b, dt_bias, a_log, d_skip, g_ssm_out, w_br_ssm, g_mem, w_mem_kv, g_ca_q, g_ca_k, w_br_cross, w_out, g_mlp, w_up, w_down):
    p = dict(g_mix=g_mix, w_in=w_in, g_da_q=g_da_q, g_da_k=g_da_k, lambda_q1=lambda_q1, lambda_k1=lambda_k1,
             lambda_q2=lambda_q2, lambda_k2=lambda_k2, g_da_out=g_da_out, w_br_attn=w_br_attn, conv_w=conv_w,
             conv_b=conv_b, dt_bias=dt_bias, a_log=a_log, d_skip=d_skip, g_ssm_out=g_ssm_out, w_br_ssm=w_br_ssm,
             g_mem=g_mem, w_mem_kv=w_mem_kv, g_ca_q=g_ca_q, g_ca_k=g_ca_k, w_br_cross=w_br_cross, w_out=w_out,
             g_mlp=g_mlp, w_up=w_up, w_down=w_down)
    depth = w_in.shape[0]
    bp, tp, _ = x_prompt.shape
    bs, ts, _ = x_sample.shape
    n_phys = cache_k.shape[1]
    past_len = page_table.shape[1] * PAGE
    tq = min(256, tp)
    table = rel_bias_table.astype(F32)
    s64 = _seg64_matrix()

    bias_p = jnp.stack([_bias_tiles(table, 0, (tq, tq)), _bias_tiles(table, tq, (tq, tq))], axis=1)
    as_heads = lambda a: a.reshape(N_HEADS, 2 * ts, PAGE)
    bias_past = as_heads(_bias_tiles(table, PAGE, (ts, PAGE)))
    bias_new = as_heads(_bias_tiles(table, 0, (ts, PAGE)))

    cache_k4 = cache_k.reshape(depth, n_phys, PAGE, D)
    cache_v4 = cache_v.reshape(depth, n_phys, PAGE, D)
    mem_flat = mem_prompt.reshape(bp * MEM_LEN, D)
    zero_h = jnp.zeros((bp, N_PAIRS, 128, SSM_N), F32)
    zero_c = jnp.zeros((bp, CONV_K - 1, CONV_CH), F32)

    xp = x_prompt.reshape(bp * tp, D)
    xs = x_sample.reshape(bs * ts, D)
    outs = {k: [] for k in ('kp', 'vp', 'mk', 'mv', 'hp', 'cp', 'ks', 'vs', 'hs', 'cs')}
    for l in range(depth):
        w = _layer_weights(l, p)
        lam0 = _lambda_init(l)

        mk, mv = _memkv(mem_flat, w['g_mem'], w['w_mem_kv'], w['g_ca_k'])
        attn_p = lambda ub3, k32, v32: _attn_prompt(ub3, w['lam'], bias_p, w['g_da_out'], lam0=lam0, tq=tq)
        xp, k, v, hn, cn = _layer(xp, bp, w, s64, attn_p, mk.reshape(bp, MEM_LEN, D), mv.reshape(bp, MEM_LEN, D),
                                  zero_h, zero_c, BF16)
        for key, val in zip(('kp', 'vp', 'mk', 'mv', 'hp', 'cp'), (k, v, mk, mv, hn, cn)):
            outs[key].append(val)

        def attn_s(ub3, k32, v32):
            q3 = ub3[:, :, 0:D].astype(F32)
            return _attn_sample(page_table, q3, k32.reshape(bs, ts, D), v32.reshape(bs, ts, D), cache_k4, cache_v4,
                                l, bias_past, bias_new, w['lam'], w['g_da_out'], lam0=lam0)

        xs, k, v, hn, cn = _layer(xs, bs, w, s64, attn_s, cache_mem_k[l].reshape(bs, MEM_LEN, D),
                                  cache_mem_v[l].reshape(bs, MEM_LEN, D),
                                  state_ssm[l].reshape(bs, N_PAIRS, 128, SSM_N), state_conv[l], F32)
        for key, val in zip(('ks', 'vs', 'hs', 'cs'), (k, v, hn, cn)):
            outs[key].append(val)

    st = lambda key, shape: jnp.stack(outs[key]).reshape((depth,) + shape)
    return (xp.reshape(bp, tp, D), xs.reshape(bs, ts, D),
            st('kp', (bp, tp, N_MAPS, DK)), st('vp', (bp, tp, N_HEADS, DV)),
            st('mk', (bp, MEM_LEN, CA_HEADS, CA_D)), st('mv', (bp, MEM_LEN, CA_HEADS, CA_D)),
            st('hp', (bp, SSM_HEADS, SSM_P, SSM_N)), st('cp', (bp, CONV_K - 1, CONV_CH)),
            st('ks', (bs, ts, N_MAPS, DK)), st('vs', (bs, ts, N_HEADS, DV)),
            st('hs', (bs, SSM_HEADS, SSM_P, SSM_N)), st('cs', (bs, CONV_K - 1, CONV_CH)))
```

```python
import functools
import math

import jax
import jax.numpy as jnp
from jax import lax
from jax.experimental import pallas as pl
from jax.experimental.pallas import tpu as pltpu

F32, BF16 = jnp.float32, jnp.bfloat16
EPS = 1e-5
NEG = -1e30

D = 1024
N_MAPS, DK = 16, 64
N_HEADS, DV = 8, 128
SSM_INNER, SSM_P, SSM_N, SSM_G = 2048, 64, 128, 8
SSM_HEADS = SSM_INNER // SSM_P
N_PAIRS = SSM_HEADS // 2
CONV_K, CONV_CH = 4, 4096
MEM_LEN, CA_HEADS, CA_D = 256, 4, 256
D_FF = 4096
PAGE = 128
REL_BUCKETS, REL_MAX_DIST = 32, 128
SSM_L = 128

OFF_Q, OFF_K, OFF_V, OFF_Z, OFF_XBC = 0, 1024, 2048, 3072, 5120
OFF_DT = OFF_XBC + CONV_CH
OFF_CQ = OFF_DT + SSM_HEADS
OFF_G = OFF_CQ + D

TN = 512
PB_Q, PB_CQ, PB_K, PB_V, PB_XBC, PB_Z, PB_G, PB_END = 0, 2, 4, 6, 8, 16, 20, 26
VMEM_LIMIT = 48 << 20


def _params(n_axes, vmem=VMEM_LIMIT):
    return pltpu.CompilerParams(dimension_semantics=("arbitrary",) * n_axes, vmem_limit_bytes=vmem)


def _rms(x, gain):
    ms = jnp.mean(x * x, axis=-1, keepdims=True)
    return x * lax.rsqrt(ms + EPS) * gain


def _sigmoid(x):
    return 1.0 / (1.0 + jnp.exp(-x))


def _softplus(x):
    return jnp.maximum(x, 0.0) + jnp.log1p(jnp.exp(-jnp.abs(x)))


def _dot(a, b):
    return jnp.dot(a, b, preferred_element_type=F32)


def _dot_nt(a, b):
    return lax.dot_general(a, b, (((1,), (1,)), ((), ())), preferred_element_type=F32)


def _dot_tn(a, b):
    return lax.dot_general(a, b, (((0,), (0,)), ((), ())), preferred_element_type=F32)


def _dot_split3(a01, x):
    hi = x.astype(BF16)
    r = x - hi.astype(F32)
    mid = r.astype(BF16)
    lo = (r - mid.astype(F32)).astype(BF16)
    return _dot(a01, hi) + _dot(a01, mid) + _dot(a01, lo)


def _bucket_thresholds():
    max_exact = REL_BUCKETS // 2
    ratio = REL_MAX_DIST / max_exact
    return [math.ceil(max_exact * ratio ** (k / (REL_BUCKETS - max_exact))) for k in range(1, REL_BUCKETS - max_exact)]


def _bias_kernel(tab_ref, o_ref, *, off, shape):
    m = pl.program_id(0)
    r = lax.broadcasted_iota(jnp.int32, shape, 0)
    c = lax.broadcasted_iota(jnp.int32, shape, 1)
    n = off + r - c
    nn = jnp.maximum(n, 0)
    max_exact = REL_BUCKETS // 2
    large = jnp.full(shape, max_exact, jnp.int32)
    for t in _bucket_thresholds():
        large = large + (nn >= t).astype(jnp.int32)
    bucket = jnp.where(nn < max_exact, nn, large)
    val = jnp.zeros(shape, F32)
    for b in range(REL_BUCKETS):
        val = jnp.where(bucket == b, tab_ref[b, m], val)
    val = val - tab_ref[REL_BUCKETS - 1, m]
    o_ref[0] = jnp.where(n >= 0, val, NEG)


def _bias_tiles(table, off, shape):
    return pl.pallas_call(
        functools.partial(_bias_kernel, off=off, shape=shape),
        out_shape=jax.ShapeDtypeStruct((N_MAPS,) + shape, F32),
        grid=(N_MAPS,),
        in_specs=[pl.BlockSpec(memory_space=pltpu.SMEM)],
        out_specs=pl.BlockSpec((1,) + shape, lambda m: (m, 0, 0)),
        compiler_params=_params(1),
        name="rel_bias_tiles",
    )(table)


def _proj_kernel(x_ref, g_ref, w_ref, wdt_ref, dtb_ref, gq_ref, gk_ref, gcq_ref, s64_ref,
                 ub_ref, k32_ref, v32_ref, uf_ref, dt_ref, h_scr):
    j = pl.program_id(1)

    @pl.when(j == 0)
    def _():
        h = _rms(x_ref[...], g_ref[...]).astype(BF16)
        h_scr[...] = h
        dt_ref[...] = _softplus(_dot(h, wdt_ref[...]) + dtb_ref[...])

    def proj():
        return _dot(h_scr[...], w_ref[...])

    def seg64_norm(u, gain):
        ms = _dot((u * u).astype(BF16), s64_ref[...])
        return u * lax.rsqrt(ms + EPS) * gain

    @pl.when(j < PB_CQ)
    def _():
        ub_ref[...] = (seg64_norm(proj(), gq_ref[...]) * (DK ** -0.5)).astype(BF16)

    @pl.when(jnp.logical_and(j >= PB_CQ, j < PB_K))
    def _():
        u = proj()
        for s in range(TN // CA_D):
            seg = u[:, s * CA_D:(s + 1) * CA_D]
            ub_ref[:, s * CA_D:(s + 1) * CA_D] = (_rms(seg, gcq_ref[...]) * (CA_D ** -0.5)).astype(BF16)

    @pl.when(jnp.logical_and(j >= PB_K, j < PB_V))
    def _():
        kn = seg64_norm(proj(), gk_ref[...])
        k32_ref[...] = kn
        ub_ref[...] = kn.astype(BF16)

    @pl.when(jnp.logical_and(j >= PB_V, j < PB_XBC))
    def _():
        u = proj()
        v32_ref[...] = u
        ub_ref[...] = u.astype(BF16)

    @pl.when(jnp.logical_and(j >= PB_XBC, j < PB_G))
    def _():
        uf_ref[...] = proj()

    @pl.when(j >= PB_G)
    def _():
        uf_ref[...] = _sigmoid(proj())


def _proj(x, g, w, wdt, dtb, gq, gk, gcq, s64):
    n = x.shape[0]
    tm = min(1024, n)
    nb_b = PB_XBC
    nb_f = PB_END - PB_XBC
    row = lambda i, j: (i, 0)
    return pl.pallas_call(
        _proj_kernel,
        out_shape=(jax.ShapeDtypeStruct((n, nb_b * TN), BF16),
                   jax.ShapeDtypeStruct((n, D), F32),
                   jax.ShapeDtypeStruct((n, D), F32),
                   jax.ShapeDtypeStruct((n, nb_f * TN), F32),
                   jax.ShapeDtypeStruct((n, 128), F32)),
        grid=(n // tm, PB_END),
        in_specs=[pl.BlockSpec((tm, D), row),
                  pl.BlockSpec((1, D), lambda i, j: (0, 0)),
                  pl.BlockSpec((D, TN), lambda i, j: (0, j)),
                  pl.BlockSpec((D, 128), lambda i, j: (0, 0)),
                  pl.BlockSpec((1, 128), lambda i, j: (0, 0)),
                  pl.BlockSpec((1, TN), lambda i, j: (0, 0)),
                  pl.BlockSpec((1, TN), lambda i, j: (0, 0)),
                  pl.BlockSpec((1, CA_D), lambda i, j: (0, 0)),
                  pl.BlockSpec((TN, TN), lambda i, j: (0, 0))],
        out_specs=(pl.BlockSpec((tm, TN), lambda i, j: (i, jnp.minimum(j, nb_b - 1))),
                   pl.BlockSpec((tm, TN), lambda i, j: (i, jnp.clip(j - PB_K, 0, 1))),
                   pl.BlockSpec((tm, TN), lambda i, j: (i, jnp.clip(j - PB_V, 0, 1))),
                   pl.BlockSpec((tm, TN), lambda i, j: (i, jnp.clip(j - PB_XBC, 0, nb_f - 1))),
                   pl.BlockSpec((tm, 128), row)),
        scratch_shapes=[pltpu.VMEM((tm, D), BF16)],
        compiler_params=_params(2),
        name="in_proj",
    )(x, g, w, wdt, dtb, gq, gk, gcq, s64)


def _memkv_kernel(m_ref, g_ref, w_ref, gk_ref, mk_ref, mv_ref, h_scr):
    j = pl.program_id(1)

    @pl.when(j == 0)
    def _():
        h_scr[...] = _rms(m_ref[...], g_ref[...]).astype(BF16)

    u = _dot(h_scr[...], w_ref[...])

    @pl.when(j == 0)
    def _():
        for s in range(CA_HEADS):
            mk_ref[:, s * CA_D:(s + 1) * CA_D] = _rms(u[:, s * CA_D:(s + 1) * CA_D], gk_ref[...])

    @pl.when(j == 1)
    def _():
        mv_ref[...] = u


def _memkv(mem, g, w, gk):
    n = mem.shape[0]
    tm = min(512, n)
    return pl.pallas_call(
        _memkv_kernel,
        out_shape=(jax.ShapeDtypeStruct((n, D), F32), jax.ShapeDtypeStruct((n, D), F32)),
        grid=(n // tm, 2),
        in_specs=[pl.BlockSpec((tm, D), lambda i, j: (i, 0)),
                  pl.BlockSpec((1, D), lambda i, j: (0, 0)),
                  pl.BlockSpec((D, D), lambda i, j: (0, j)),
                  pl.BlockSpec((1, CA_D), lambda i, j: (0, 0))],
        out_specs=(pl.BlockSpec((tm, D), lambda i, j: (i, 0)), pl.BlockSpec((tm, D), lambda i, j: (i, 0))),
        scratch_shapes=[pltpu.VMEM((tm, D), BF16)],
        compiler_params=_params(2),
        name="mem_kv",
    )(mem, g, w, gk)


def _lambda(lam_ref, lam0):
    lv = lam_ref[...]
    s1 = jnp.sum(lv[0:1] * lv[1:2], axis=-1, keepdims=True)
    s2 = jnp.sum(lv[2:3] * lv[3:4], axis=-1, keepdims=True)
    return jnp.exp(s1) - jnp.exp(s2) + lam0


def _softmax_step(state, s, v_bf16):
    m, l, acc = state
    m_new = jnp.maximum(m, jnp.max(s, axis=-1, keepdims=True))
    alpha = jnp.exp(m - m_new)
    p = jnp.exp(s - m_new)
    l = alpha * l + jnp.sum(p, axis=-1, keepdims=True)
    acc = alpha * acc + _dot(p.astype(BF16), v_bf16)
    return m_new, l, acc


def _attn_prompt_kernel(lam_ref, q_ref, k_ref, v_ref, bias_ref, gout_ref, o_ref, *, tq, lam0):
    qi = pl.program_id(2)
    q = q_ref[...]
    lane = lax.broadcasted_iota(jnp.int32, q.shape, 1)
    qm = (jnp.where(lane < DK, q, jnp.zeros_like(q)), jnp.where(lane >= DK, q, jnp.zeros_like(q)))

    def block(kb, states, bias_idx):
        start = pl.multiple_of(kb * tq, tq)
        k = k_ref[pl.ds(start, tq), :]
        v = v_ref[pl.ds(start, tq), :]
        out = []
        for mi in range(2):
            s = _dot_nt(qm[mi], k)
            if bias_idx is not None:
                s = s + bias_ref[mi, bias_idx]
            out.append(_softmax_step(states[mi], s, v))
        return tuple(out)

    init = tuple((jnp.full((tq, 1), NEG, F32), jnp.zeros((tq, 1), F32), jnp.zeros((tq, DV), F32))
                 for _ in range(2))
    states = lax.fori_loop(0, qi - 1, lambda kb, st: block(kb, st, None), init)
    states = lax.fori_loop(jnp.maximum(qi - 1, 0), qi, lambda kb, st: block(kb, st, 1), states)
    states = block(qi, states, 0)

    lam = _lambda(lam_ref, lam0)
    (_, l0, a0), (_, l1, a1) = states
    o = a0 / l0 - lam * (a1 / l1)
    o_ref[...] = (_rms(o, gout_ref[...]) * (1.0 - lam0)).astype(o_ref.dtype)


def _attn_prompt(ub3, lam_vecs, bias, gout, *, lam0, tq):
    b, t, _ = ub3.shape
    kcol, vcol = (PB_K * TN) // DV, (PB_V * TN) // DV
    return pl.pallas_call(
        functools.partial(_attn_prompt_kernel, tq=tq, lam0=lam0),
        out_shape=jax.ShapeDtypeStruct((b, t, D), BF16),
        grid=(b, N_HEADS, t // tq),
        in_specs=[pl.BlockSpec((4, DK), lambda b, h, i: (0, 0)),
                  pl.BlockSpec((None, tq, DV), lambda b, h, i: (b, i, h)),
                  pl.BlockSpec((None, t, DV), lambda b, h, i: (b, 0, kcol + h)),
                  pl.BlockSpec((None, t, DV), lambda b, h, i: (b, 0, vcol + h)),
                  pl.BlockSpec((2, 2, tq, tq), lambda b, h, i: (h, 0, 0, 0)),
                  pl.BlockSpec((1, DV), lambda b, h, i: (0, 0))],
        out_specs=pl.BlockSpec((None, tq, DV), lambda b, h, i: (b, i, h)),
        compiler_params=_params(3),
        name="diff_attn_prompt",
    )(lam_vecs, ub3, ub3, ub3, bias, gout)


def _head_queries(q, h):
    qh = q[:, h * DV:(h + 1) * DV]
    lane = lax.broadcasted_iota(jnp.int32, qh.shape, 1)
    return jnp.concatenate([jnp.where(lane < DK, qh, 0.0), jnp.where(lane >= DK, qh, 0.0)], axis=0).astype(BF16)


def _attn_sample_kernel(pt_ref, q_ref, kn_ref, vn_ref, bpast_ref, bnew_ref, lam_ref, gout_ref, *rest, npg, lam0):
    k_refs, v_refs = rest[:npg], rest[npg:2 * npg]
    o_ref, m_scr, l_scr, acc_scr = rest[2 * npg:]
    c = pl.program_id(1)
    last = c == pl.num_programs(1) - 1
    tdec = q_ref.shape[0]
    rows = 2 * tdec

    @pl.when(c == 0)
    def _():
        m_scr[...] = jnp.full(m_scr.shape, NEG, F32)
        l_scr[...] = jnp.zeros(l_scr.shape, F32)
        acc_scr[...] = jnp.zeros(acc_scr.shape, F32)

    q = q_ref[...]
    qms = [_head_queries(q, h) for h in range(N_HEADS)]

    def update(s, pv_fn):
        m_old = m_scr[...]
        m_new = jnp.maximum(m_old, jnp.max(s, axis=-1, keepdims=True))
        alpha = jnp.exp(m_old - m_new)
        p = jnp.exp(s - m_new[:, 0:1])
        l_scr[...] = alpha * l_scr[...] + jnp.sum(p, axis=-1, keepdims=True)
        acc_scr[...] = alpha * acc_scr[...] + pv_fn(p.astype(BF16))
        m_scr[...] = m_new

    near = last.astype(F32)
    s_pages = []
    for i in range(npg):
        s_i = jnp.concatenate([_dot(qms[h], k_refs[i][h * DV:(h + 1) * DV, :].astype(BF16))
                               for h in range(N_HEADS)], axis=0)
        if i == npg - 1:
            s_i = s_i + near * bpast_ref[...]
        s_pages.append(s_i)

    def pv_pages(p):
        heads = []
        for h in range(N_HEADS):
            out = None
            for i in range(npg):
                v_h = v_refs[i][pl.ds(h, PAGE, stride=N_HEADS), :].astype(BF16)
                part = _dot(p[h * rows:(h + 1) * rows, i * PAGE:(i + 1) * PAGE], v_h)
                out = part if out is None else out + part
            heads.append(out)
        return jnp.concatenate(heads, axis=0)

    update(jnp.concatenate(s_pages, axis=1), pv_pages)

    @pl.when(last)
    def _():
        pad = jnp.zeros((PAGE - tdec, D), F32)
        kn = jnp.concatenate([kn_ref[...], pad], axis=0).astype(BF16)
        vn = jnp.concatenate([vn_ref[...], pad], axis=0).astype(BF16)
        s_new = jnp.concatenate([_dot_nt(qms[h], kn[:, h * DV:(h + 1) * DV]) for h in range(N_HEADS)], axis=0)
        update(s_new + bnew_ref[...],
               lambda p: jnp.concatenate([_dot(p[h * rows:(h + 1) * rows], vn[:, h * DV:(h + 1) * DV])
                                          for h in range(N_HEADS)], axis=0))
        lam = _lambda(lam_ref, lam0)
        o_all = acc_scr[...] / l_scr[...]
        for h in range(N_HEADS):
            o = o_all[h * rows:h * rows + tdec] - lam * o_all[h * rows + tdec:(h + 1) * rows]
            o_ref[:, h * DV:(h + 1) * DV] = _rms(o, gout_ref[...]) * (1.0 - lam0)


def _attn_sample(page_table, q3, kn3, vn3, cache_k4, cache_v4, layer, bpast, bnew, lam_vecs, gout, *, lam0):
    bs, tdec, _ = q3.shape
    n_pages = page_table.shape[1]
    npg = math.gcd(n_pages, 8)
    pt = page_table.reshape(-1)

    def page_spec(i):
        return pl.BlockSpec((None, None, D, PAGE),
                            lambda b, c, pt_ref: (layer, pt_ref[b * n_pages + c * npg + i], 0, 0))

    per_b = pl.BlockSpec((None, tdec, D), lambda b, c, pt_ref: (b, 0, 0))
    whole = lambda shape: pl.BlockSpec(shape, lambda b, c, pt_ref: (0,) * len(shape))
    stat = pltpu.VMEM((N_HEADS * 2 * tdec, DV), F32)
    return pl.pallas_call(
        functools.partial(_attn_sample_kernel, npg=npg, lam0=lam0),
        out_shape=jax.ShapeDtypeStruct((bs, tdec, D), F32),
        grid_spec=pltpu.PrefetchScalarGridSpec(
            num_scalar_prefetch=1,
            grid=(bs, n_pages // npg),
            in_specs=[per_b, per_b, per_b,
                      whole((N_HEADS * 2 * tdec, PAGE)), whole((N_HEADS * 2 * tdec, PAGE)),
                      whole((4, DK)), whole((1, DV))]
                     + [page_spec(i) for i in range(npg)] + [page_spec(i) for i in range(npg)],
            out_specs=per_b,
            scratch_shapes=[stat, stat, stat]),
        compiler_params=_params(2),
        name="diff_attn_sample",
    )(pt, q3, kn3, vn3, bpast, bnew, lam_vecs, gout, *([cache_k4] * npg), *([cache_v4] * npg))


def _cross_kernel(cq_ref, mk_ref, mv_ref, o_ref):
    rows = cq_ref.shape[0]
    for h in range(CA_HEADS):
        cols = slice(h * CA_D, (h + 1) * CA_D)
        q = cq_ref[:, cols]
        if rows % 16:
            q = jnp.concatenate([q.astype(F32), jnp.zeros((16 - rows % 16, CA_D), F32)], axis=0)
        s = _dot_nt(q.astype(BF16), mk_ref[:, cols].astype(BF16))
        p = jnp.exp(s - jnp.max(s, axis=-1, keepdims=True))
        o = _dot(p.astype(BF16), mv_ref[:, cols].astype(BF16)) / jnp.sum(p, axis=-1, keepdims=True)
        o_ref[:, cols] = o[0:rows].astype(o_ref.dtype)


def _cross(cq3, col_block, mk3, mv3, out_dtype):
    b, t, _ = cq3.shape
    tm = min(512, t)
    return pl.pallas_call(
        _cross_kernel,
        out_shape=jax.ShapeDtypeStruct((b, t, D), out_dtype),
        grid=(b, t // tm),
        in_specs=[pl.BlockSpec((None, tm, D), lambda b, i: (b, i, col_block)),
                  pl.BlockSpec((None, MEM_LEN, D), lambda b, i: (b, 0, 0)),
                  pl.BlockSpec((None, MEM_LEN, D), lambda b, i: (b, 0, 0))],
        out_specs=pl.BlockSpec((None, tm, D), lambda b, i: (b, i, 0)),
        compiler_params=_params(2),
        name="mem_cross_attn",
    )(cq3, mk3, mv3)


def _ssd_kernel(xbc_ref, z_ref, dt_ref, h0_ref, c0_ref, cw_ref, cb_ref, alog_ref, dskip_ref, gout_ref,
                y_ref, hn_ref, cn_ref, buf, xc_scr, dt_scr, yb, h_scr, *, tv):
    L = SSM_L
    c = pl.program_id(1)
    head = 8

    @pl.when(c == 0)
    def _():
        buf[0:head, :] = jnp.zeros((head, CONV_CH), F32)
        buf[head - (CONV_K - 1):head, :] = c0_ref[...]
        h_scr[...] = h0_ref[...]
        if tv < L:
            buf[head + tv:, :] = jnp.zeros((L - tv, CONV_CH), F32)
            dt_scr[...] = jnp.zeros(dt_scr.shape, F32)

    buf[head:head + tv, :] = xbc_ref[...]
    cn_ref[...] = buf[head + tv - (CONV_K - 1):head + tv, :]

    cw = cw_ref[...]
    for j in range(CONV_CH // 512):
        cols = slice(j * 512, (j + 1) * 512)
        acc = cb_ref[:, cols] + cw[CONV_K - 1:CONV_K, cols] * buf[head:head + L, cols]
        for k in range(CONV_K - 1):
            acc = acc + cw[k:k + 1, cols] * buf[head - (CONV_K - 1) + k:head - (CONV_K - 1) + k + L, cols]
        xc_scr[:, cols] = acc * _sigmoid(acc)

    if tv == L:
        buf[0:head, :] = buf[L:L + head, :]
        dt = dt_ref[...]
    else:
        dt_scr[0:tv, :] = dt_ref[...]
        dt = dt_scr[...]

    a_neg = -jnp.exp(alog_ref[...])
    d_a = dt * a_neg
    row = lax.broadcasted_iota(jnp.int32, (L, L), 0)
    col = lax.broadcasted_iota(jnp.int32, (L, L), 1)
    causal = col <= row
    a_cum = _dot_split3(causal.astype(BF16), d_a)
    a_cum_t = a_cum.T
    dt_t = dt.T
    a_last = a_cum[L - 1:L, :]
    e_cum = jnp.exp(a_cum)
    to_end = jnp.exp(a_last - a_cum) * dt
    e_last = jnp.exp(a_last)
    lane = lax.broadcasted_iota(jnp.int32, (L, 128), 1)
    first = lane < SSM_P
    row_first = lax.broadcasted_iota(jnp.int32, (128, 128), 0) < SSM_P

    for g in range(SSM_G):
        b_g = xc_scr[:, SSM_INNER + g * SSM_N:SSM_INNER + (g + 1) * SSM_N].astype(BF16)
        c_g = xc_scr[:, SSM_INNER + SSM_G * SSM_N + g * SSM_N:SSM_INNER + SSM_G * SSM_N + (g + 1) * SSM_N].astype(BF16)
        cb = _dot_nt(c_g, b_g)
        for jj in (2 * g, 2 * g + 1):
            cols = slice(jj * 128, (jj + 1) * 128)
            x_pair = xc_scr[:, cols]
            x_bf = x_pair.astype(BF16)
            yw = []
            for hh in (2 * jj, 2 * jj + 1):
                seg = a_cum[:, hh:hh + 1] - a_cum_t[hh:hh + 1, :]
                w = cb * jnp.exp(jnp.where(causal, seg, NEG)) * dt_t[hh:hh + 1, :]
                yw.append(_dot(w.astype(BF16), x_bf))
            ha, hb = 2 * jj, 2 * jj + 1
            hp = h_scr[jj]
            y = jnp.where(first, yw[0], yw[1])
            y = y + _dot_nt(c_g, hp.astype(BF16)) * jnp.where(first, e_cum[:, ha:ha + 1], e_cum[:, hb:hb + 1])
            y = y + dskip_ref[:, cols] * x_pair
            te = jnp.where(first, to_end[:, ha:ha + 1], to_end[:, hb:hb + 1])
            upd = _dot_tn((x_pair * te).astype(BF16), b_g)
            h_scr[jj] = hp * jnp.where(row_first, e_last[:, ha:ha + 1], e_last[:, hb:hb + 1]) + upd
            yb[:, cols] = y[0:tv] * (z_ref[:, cols] * _sigmoid(z_ref[:, cols]))

    gw = SSM_INNER // SSM_G
    for g in range(SSM_G):
        cols = slice(g * gw, (g + 1) * gw)
        y_ref[:, cols] = _rms(yb[:, cols], gout_ref[:, cols]).astype(y_ref.dtype)
    hn_ref[...] = h_scr[...]


def _ssd(xbc_src, xbc_blk, z_src, z_blk, dt3, h0, c0, cw, cb, alog, dskip, gout, out_dtype):
    b, t, _ = dt3.shape
    tv = min(SSM_L, t)
    nc = t // tv
    seq = lambda width, blk: pl.BlockSpec((None, tv, width), lambda b, c: (b, c, blk))
    per_b = lambda shape: pl.BlockSpec((None,) + shape, lambda b, c: (b,) + (0,) * len(shape))
    whole = lambda shape: pl.BlockSpec(shape, lambda b, c: (0,) * len(shape))
    return pl.pallas_call(
        functools.partial(_ssd_kernel, tv=tv),
        out_shape=(jax.ShapeDtypeStruct((b, t, SSM_INNER), out_dtype),
                   jax.ShapeDtypeStruct((b, N_PAIRS, 128, SSM_N), F32),
                   jax.ShapeDtypeStruct((b, CONV_K - 1, CONV_CH), F32)),
        grid=(b, nc),
        in_specs=[seq(CONV_CH, xbc_blk), seq(SSM_INNER, z_blk), seq(128, 0),
                  per_b((N_PAIRS, 128, SSM_N)), per_b((CONV_K - 1, CONV_CH)),
                  whole((CONV_K, CONV_CH)), whole((1, CONV_CH)), whole((1, 128)),
                  whole((1, SSM_INNER)), whole((1, SSM_INNER))],
        out_specs=(seq(SSM_INNER, 0), per_b((N_PAIRS, 128, SSM_N)), per_b((CONV_K - 1, CONV_CH))),
        scratch_shapes=[pltpu.VMEM((SSM_L + 8, CONV_CH), F32), pltpu.VMEM((SSM_L, CONV_CH), F32),
                        pltpu.VMEM((SSM_L, 128), F32), pltpu.VMEM((tv, SSM_INNER), F32),
                        pltpu.VMEM((N_PAIRS, 128, SSM_N), F32)],
        compiler_params=_params(2),
        name="conv_ssd",
    )(xbc_src, z_src, dt3, h0, c0, cw, cb, alog, dskip, gout)


def _merge_kernel(x_ref, oa_ref, ys_ref, oc_ref, g_ref, wa_ref, ws_ref, wc_ref, wo_ref, o_ref):
    ba = _dot(oa_ref[...].astype(BF16), wa_ref[...])
    bs = _dot(ys_ref[...].astype(BF16), ws_ref[...])
    bc = _dot(oc_ref[...].astype(BF16), wc_ref[...])
    merged = g_ref[:, 0:D] * ba + g_ref[:, D:2 * D] * bs + g_ref[:, 2 * D:3 * D] * bc
    o_ref[...] = x_ref[...] + _dot(merged.astype(BF16), wo_ref[...])


def _merge(x, oa, ys, oc, uf, wa, ws, wc, wo):
    n = x.shape[0]
    tm = min(256, n)
    gate_blk = ((PB_G - PB_XBC) * TN) // (3 * D)
    row = lambda width: pl.BlockSpec((tm, width), lambda i: (i, 0))
    whole = lambda shape: pl.BlockSpec(shape, lambda i: (0, 0))
    return pl.pallas_call(
        _merge_kernel,
        out_shape=jax.ShapeDtypeStruct((n, D), F32),
        grid=(n // tm,),
        in_specs=[row(D), row(D), row(SSM_INNER), row(D),
                  pl.BlockSpec((tm, 3 * D), lambda i: (i, gate_blk)),
                  whole((D, D)), whole((SSM_INNER, D)), whole((D, D)), whole((D, D))],
        out_specs=row(D),
        compiler_params=_params(1),
        name="branch_merge",
    )(x, oa, ys, oc, uf, wa, ws, wc, wo)


def _mlp_kernel(x_ref, g_ref, wu_ref, wd_ref, o_ref, h_scr, acc):
    f = pl.program_id(1)

    @pl.when(f == 0)
    def _():
        x = x_ref[...]
        h_scr[...] = _rms(x, g_ref[...]).astype(BF16)
        acc[...] = x

    a = jnp.maximum(_dot(h_scr[...], wu_ref[...]), 0.0)
    acc[...] += _dot((a * a).astype(BF16), wd_ref[...])

    @pl.when(f == pl.num_programs(1) - 1)
    def _():
        o_ref[...] = acc[...]


def _mlp(x, g, wu, wd):
    n = x.shape[0]
    tm = min(1024, n)
    tf = 1024
    return pl.pallas_call(
        _mlp_kernel,
        out_shape=jax.ShapeDtypeStruct((n, D), F32),
        grid=(n // tm, D_FF // tf),
        in_specs=[pl.BlockSpec((tm, D), lambda i, f: (i, 0)),
                  pl.BlockSpec((1, D), lambda i, f: (0, 0)),
                  pl.BlockSpec((D, tf), lambda i, f: (0, f)),
                  pl.BlockSpec((tf, D), lambda i, f: (f, 0))],
        out_specs=pl.BlockSpec((tm, D), lambda i, f: (i, 0)),
        scratch_shapes=[pltpu.VMEM((tm, D), BF16), pltpu.VMEM((tm, D), F32)],
        compiler_params=_params(2),
        name="relu2_mlp",
    )(x, g, wu, wd)


def _lambda_init(layer):
    return 0.8 - 0.6 * math.exp(-0.3 * layer)


def _layer_weights(l, p):
    w_in = p['w_in'][l]
    cols = lambda off, width: w_in[:, off:off + width]
    w_main = jnp.concatenate([cols(OFF_Q, D), cols(OFF_CQ, D), cols(OFF_K, D), cols(OFF_V, D),
                              cols(OFF_XBC, CONV_CH), cols(OFF_Z, SSM_INNER), cols(OFF_G, 3 * D)],
                             axis=1).astype(BF16)
    w_dt = jnp.pad(cols(OFF_DT, SSM_HEADS), ((0, 0), (0, 128 - SSM_HEADS))).astype(BF16)
    row = lambda v: v.reshape(1, -1).astype(F32)
    return dict(
        g_mix=row(p['g_mix'][l]), w_main=w_main, w_dt=w_dt,
        dt_bias=jnp.pad(row(p['dt_bias'][l]), ((0, 0), (0, 128 - SSM_HEADS))),
        gq=row(jnp.tile(p['g_da_q'][l], TN // DK)), gk=row(jnp.tile(p['g_da_k'][l], TN // DK)),
        gcq=row(p['g_ca_q'][l]),
        lam=jnp.stack([p['lambda_q1'][l], p['lambda_k1'][l], p['lambda_q2'][l], p['lambda_k2'][l]]).astype(F32),
        g_da_out=row(p['g_da_out'][l]),
        conv_w=p['conv_w'][l].astype(F32), conv_b=row(p['conv_b'][l]),
        a_log=jnp.pad(row(p['a_log'][l]), ((0, 0), (0, 128 - SSM_HEADS))),
        d_skip=row(jnp.repeat(p['d_skip'][l], SSM_P)), g_ssm_out=row(p['g_ssm_out'][l]),
        w_br_attn=p['w_br_attn'][l].astype(BF16), w_br_ssm=p['w_br_ssm'][l].astype(BF16),
        w_br_cross=p['w_br_cross'][l].astype(BF16), w_out=p['w_out'][l].astype(BF16),
        g_mlp=row(p['g_mlp'][l]), w_up=p['w_up'][l].astype(BF16), w_down=p['w_down'][l].astype(BF16),
        g_mem=row(p['g_mem'][l]), w_mem_kv=p['w_mem_kv'][l].astype(BF16), g_ca_k=row(p['g_ca_k'][l]),
    )


def _seg64_matrix():
    i = jnp.arange(TN) // DK
    return jnp.where(i[:, None] == i[None, :], 1.0 / DK, 0.0).astype(BF16)


def _layer(x, bsz, w, s64, attn_fn, mem_k3, mem_v3, h0, c0, act_dtype):
    n = x.shape[0]
    t = n // bsz
    ub, k32, v32, uf, dt = _proj(x, w['g_mix'], w['w_main'], w['w_dt'], w['dt_bias'],
                                 w['gq'], w['gk'], w['gcq'], s64)
    ub3 = ub.reshape(bsz, t, -1)
    o_a = attn_fn(ub3, k32, v32)
    uf3 = uf.reshape(bsz, t, -1)
    y_n, h_new, c_new = _ssd(uf3, 0, uf3, ((PB_Z - PB_XBC) * TN) // SSM_INNER, dt.reshape(bsz, t, 128), h0, c0,
                             w['conv_w'], w['conv_b'], w['a_log'], w['d_skip'], w['g_ssm_out'], act_dtype)
    if act_dtype == BF16:
        cq3, cq_blk = ub3, (PB_CQ * TN) // D
    else:
        cq3, cq_blk = ub3[:, :, PB_CQ * TN:PB_K * TN].astype(F32), 0
    o_c = _cross(cq3, cq_blk, mem_k3, mem_v3, act_dtype)
    x = _merge(x, o_a.reshape(n, D), y_n.reshape(n, SSM_INNER), o_c.reshape(n, D), uf,
               w['w_br_attn'], w['w_br_ssm'], w['w_br_cross'], w['w_out'])
    x = _mlp(x, w['g_mlp'], w['w_up'], w['w_down'])
    return x, k32, v32, h_new, c_new


def kernel(x_prompt, x_sample, cache_k, cache_v, cache_mem_k, cache_mem_v, state_ssm, state_conv, page_table, mem_prompt, rel_bias_table, g_mix, w_in, g_da_q, g_da_k, lambda_q1, lambda_k1, lambda_q2, lambda_k2, g_da_out, w_br_attn, conv_w, conv_b, dt_bias, a_log, d_skip, g_ssm_out, w_br_ssm, g_mem, w_mem_kv, g_ca_q, g_ca_k, w_br_cross, w_out, g_mlp, w_up, w_down):
    p = dict(g_mix=g_mix, w_in=w_in, g_da_q=g_da_q, g_da_k=g_da_k, lambda_q1=lambda_q1, lambda_k1=lambda_k1,
             lambda_q2=lambda_q2, lambda_k2=lambda_k2, g_da_out=g_da_out, w_br_attn=w_br_attn, conv_w=conv_w,
             conv_b=conv_b, dt_bias=dt_bias, a_log=a_log, d_skip=d_skip, g_ssm_out=g_ssm_out, w_br_ssm=w_br_ssm,
             g_mem=g_mem, w_mem_kv=w_mem_kv, g_ca_q=g_ca_q, g_ca_k=g_ca_k, w_br_cross=w_br_cross, w_out=w_out,
             g_mlp=g_mlp, w_up=w_up, w_down=w_down)
    depth = w_in.shape[0]
    bp, tp, _ = x_prompt.shape
    bs, ts, _ = x_sample.shape
    n_phys = cache_k.shape[1]
    past_len = page_table.shape[1] * PAGE
    tq = min(512, tp)
    table = rel_bias_table.astype(F32)
    s64 = _seg64_matrix()

    bias_p = jnp.stack([_bias_tiles(table, 0, (tq, tq)), _bias_tiles(table, tq, (tq, tq))], axis=1)
    as_heads = lambda a: a.reshape(N_HEADS * 2 * ts, PAGE)
    bias_past = as_heads(_bias_tiles(table, PAGE, (ts, PAGE)))
    bias_new = as_heads(_bias_tiles(table, 0, (ts, PAGE)))

    cache_k4 = jnp.transpose(cache_k, (0, 1, 3, 4, 2)).reshape(depth, n_phys, D, PAGE)
    cache_v4 = cache_v.reshape(depth, n_phys, PAGE * N_HEADS, DV)
    mem_flat = mem_prompt.reshape(bp * MEM_LEN, D)
    zero_h = jnp.zeros((bp, N_PAIRS, 128, SSM_N), F32)
    zero_c = jnp.zeros((bp, CONV_K - 1, CONV_CH), F32)

    xp = x_prompt.reshape(bp * tp, D)
    xs = x_sample.reshape(bs * ts, D)
    outs = {k: [] for k in ('kp', 'vp', 'mk', 'mv', 'hp', 'cp', 'ks', 'vs', 'hs', 'cs')}
    for l in range(depth):
        w = _layer_weights(l, p)
        lam0 = _lambda_init(l)

        mk, mv = _memkv(mem_flat, w['g_mem'], w['w_mem_kv'], w['g_ca_k'])
        attn_p = lambda ub3, k32, v32: _attn_prompt(ub3, w['lam'], bias_p, w['g_da_out'], lam0=lam0, tq=tq)
        xp, k, v, hn, cn = _layer(xp, bp, w, s64, attn_p, mk.reshape(bp, MEM_LEN, D), mv.reshape(bp, MEM_LEN, D),
                                  zero_h, zero_c, BF16)
        for key, val in zip(('kp', 'vp', 'mk', 'mv', 'hp', 'cp'), (k, v, mk, mv, hn, cn)):
            outs[key].append(val)

        def attn_s(ub3, k32, v32):
            q3 = ub3[:, :, 0:D].astype(F32)
            return _attn_sample(page_table, q3, k32.reshape(bs, ts, D), v32.reshape(bs, ts, D), cache_k4, cache_v4,
                                l, bias_past, bias_new, w['lam'], w['g_da_out'], lam0=lam0)

        xs, k, v, hn, cn = _layer(xs, bs, w, s64, attn_s, cache_mem_k[l].reshape(bs, MEM_LEN, D),
                                  cache_mem_v[l].reshape(bs, MEM_LEN, D),
                                  state_ssm[l].reshape(bs, N_PAIRS, 128, SSM_N), state_conv[l], F32)
        for key, val in zip(('ks', 'vs', 'hs', 'cs'), (k, v, hn, cn)):
            outs[key].append(val)

    st = lambda key, shape: jnp.stack(outs[key]).reshape((depth,) + shape)
    return (xp.reshape(bp, tp, D), xs.reshape(bs, ts, D),
            st('kp', (bp, tp, N_MAPS, DK)), st('vp', (bp, tp, N_HEADS, DV)),
            st('mk', (bp, MEM_LEN, CA_HEADS, CA_D)), st('mv', (bp, MEM_LEN, CA_HEADS, CA_D)),
            st('hp', (bp, SSM_HEADS, SSM_P, SSM_N)), st('cp', (bp, CONV_K - 1, CONV_CH)),
            st('ks', (bs, ts, N_MAPS, DK)), st('vs', (bs, ts, N_HEADS, DV)),
            st('hs', (bs, SSM_HEADS, SSM_P, SSM_N)), st('cs', (bs, CONV_K - 1, CONV_CH)))
```

```python
import functools
import math

import jax
import jax.numpy as jnp
from jax import lax
from jax.experimental import pallas as pl
from jax.experimental.pallas import tpu as pltpu

F32, BF16 = jnp.float32, jnp.bfloat16
EPS = 1e-5
NEG = -1e30

D = 1024
N_MAPS, DK = 16, 64
N_HEADS, DV = 8, 128
SSM_INNER, SSM_P, SSM_N, SSM_G = 2048, 64, 128, 8
SSM_HEADS = SSM_INNER // SSM_P
N_PAIRS = SSM_HEADS // 2
CONV_K, CONV_CH = 4, 4096
MEM_LEN, CA_HEADS, CA_D = 256, 4, 256
D_FF = 4096
PAGE = 128
REL_BUCKETS, REL_MAX_DIST = 32, 128
SSM_L = 128

OFF_Q, OFF_K, OFF_V, OFF_Z, OFF_XBC = 0, 1024, 2048, 3072, 5120
OFF_DT = OFF_XBC + CONV_CH
OFF_CQ = OFF_DT + SSM_HEADS
OFF_G = OFF_CQ + D

TN = 1024
SEG_W = 256
PB_Q, PB_CQ, PB_K, PB_V, PB_XBC, PB_Z, PB_END = 0, 1, 2, 3, 4, 8, 10
VMEM_LIMIT = 48 << 20


def _params(n_axes, vmem=VMEM_LIMIT):
    return pltpu.CompilerParams(dimension_semantics=("arbitrary",) * n_axes, vmem_limit_bytes=vmem)


def _rms(x, gain):
    ms = jnp.mean(x * x, axis=-1, keepdims=True)
    return x * lax.rsqrt(ms + EPS) * gain


def _sigmoid(x):
    return 1.0 / (1.0 + jnp.exp(-x))


def _softplus(x):
    return jnp.maximum(x, 0.0) + jnp.log1p(jnp.exp(-jnp.abs(x)))


def _dot(a, b):
    return jnp.dot(a, b, preferred_element_type=F32)


def _dot_nt(a, b):
    return lax.dot_general(a, b, (((1,), (1,)), ((), ())), preferred_element_type=F32)


def _dot_tn(a, b):
    return lax.dot_general(a, b, (((0,), (0,)), ((), ())), preferred_element_type=F32)


def _dot_split3(a01, x):
    hi = x.astype(BF16)
    r = x - hi.astype(F32)
    mid = r.astype(BF16)
    lo = (r - mid.astype(F32)).astype(BF16)
    return _dot(a01, hi) + _dot(a01, mid) + _dot(a01, lo)


def _bucket_thresholds():
    max_exact = REL_BUCKETS // 2
    ratio = REL_MAX_DIST / max_exact
    return [math.ceil(max_exact * ratio ** (k / (REL_BUCKETS - max_exact))) for k in range(1, REL_BUCKETS - max_exact)]


def _bias_kernel(tab_ref, o_ref, *, off, shape):
    m = pl.program_id(0)
    r = lax.broadcasted_iota(jnp.int32, shape, 0)
    c = lax.broadcasted_iota(jnp.int32, shape, 1)
    n = off + r - c
    nn = jnp.maximum(n, 0)
    max_exact = REL_BUCKETS // 2
    large = jnp.full(shape, max_exact, jnp.int32)
    for t in _bucket_thresholds():
        large = large + (nn >= t).astype(jnp.int32)
    bucket = jnp.where(nn < max_exact, nn, large)
    val = jnp.zeros(shape, F32)
    for b in range(REL_BUCKETS):
        val = jnp.where(bucket == b, tab_ref[b, m], val)
    val = val - tab_ref[REL_BUCKETS - 1, m]
    o_ref[0] = jnp.where(n >= 0, val, NEG)


def _bias_tiles(table, off, shape):
    return pl.pallas_call(
        functools.partial(_bias_kernel, off=off, shape=shape),
        out_shape=jax.ShapeDtypeStruct((N_MAPS,) + shape, F32),
        grid=(N_MAPS,),
        in_specs=[pl.BlockSpec(memory_space=pltpu.SMEM)],
        out_specs=pl.BlockSpec((1,) + shape, lambda m: (m, 0, 0)),
        compiler_params=_params(1),
        name="rel_bias_tiles",
    )(table)


def _proj_kernel(x_ref, g_ref, w_ref, wdt_ref, dtb_ref, gq_ref, gk_ref, gcq_ref, s64_ref,
                 ub_ref, kv32_ref, uf_ref, dt_ref, h_scr):
    j = pl.program_id(1)

    @pl.when(j == 0)
    def _():
        h = _rms(x_ref[...], g_ref[...]).astype(BF16)
        h_scr[...] = h
        dt_ref[...] = _softplus(_dot(h, wdt_ref[...]) + dtb_ref[...])

    def proj():
        return _dot(h_scr[...], w_ref[...])

    def seg64_norm(u, gain):
        sq = (u * u).astype(BF16)
        ms = jnp.concatenate([_dot(sq[:, c * SEG_W:(c + 1) * SEG_W], s64_ref[...]) for c in range(TN // SEG_W)],
                             axis=1)
        return u * lax.rsqrt(ms + EPS) * gain

    @pl.when(j == PB_Q)
    def _():
        ub_ref[...] = (seg64_norm(proj(), gq_ref[...]) * (DK ** -0.5)).astype(BF16)

    @pl.when(j == PB_CQ)
    def _():
        u = proj()
        for s in range(TN // CA_D):
            seg = u[:, s * CA_D:(s + 1) * CA_D]
            ub_ref[:, s * CA_D:(s + 1) * CA_D] = (_rms(seg, gcq_ref[...]) * (CA_D ** -0.5)).astype(BF16)

    @pl.when(j == PB_K)
    def _():
        kn = seg64_norm(proj(), gk_ref[...])
        kv32_ref[...] = kn
        ub_ref[...] = kn.astype(BF16)

    @pl.when(j == PB_V)
    def _():
        u = proj()
        kv32_ref[...] = u
        ub_ref[...] = u.astype(BF16)

    @pl.when(j >= PB_XBC)
    def _():
        uf_ref[...] = proj()


def _proj(x, g, w, wdt, dtb, gq, gk, gcq, s64):
    n = x.shape[0]
    tm = min(1024, n)
    nb_b = PB_XBC
    nb_f = PB_END - PB_XBC
    row = lambda i, j: (i, 0)
    return pl.pallas_call(
        _proj_kernel,
        out_shape=(jax.ShapeDtypeStruct((n, nb_b * TN), BF16),
                   jax.ShapeDtypeStruct((n, 2 * D), F32),
                   jax.ShapeDtypeStruct((n, nb_f * TN), F32),
                   jax.ShapeDtypeStruct((n, 128), F32)),
        grid=(n // tm, PB_END),
        in_specs=[pl.BlockSpec((tm, D), row),
                  pl.BlockSpec((1, D), lambda i, j: (0, 0)),
                  pl.BlockSpec((D, TN), lambda i, j: (0, j)),
                  pl.BlockSpec((D, 128), lambda i, j: (0, 0)),
                  pl.BlockSpec((1, 128), lambda i, j: (0, 0)),
                  pl.BlockSpec((1, TN), lambda i, j: (0, 0)),
                  pl.BlockSpec((1, TN), lambda i, j: (0, 0)),
                  pl.BlockSpec((1, CA_D), lambda i, j: (0, 0)),
                  pl.BlockSpec((SEG_W, SEG_W), lambda i, j: (0, 0))],
        out_specs=(pl.BlockSpec((tm, TN), lambda i, j: (i, jnp.minimum(j, nb_b - 1))),
                   pl.BlockSpec((tm, TN), lambda i, j: (i, jnp.clip(j - PB_K, 0, 1))),
                   pl.BlockSpec((tm, TN), lambda i, j: (i, jnp.clip(j - PB_XBC, 0, nb_f - 1))),
                   pl.BlockSpec((tm, 128), row)),
        scratch_shapes=[pltpu.VMEM((tm, D), BF16)],
        compiler_params=_params(2, 56 << 20),
        name="in_proj",
    )(x, g, w, wdt, dtb, gq, gk, gcq, s64)


def _memkv_kernel(m_ref, g_ref, w_ref, gk_ref, mk_ref, mv_ref, h_scr):
    j = pl.program_id(1)

    @pl.when(j == 0)
    def _():
        h_scr[...] = _rms(m_ref[...], g_ref[...]).astype(BF16)

    u = _dot(h_scr[...], w_ref[...])

    @pl.when(j == 0)
    def _():
        for s in range(CA_HEADS):
            mk_ref[:, s * CA_D:(s + 1) * CA_D] = _rms(u[:, s * CA_D:(s + 1) * CA_D], gk_ref[...])

    @pl.when(j == 1)
    def _():
        mv_ref[...] = u


def _memkv(mem, g, w, gk):
    n = mem.shape[0]
    tm = min(512, n)
    return pl.pallas_call(
        _memkv_kernel,
        out_shape=(jax.ShapeDtypeStruct((n, D), F32), jax.ShapeDtypeStruct((n, D), F32)),
        grid=(n // tm, 2),
        in_specs=[pl.BlockSpec((tm, D), lambda i, j: (i, 0)),
                  pl.BlockSpec((1, D), lambda i, j: (0, 0)),
                  pl.BlockSpec((D, D), lambda i, j: (0, j)),
                  pl.BlockSpec((1, CA_D), lambda i, j: (0, 0))],
        out_specs=(pl.BlockSpec((tm, D), lambda i, j: (i, 0)), pl.BlockSpec((tm, D), lambda i, j: (i, 0))),
        scratch_shapes=[pltpu.VMEM((tm, D), BF16)],
        compiler_params=_params(2),
        name="mem_kv",
    )(mem, g, w, gk)


def _lambda(lam_ref, lam0):
    lv = lam_ref[...]
    s1 = jnp.sum(lv[0:1] * lv[1:2], axis=-1, keepdims=True)
    s2 = jnp.sum(lv[2:3] * lv[3:4], axis=-1, keepdims=True)
    return jnp.exp(s1) - jnp.exp(s2) + lam0


def _softmax_step(state, s, v_bf16):
    m, l, acc = state
    m_new = jnp.maximum(m, jnp.max(s, axis=-1, keepdims=True))
    alpha = jnp.exp(m - m_new)
    p = jnp.exp(s - m_new)
    l = alpha * l + jnp.sum(p, axis=-1, keepdims=True)
    acc = alpha * acc + _dot(p.astype(BF16), v_bf16)
    return m_new, l, acc


def _attn_prompt_kernel(lam_ref, q_ref, k_ref, v_ref, bias_ref, gout_ref, o_ref, *, tq, lam0):
    qi = pl.program_id(2)
    q = q_ref[...]
    lane = lax.broadcasted_iota(jnp.int32, q.shape, 1)
    qm = (jnp.where(lane < DK, q, jnp.zeros_like(q)), jnp.where(lane >= DK, q, jnp.zeros_like(q)))

    def block(kb, states, bias_idx):
        start = pl.multiple_of(kb * tq, tq)
        k = k_ref[pl.ds(start, tq), :]
        v = v_ref[pl.ds(start, tq), :]
        out = []
        for mi in range(2):
            s = _dot_nt(qm[mi], k)
            if bias_idx is not None:
                s = s + bias_ref[mi, bias_idx]
            out.append(_softmax_step(states[mi], s, v))
        return tuple(out)

    init = tuple((jnp.full((tq, 1), NEG, F32), jnp.zeros((tq, 1), F32), jnp.zeros((tq, DV), F32))
                 for _ in range(2))
    states = lax.fori_loop(0, qi - 1, lambda kb, st: block(kb, st, None), init)
    states = lax.fori_loop(jnp.maximum(qi - 1, 0), qi, lambda kb, st: block(kb, st, 1), states)
    states = block(qi, states, 0)

    lam = _lambda(lam_ref, lam0)
    (_, l0, a0), (_, l1, a1) = states
    o = a0 / l0 - lam * (a1 / l1)
    o_ref[...] = (_rms(o, gout_ref[...]) * (1.0 - lam0)).astype(o_ref.dtype)


def _attn_prompt(ub3, lam_vecs, bias, gout, *, lam0, tq):
    b, t, _ = ub3.shape
    kcol, vcol = (PB_K * TN) // DV, (PB_V * TN) // DV
    return pl.pallas_call(
        functools.partial(_attn_prompt_kernel, tq=tq, lam0=lam0),
        out_shape=jax.ShapeDtypeStruct((b, t, D), BF16),
        grid=(b, N_HEADS, t // tq),
        in_specs=[pl.BlockSpec((4, DK), lambda b, h, i: (0, 0)),
                  pl.BlockSpec((None, tq, DV), lambda b, h, i: (b, i, h)),
                  pl.BlockSpec((None, t, DV), lambda b, h, i: (b, 0, kcol + h)),
                  pl.BlockSpec((None, t, DV), lambda b, h, i: (b, 0, vcol + h)),
                  pl.BlockSpec((2, 2, tq, tq), lambda b, h, i: (h, 0, 0, 0)),
                  pl.BlockSpec((1, DV), lambda b, h, i: (0, 0))],
        out_specs=pl.BlockSpec((None, tq, DV), lambda b, h, i: (b, i, h)),
        compiler_params=_params(3),
        name="diff_attn_prompt",
    )(lam_vecs, ub3, ub3, ub3, bias, gout)


def _head_queries(q, h):
    qh = q[:, h * DV:(h + 1) * DV]
    lane = lax.broadcasted_iota(jnp.int32, qh.shape, 1)
    return jnp.concatenate([jnp.where(lane < DK, qh, 0.0), jnp.where(lane >= DK, qh, 0.0)], axis=0).astype(BF16)


def _attn_sample_kernel(pt_ref, q_ref, kn_ref, vn_ref, bpast_ref, bnew_ref, lam_ref, gout_ref, *rest, npg, lam0):
    k_refs, v_refs = rest[:npg], rest[npg:2 * npg]
    o_ref, m_scr, l_scr, acc_scr = rest[2 * npg:]
    c = pl.program_id(1)
    last = c == pl.num_programs(1) - 1
    tdec = q_ref.shape[0]
    rows = 2 * tdec

    @pl.when(c == 0)
    def _():
        m_scr[...] = jnp.full(m_scr.shape, NEG, F32)
        l_scr[...] = jnp.zeros(l_scr.shape, F32)
        acc_scr[...] = jnp.zeros(acc_scr.shape, F32)

    q = q_ref[...]
    qms = [_head_queries(q, h) for h in range(N_HEADS)]

    def update(s, pv_fn):
        m_old = m_scr[...]
        m_new = jnp.maximum(m_old, jnp.max(s, axis=-1, keepdims=True))
        alpha = jnp.exp(m_old - m_new)
        p = jnp.exp(s - m_new[:, 0:1])
        l_scr[...] = alpha * l_scr[...] + jnp.sum(p, axis=-1, keepdims=True)
        acc_scr[...] = alpha * acc_scr[...] + pv_fn(p.astype(BF16))
        m_scr[...] = m_new

    near = last.astype(F32)
    s_pages = []
    for i in range(npg):
        s_i = jnp.concatenate([_dot(qms[h], k_refs[i][h * DV:(h + 1) * DV, :].astype(BF16))
                               for h in range(N_HEADS)], axis=0)
        if i == npg - 1:
            s_i = s_i + near * bpast_ref[...]
        s_pages.append(s_i)

    def pv_pages(p):
        heads = []
        for h in range(N_HEADS):
            out = None
            for i in range(npg):
                v_h = v_refs[i][pl.ds(h, PAGE, stride=N_HEADS), :].astype(BF16)
                part = _dot(p[h * rows:(h + 1) * rows, i * PAGE:(i + 1) * PAGE], v_h)
                out = part if out is None else out + part
            heads.append(out)
        return jnp.concatenate(heads, axis=0)

    update(jnp.concatenate(s_pages, axis=1), pv_pages)

    @pl.when(last)
    def _():
        pad = jnp.zeros((PAGE - tdec, D), F32)
        kn = jnp.concatenate([kn_ref[...], pad], axis=0).astype(BF16)
        vn = jnp.concatenate([vn_ref[...], pad], axis=0).astype(BF16)
        s_new = jnp.concatenate([_dot_nt(qms[h], kn[:, h * DV:(h + 1) * DV]) for h in range(N_HEADS)], axis=0)
        update(s_new + bnew_ref[...],
               lambda p: jnp.concatenate([_dot(p[h * rows:(h + 1) * rows], vn[:, h * DV:(h + 1) * DV])
                                          for h in range(N_HEADS)], axis=0))
        lam = _lambda(lam_ref, lam0)
        o_all = acc_scr[...] / l_scr[...]
        for h in range(N_HEADS):
            o = o_all[h * rows:h * rows + tdec] - lam * o_all[h * rows + tdec:(h + 1) * rows]
            o_ref[:, h * DV:(h + 1) * DV] = _rms(o, gout_ref[...]) * (1.0 - lam0)


def _attn_sample(page_table, q3, kv3, cache_k4, cache_v4, layer, bpast, bnew, lam_vecs, gout, *, lam0):
    bs, tdec, _ = q3.shape
    n_pages = page_table.shape[1]
    npg = math.gcd(n_pages, 8)
    pt = page_table.reshape(-1)

    def page_spec(i):
        return pl.BlockSpec((None, None, D, PAGE),
                            lambda b, c, pt_ref: (layer, pt_ref[b * n_pages + c * npg + i], 0, 0))

    per_b = pl.BlockSpec((None, tdec, D), lambda b, c, pt_ref: (b, 0, 0))
    whole = lambda shape: pl.BlockSpec(shape, lambda b, c, pt_ref: (0,) * len(shape))
    stat = pltpu.VMEM((N_HEADS * 2 * tdec, DV), F32)
    return pl.pallas_call(
        functools.partial(_attn_sample_kernel, npg=npg, lam0=lam0),
        out_shape=jax.ShapeDtypeStruct((bs, tdec, D), F32),
        grid_spec=pltpu.PrefetchScalarGridSpec(
            num_scalar_prefetch=1,
            grid=(bs, n_pages // npg),
            in_specs=[per_b, per_b, pl.BlockSpec((None, tdec, D), lambda b, c, pt_ref: (b, 0, 1)),
                      whole((N_HEADS * 2 * tdec, PAGE)), whole((N_HEADS * 2 * tdec, PAGE)),
                      whole((4, DK)), whole((1, DV))]
                     + [page_spec(i) for i in range(npg)] + [page_spec(i) for i in range(npg)],
            out_specs=per_b,
            scratch_shapes=[stat, stat, stat]),
        compiler_params=_params(2),
        name="diff_attn_sample",
    )(pt, q3, kv3, kv3, bpast, bnew, lam_vecs, gout, *([cache_k4] * npg), *([cache_v4] * npg))


def _cross_kernel(cq_ref, mk_ref, mv_ref, o_ref):
    rows = cq_ref.shape[0]
    for h in range(CA_HEADS):
        cols = slice(h * CA_D, (h + 1) * CA_D)
        q = cq_ref[:, cols]
        if rows % 16:
            q = jnp.concatenate([q.astype(F32), jnp.zeros((16 - rows % 16, CA_D), F32)], axis=0)
        s = _dot_nt(q.astype(BF16), mk_ref[:, cols].astype(BF16))
        p = jnp.exp(s - jnp.max(s, axis=-1, keepdims=True))
        o = _dot(p.astype(BF16), mv_ref[:, cols].astype(BF16)) / jnp.sum(p, axis=-1, keepdims=True)
        o_ref[:, cols] = o[0:rows].astype(o_ref.dtype)


def _cross(cq3, col_block, mk3, mv3, out_dtype):
    b, t, _ = cq3.shape
    tm = min(512, t)
    return pl.pallas_call(
        _cross_kernel,
        out_shape=jax.ShapeDtypeStruct((b, t, D), out_dtype),
        grid=(b, t // tm),
        in_specs=[pl.BlockSpec((None, tm, D), lambda b, i: (b, i, col_block)),
                  pl.BlockSpec((None, MEM_LEN, D), lambda b, i: (b, 0, 0)),
                  pl.BlockSpec((None, MEM_LEN, D), lambda b, i: (b, 0, 0))],
        out_specs=pl.BlockSpec((None, tm, D), lambda b, i: (b, i, 0)),
        compiler_params=_params(2),
        name="mem_cross_attn",
    )(cq3, mk3, mv3)


def _ssd_kernel(xbc_ref, z_ref, dt_ref, h0_ref, c0_ref, cw_ref, cb_ref, alog_ref, dskip_ref, gout_ref,
                y_ref, hn_ref, cn_ref, buf, xc_scr, dt_scr, yb, h_scr, *, tv):
    L = SSM_L
    c = pl.program_id(1)
    head = 8

    @pl.when(c == 0)
    def _():
        buf[0:head, :] = jnp.zeros((head, CONV_CH), F32)
        buf[head - (CONV_K - 1):head, :] = c0_ref[...]
        h_scr[...] = h0_ref[...]
        if tv < L:
            buf[head + tv:, :] = jnp.zeros((L - tv, CONV_CH), F32)
            dt_scr[...] = jnp.zeros(dt_scr.shape, F32)

    buf[head:head + tv, :] = xbc_ref[...]
    cn_ref[...] = buf[head + tv - (CONV_K - 1):head + tv, :]

    cw = cw_ref[...]
    for j in range(CONV_CH // 512):
        cols = slice(j * 512, (j + 1) * 512)
        acc = cb_ref[:, cols] + cw[CONV_K - 1:CONV_K, cols] * buf[head:head + L, cols]
        for k in range(CONV_K - 1):
            acc = acc + cw[k:k + 1, cols] * buf[head - (CONV_K - 1) + k:head - (CONV_K - 1) + k + L, cols]
        xc_scr[:, cols] = acc * _sigmoid(acc)

    if tv == L:
        buf[0:head, :] = buf[L:L + head, :]
        dt = dt_ref[...]
    else:
        dt_scr[0:tv, :] = dt_ref[...]
        dt = dt_scr[...]

    a_neg = -jnp.exp(alog_ref[...])
    d_a = dt * a_neg
    row = lax.broadcasted_iota(jnp.int32, (L, L), 0)
    col = lax.broadcasted_iota(jnp.int32, (L, L), 1)
    causal = col <= row
    a_cum = _dot_split3(causal.astype(BF16), d_a)
    a_cum_t = a_cum.T
    dt_t = dt.T
    a_last = a_cum[L - 1:L, :]
    e_cum = jnp.exp(a_cum)
    to_end = jnp.exp(a_last - a_cum) * dt
    e_last = jnp.exp(a_last)
    lane = lax.broadcasted_iota(jnp.int32, (L, 128), 1)
    first = lane < SSM_P
    row_first = lax.broadcasted_iota(jnp.int32, (128, 128), 0) < SSM_P

    for g in range(SSM_G):
        b_g = xc_scr[:, SSM_INNER + g * SSM_N:SSM_INNER + (g + 1) * SSM_N].astype(BF16)
        c_g = xc_scr[:, SSM_INNER + SSM_G * SSM_N + g * SSM_N:SSM_INNER + SSM_G * SSM_N + (g + 1) * SSM_N].astype(BF16)
        cb = _dot_nt(c_g, b_g)
        for jj in (2 * g, 2 * g + 1):
            cols = slice(jj * 128, (jj + 1) * 128)
            x_pair = xc_scr[:, cols]
            x_bf = x_pair.astype(BF16)
            yw = []
            for hh in (2 * jj, 2 * jj + 1):
                seg = a_cum[:, hh:hh + 1] - a_cum_t[hh:hh + 1, :]
                w = cb * jnp.exp(jnp.where(causal, seg, NEG)) * dt_t[hh:hh + 1, :]
                yw.append(_dot(w.astype(BF16), x_bf))
            ha, hb = 2 * jj, 2 * jj + 1
            hp = h_scr[jj]
            y = jnp.where(first, yw[0], yw[1])
            y = y + _dot_nt(c_g, hp.astype(BF16)) * jnp.where(first, e_cum[:, ha:ha + 1], e_cum[:, hb:hb + 1])
            y = y + dskip_ref[:, cols] * x_pair
            te = jnp.where(first, to_end[:, ha:ha + 1], to_end[:, hb:hb + 1])
            upd = _dot_tn((x_pair * te).astype(BF16), b_g)
            h_scr[jj] = hp * jnp.where(row_first, e_last[:, ha:ha + 1], e_last[:, hb:hb + 1]) + upd
            yb[:, cols] = y[0:tv] * (z_ref[:, cols] * _sigmoid(z_ref[:, cols]))

    gw = SSM_INNER // SSM_G
    for g in range(SSM_G):
        cols = slice(g * gw, (g + 1) * gw)
        y_ref[:, cols] = _rms(yb[:, cols], gout_ref[:, cols]).astype(y_ref.dtype)
    hn_ref[...] = h_scr[...]


def _ssd(xbc_src, xbc_blk, z_src, z_blk, dt3, h0, c0, cw, cb, alog, dskip, gout, out_dtype):
    b, t, _ = dt3.shape
    tv = min(SSM_L, t)
    nc = t // tv
    seq = lambda width, blk: pl.BlockSpec((None, tv, width), lambda b, c: (b, c, blk))
    per_b = lambda shape: pl.BlockSpec((None,) + shape, lambda b, c: (b,) + (0,) * len(shape))
    whole = lambda shape: pl.BlockSpec(shape, lambda b, c: (0,) * len(shape))
    return pl.pallas_call(
        functools.partial(_ssd_kernel, tv=tv),
        out_shape=(jax.ShapeDtypeStruct((b, t, SSM_INNER), out_dtype),
                   jax.ShapeDtypeStruct((b, N_PAIRS, 128, SSM_N), F32),
                   jax.ShapeDtypeStruct((b, CONV_K - 1, CONV_CH), F32)),
        grid=(b, nc),
        in_specs=[seq(CONV_CH, xbc_blk), seq(SSM_INNER, z_blk), seq(128, 0),
                  per_b((N_PAIRS, 128, SSM_N)), per_b((CONV_K - 1, CONV_CH)),
                  whole((CONV_K, CONV_CH)), whole((1, CONV_CH)), whole((1, 128)),
                  whole((1, SSM_INNER)), whole((1, SSM_INNER))],
        out_specs=(seq(SSM_INNER, 0), per_b((N_PAIRS, 128, SSM_N)), per_b((CONV_K - 1, CONV_CH))),
        scratch_shapes=[pltpu.VMEM((SSM_L + 8, CONV_CH), F32), pltpu.VMEM((SSM_L, CONV_CH), F32),
                        pltpu.VMEM((SSM_L, 128), F32), pltpu.VMEM((tv, SSM_INNER), F32),
                        pltpu.VMEM((N_PAIRS, 128, SSM_N), F32)],
        compiler_params=_params(2),
        name="conv_ssd",
    )(xbc_src, z_src, dt3, h0, c0, cw, cb, alog, dskip, gout)


def _merge_kernel(x_ref, gmix_ref, wg_ref, oa_ref, ys_ref, oc_ref, wa_ref, ws_ref, wc_ref, wo_ref, o_ref):
    x = x_ref[...]
    h = _rms(x, gmix_ref[...]).astype(BF16)
    branches = ((oa_ref, wa_ref), (ys_ref, ws_ref), (oc_ref, wc_ref))
    merged = None
    for b, (act_ref, w_ref) in enumerate(branches):
        gate = _sigmoid(_dot(h, wg_ref[:, b * D:(b + 1) * D]))
        term = gate * _dot(act_ref[...].astype(BF16), w_ref[...])
        merged = term if merged is None else merged + term
    o_ref[...] = x + _dot(merged.astype(BF16), wo_ref[...])


def _merge(x, gmix, wg, oa, ys, oc, wa, ws, wc, wo):
    n = x.shape[0]
    tm = min(512, n)
    row = lambda width: pl.BlockSpec((tm, width), lambda i: (i, 0))
    whole = lambda shape: pl.BlockSpec(shape, lambda i: (0, 0), pipeline_mode=pl.Buffered(1))
    return pl.pallas_call(
        _merge_kernel,
        out_shape=jax.ShapeDtypeStruct((n, D), F32),
        grid=(n // tm,),
        in_specs=[row(D), whole((1, D)), whole((D, 3 * D)), row(D), row(SSM_INNER), row(D),
                  whole((D, D)), whole((SSM_INNER, D)), whole((D, D)), whole((D, D))],
        out_specs=row(D),
        compiler_params=_params(1),
        name="branch_merge",
    )(x, gmix, wg, oa, ys, oc, wa, ws, wc, wo)


def _mlp_kernel(x_ref, g_ref, wu_ref, wd_ref, o_ref, h_scr, acc):
    f = pl.program_id(1)

    @pl.when(f == 0)
    def _():
        x = x_ref[...]
        h_scr[...] = _rms(x, g_ref[...]).astype(BF16)
        acc[...] = x

    a = jnp.maximum(_dot(h_scr[...], wu_ref[...]), 0.0)
    acc[...] += _dot((a * a).astype(BF16), wd_ref[...])

    @pl.when(f == pl.num_programs(1) - 1)
    def _():
        o_ref[...] = acc[...]


def _mlp(x, g, wu, wd):
    n = x.shape[0]
    tm = min(1024, n)
    tf = 1024
    return pl.pallas_call(
        _mlp_kernel,
        out_shape=jax.ShapeDtypeStruct((n, D), F32),
        grid=(n // tm, D_FF // tf),
        in_specs=[pl.BlockSpec((tm, D), lambda i, f: (i, 0)),
                  pl.BlockSpec((1, D), lambda i, f: (0, 0)),
                  pl.BlockSpec((D, tf), lambda i, f: (0, f)),
                  pl.BlockSpec((tf, D), lambda i, f: (f, 0))],
        out_specs=pl.BlockSpec((tm, D), lambda i, f: (i, 0)),
        scratch_shapes=[pltpu.VMEM((tm, D), BF16), pltpu.VMEM((tm, D), F32)],
        compiler_params=_params(2),
        name="relu2_mlp",
    )(x, g, wu, wd)


def _lambda_init(layer):
    return 0.8 - 0.6 * math.exp(-0.3 * layer)


def _layer_weights(l, p):
    w_in = p['w_in'][l]
    cols = lambda off, width: w_in[:, off:off + width]
    w_main = jnp.concatenate([cols(OFF_Q, D), cols(OFF_CQ, D), cols(OFF_K, D), cols(OFF_V, D),
                              cols(OFF_XBC, CONV_CH), cols(OFF_Z, SSM_INNER)], axis=1).astype(BF16)
    w_gate = cols(OFF_G, 3 * D).astype(BF16)
    w_dt = jnp.pad(cols(OFF_DT, SSM_HEADS), ((0, 0), (0, 128 - SSM_HEADS))).astype(BF16)
    row = lambda v: v.reshape(1, -1).astype(F32)
    return dict(
        g_mix=row(p['g_mix'][l]), w_main=w_main, w_dt=w_dt, w_gate=w_gate,
        dt_bias=jnp.pad(row(p['dt_bias'][l]), ((0, 0), (0, 128 - SSM_HEADS))),
        gq=row(jnp.tile(p['g_da_q'][l], TN // DK)), gk=row(jnp.tile(p['g_da_k'][l], TN // DK)),
        gcq=row(p['g_ca_q'][l]),
        lam=jnp.stack([p['lambda_q1'][l], p['lambda_k1'][l], p['lambda_q2'][l], p['lambda_k2'][l]]).astype(F32),
        g_da_out=row(p['g_da_out'][l]),
        conv_w=p['conv_w'][l].astype(F32), conv_b=row(p['conv_b'][l]),
        a_log=jnp.pad(row(p['a_log'][l]), ((0, 0), (0, 128 - SSM_HEADS))),
        d_skip=row(jnp.repeat(p['d_skip'][l], SSM_P)), g_ssm_out=row(p['g_ssm_out'][l]),
        w_br_attn=p['w_br_attn'][l].astype(BF16), w_br_ssm=p['w_br_ssm'][l].astype(BF16),
        w_br_cross=p['w_br_cross'][l].astype(BF16), w_out=p['w_out'][l].astype(BF16),
        g_mlp=row(p['g_mlp'][l]), w_up=p['w_up'][l].astype(BF16), w_down=p['w_down'][l].astype(BF16),
        g_mem=row(p['g_mem'][l]), w_mem_kv=p['w_mem_kv'][l].astype(BF16), g_ca_k=row(p['g_ca_k'][l]),
    )


def _seg64_matrix():
    i = jnp.arange(SEG_W) // DK
    return jnp.where(i[:, None] == i[None, :], 1.0 / DK, 0.0).astype(BF16)


def _layer(x, bsz, w, s64, attn_fn, mem_k3, mem_v3, h0, c0, act_dtype):
    n = x.shape[0]
    t = n // bsz
    ub, kv32, uf, dt = _proj(x, w['g_mix'], w['w_main'], w['w_dt'], w['dt_bias'], w['gq'], w['gk'], w['gcq'], s64)
    ub3 = ub.reshape(bsz, t, -1)
    o_a = attn_fn(ub3, kv32)
    uf3 = uf.reshape(bsz, t, -1)
    y_n, h_new, c_new = _ssd(uf3, 0, uf3, ((PB_Z - PB_XBC) * TN) // SSM_INNER, dt.reshape(bsz, t, 128), h0, c0,
                             w['conv_w'], w['conv_b'], w['a_log'], w['d_skip'], w['g_ssm_out'], act_dtype)
    if act_dtype == BF16:
        cq3, cq_blk = ub3, (PB_CQ * TN) // D
    else:
        cq3, cq_blk = ub3[:, :, PB_CQ * TN:PB_K * TN].astype(F32), 0
    o_c = _cross(cq3, cq_blk, mem_k3, mem_v3, act_dtype)
    x = _merge(x, w['g_mix'], w['w_gate'], o_a.reshape(n, D), y_n.reshape(n, SSM_INNER), o_c.reshape(n, D),
               w['w_br_attn'], w['w_br_ssm'], w['w_br_cross'], w['w_out'])
    x = _mlp(x, w['g_mlp'], w['w_up'], w['w_down'])
    return x, kv32[:, 0:D], kv32[:, D:2 * D], h_new, c_new


def kernel(x_prompt, x_sample, cache_k, cache_v, cache_mem_k, cache_mem_v, state_ssm, state_conv, page_table, mem_prompt, rel_bias_table, g_mix, w_in, g_da_q, g_da_k, lambda_q1, lambda_k1, lambda_q2, lambda_k2, g_da_out, w_br_attn, conv_w, conv_b, dt_bias, a_log, d_skip, g_ssm_out, w_br_ssm, g_mem, w_mem_kv, g_ca_q, g_ca_k, w_br_cross, w_out, g_mlp, w_up, w_down):
    p = dict(g_mix=g_mix, w_in=w_in, g_da_q=g_da_q, g_da_k=g_da_k, lambda_q1=lambda_q1, lambda_k1=lambda_k1,
             lambda_q2=lambda_q2, lambda_k2=lambda_k2, g_da_out=g_da_out, w_br_attn=w_br_attn, conv_w=conv_w,
             conv_b=conv_b, dt_bias=dt_bias, a_log=a_log, d_skip=d_skip, g_ssm_out=g_ssm_out, w_br_ssm=w_br_ssm,
             g_mem=g_mem, w_mem_kv=w_mem_kv, g_ca_q=g_ca_q, g_ca_k=g_ca_k, w_br_cross=w_br_cross, w_out=w_out,
             g_mlp=g_mlp, w_up=w_up, w_down=w_down)
    depth = w_in.shape[0]
    bp, tp, _ = x_prompt.shape
    bs, ts, _ = x_sample.shape
    n_phys = cache_k.shape[1]
    past_len = page_table.shape[1] * PAGE
    tq = min(512, tp)
    table = rel_bias_table.astype(F32)
    s64 = _seg64_matrix()

    bias_p = jnp.stack([_bias_tiles(table, 0, (tq, tq)), _bias_tiles(table, tq, (tq, tq))], axis=1)
    as_heads = lambda a: a.reshape(N_HEADS * 2 * ts, PAGE)
    bias_past = as_heads(_bias_tiles(table, PAGE, (ts, PAGE)))
    bias_new = as_heads(_bias_tiles(table, 0, (ts, PAGE)))

    cache_k4 = jnp.transpose(cache_k, (0, 1, 3, 4, 2)).reshape(depth, n_phys, D, PAGE)
    cache_v4 = cache_v.reshape(depth, n_phys, PAGE * N_HEADS, DV)
    mem_flat = mem_prompt.reshape(bp * MEM_LEN, D)
    zero_h = jnp.zeros((bp, N_PAIRS, 128, SSM_N), F32)
    zero_c = jnp.zeros((bp, CONV_K - 1, CONV_CH), F32)

    xp = x_prompt.reshape(bp * tp, D)
    xs = x_sample.reshape(bs * ts, D)
    outs = {k: [] for k in ('kp', 'vp', 'mk', 'mv', 'hp', 'cp', 'ks', 'vs', 'hs', 'cs')}
    for l in range(depth):
        w = _layer_weights(l, p)
        lam0 = _lambda_init(l)

        mk, mv = _memkv(mem_flat, w['g_mem'], w['w_mem_kv'], w['g_ca_k'])
        attn_p = lambda ub3, kv32: _attn_prompt(ub3, w['lam'], bias_p, w['g_da_out'], lam0=lam0, tq=tq)
        xp, k, v, hn, cn = _layer(xp, bp, w, s64, attn_p, mk.reshape(bp, MEM_LEN, D), mv.reshape(bp, MEM_LEN, D),
                                  zero_h, zero_c, BF16)
        for key, val in zip(('kp', 'vp', 'mk', 'mv', 'hp', 'cp'), (k, v, mk, mv, hn, cn)):
            outs[key].append(val)

        def attn_s(ub3, kv32):
            q3 = ub3[:, :, 0:D].astype(F32)
            return _attn_sample(page_table, q3, kv32.reshape(bs, ts, 2 * D), cache_k4, cache_v4,
                                l, bias_past, bias_new, w['lam'], w['g_da_out'], lam0=lam0)

        xs, k, v, hn, cn = _layer(xs, bs, w, s64, attn_s, cache_mem_k[l].reshape(bs, MEM_LEN, D),
                                  cache_mem_v[l].reshape(bs, MEM_LEN, D),
                                  state_ssm[l].reshape(bs, N_PAIRS, 128, SSM_N), state_conv[l], F32)
        for key, val in zip(('ks', 'vs', 'hs', 'cs'), (k, v, hn, cn)):
            outs[key].append(val)

    st = lambda key, shape: jnp.stack(outs[key]).reshape((depth,) + shape)
    return (xp.reshape(bp, tp, D), xs.reshape(bs, ts, D),
            st('kp', (bp, tp, N_MAPS, DK)), st('vp', (bp, tp, N_HEADS, DV)),
            st('mk', (bp, MEM_LEN, CA_HEADS, CA_D)), st('mv', (bp, MEM_LEN, CA_HEADS, CA_D)),
            st('hp', (bp, SSM_HEADS, SSM_P, SSM_N)), st('cp', (bp, CONV_K - 1, CONV_CH)),
            st('ks', (bs, ts, N_MAPS, DK)), st('vs', (bs, ts, N_HEADS, DV)),
            st('hs', (bs, SSM_HEADS, SSM_P, SSM_N)), st('cs', (bs, CONV_K - 1, CONV_CH)))
```

```python
import functools
import math

import jax
import jax.numpy as jnp
from jax import lax
from jax.experimental import pallas as pl
from jax.experimental.pallas import tpu as pltpu

F32, BF16 = jnp.float32, jnp.bfloat16
EPS = 1e-5
NEG = -1e30
LOG2E = math.log2(math.e)

D = 1024
N_MAPS, DK = 16, 64
N_HEADS, DV = 8, 128
SSM_INNER, SSM_P, SSM_N, SSM_G = 2048, 64, 128, 8
SSM_HEADS = SSM_INNER // SSM_P
N_PAIRS = SSM_HEADS // 2
CONV_K, CONV_CH = 4, 4096
MEM_LEN, CA_HEADS, CA_D = 256, 4, 256
D_FF = 4096
PAGE = 128
REL_BUCKETS, REL_MAX_DIST = 32, 128
SSM_L = 128

OFF_Q, OFF_K, OFF_V, OFF_Z, OFF_XBC = 0, 1024, 2048, 3072, 5120
OFF_DT = OFF_XBC + CONV_CH
OFF_CQ = OFF_DT + SSM_HEADS
OFF_G = OFF_CQ + D

TN = 1024
SEG_W = 256
PB_Q, PB_CQ, PB_K, PB_V, PB_XBC, PB_Z, PB_END = 0, 1, 2, 3, 4, 8, 10
VMEM_LIMIT = 48 << 20


def _params(n_axes, vmem=VMEM_LIMIT):
    return pltpu.CompilerParams(dimension_semantics=("arbitrary",) * n_axes, vmem_limit_bytes=vmem)


def _rms(x, gain):
    ms = jnp.mean(x * x, axis=-1, keepdims=True)
    return x * lax.rsqrt(ms + EPS) * gain


def _sigmoid(x):
    return 1.0 / (1.0 + jnp.exp(-x))


def _softplus(x):
    return jnp.maximum(x, 0.0) + jnp.log1p(jnp.exp(-jnp.abs(x)))


def _dot(a, b):
    return jnp.dot(a, b, preferred_element_type=F32)


def _dot_nt(a, b):
    return lax.dot_general(a, b, (((1,), (1,)), ((), ())), preferred_element_type=F32)


def _dot_tn(a, b):
    return lax.dot_general(a, b, (((0,), (0,)), ((), ())), preferred_element_type=F32)


def _dot_split3(a01, x):
    hi = x.astype(BF16)
    r = x - hi.astype(F32)
    mid = r.astype(BF16)
    lo = (r - mid.astype(F32)).astype(BF16)
    return _dot(a01, hi) + _dot(a01, mid) + _dot(a01, lo)


def _bucket_thresholds():
    max_exact = REL_BUCKETS // 2
    ratio = REL_MAX_DIST / max_exact
    return [math.ceil(max_exact * ratio ** (k / (REL_BUCKETS - max_exact))) for k in range(1, REL_BUCKETS - max_exact)]


def _bias_kernel(tab_ref, o_ref, *, off, shape):
    m = pl.program_id(0)
    r = lax.broadcasted_iota(jnp.int32, shape, 0)
    c = lax.broadcasted_iota(jnp.int32, shape, 1)
    n = off + r - c
    nn = jnp.maximum(n, 0)
    max_exact = REL_BUCKETS // 2
    large = jnp.full(shape, max_exact, jnp.int32)
    for t in _bucket_thresholds():
        large = large + (nn >= t).astype(jnp.int32)
    bucket = jnp.where(nn < max_exact, nn, large)
    val = jnp.zeros(shape, F32)
    for b in range(REL_BUCKETS):
        val = jnp.where(bucket == b, tab_ref[b, m], val)
    val = (val - tab_ref[REL_BUCKETS - 1, m]) * LOG2E
    o_ref[0] = jnp.where(n >= 0, val, NEG)


def _bias_tiles(table, off, shape):
    return pl.pallas_call(
        functools.partial(_bias_kernel, off=off, shape=shape),
        out_shape=jax.ShapeDtypeStruct((N_MAPS,) + shape, F32),
        grid=(N_MAPS,),
        in_specs=[pl.BlockSpec(memory_space=pltpu.SMEM)],
        out_specs=pl.BlockSpec((1,) + shape, lambda m: (m, 0, 0)),
        compiler_params=_params(1),
        name="rel_bias_tiles",
    )(table)


def _proj_kernel(x_ref, g_ref, w_ref, wdt_ref, dtb_ref, gq_ref, gk_ref, gcq_ref, s64_ref,
                 ub_ref, k32_ref, v32_ref, uf_ref, dt_ref, h_scr):
    j = pl.program_id(1)

    @pl.when(j == 0)
    def _():
        h = _rms(x_ref[...], g_ref[...]).astype(BF16)
        h_scr[...] = h
        dt_ref[...] = _softplus(_dot(h, wdt_ref[...]) + dtb_ref[...])

    def proj():
        return _dot(h_scr[...], w_ref[...])

    def seg64_norm(u, gain):
        sq = (u * u).astype(BF16)
        ms = jnp.concatenate([_dot(sq[:, c * SEG_W:(c + 1) * SEG_W], s64_ref[...]) for c in range(TN // SEG_W)],
                             axis=1)
        return u * lax.rsqrt(ms + EPS) * gain

    @pl.when(j == PB_Q)
    def _():
        ub_ref[...] = (seg64_norm(proj(), gq_ref[...]) * (DK ** -0.5 * LOG2E)).astype(BF16)

    @pl.when(j == PB_CQ)
    def _():
        u = proj()
        for s in range(TN // CA_D):
            seg = u[:, s * CA_D:(s + 1) * CA_D]
            ub_ref[:, s * CA_D:(s + 1) * CA_D] = (_rms(seg, gcq_ref[...]) * (CA_D ** -0.5)).astype(BF16)

    @pl.when(j == PB_K)
    def _():
        kn = seg64_norm(proj(), gk_ref[...])
        k32_ref[...] = kn
        ub_ref[...] = kn.astype(BF16)

    @pl.when(j == PB_V)
    def _():
        u = proj()
        v32_ref[...] = u
        ub_ref[...] = u.astype(BF16)

    @pl.when(j >= PB_XBC)
    def _():
        uf_ref[...] = proj()


def _proj(x, g, w, wdt, dtb, gq, gk, gcq, s64):
    n = x.shape[0]
    tm = min(1024, n)
    nb_b = PB_XBC
    nb_f = PB_END - PB_XBC
    row = lambda i, j: (i, 0)
    return pl.pallas_call(
        _proj_kernel,
        out_shape=(jax.ShapeDtypeStruct((n, nb_b * TN), BF16),
                   jax.ShapeDtypeStruct((n, D), F32),
                   jax.ShapeDtypeStruct((n, D), F32),
                   jax.ShapeDtypeStruct((n, nb_f * TN), F32),
                   jax.ShapeDtypeStruct((n, 128), F32)),
        grid=(n // tm, PB_END),
        in_specs=[pl.BlockSpec((tm, D), row),
                  pl.BlockSpec((1, D), lambda i, j: (0, 0)),
                  pl.BlockSpec((D, TN), lambda i, j: (0, j)),
                  pl.BlockSpec((D, 128), lambda i, j: (0, 0)),
                  pl.BlockSpec((1, 128), lambda i, j: (0, 0)),
                  pl.BlockSpec((1, TN), lambda i, j: (0, 0)),
                  pl.BlockSpec((1, TN), lambda i, j: (0, 0)),
                  pl.BlockSpec((1, CA_D), lambda i, j: (0, 0)),
                  pl.BlockSpec((SEG_W, SEG_W), lambda i, j: (0, 0))],
        out_specs=(pl.BlockSpec((tm, TN), lambda i, j: (i, jnp.minimum(j, nb_b - 1))),
                   pl.BlockSpec((tm, TN), row),
                   pl.BlockSpec((tm, TN), row),
                   pl.BlockSpec((tm, TN), lambda i, j: (i, jnp.clip(j - PB_XBC, 0, nb_f - 1))),
                   pl.BlockSpec((tm, 128), row)),
        scratch_shapes=[pltpu.VMEM((tm, D), BF16)],
        compiler_params=_params(2, 56 << 20),
        name="in_proj",
    )(x, g, w, wdt, dtb, gq, gk, gcq, s64)


def _memkv_kernel(m_ref, g_ref, w_ref, gk_ref, mk_ref, mv_ref, h_scr):
    j = pl.program_id(1)

    @pl.when(j == 0)
    def _():
        h_scr[...] = _rms(m_ref[...], g_ref[...]).astype(BF16)

    u = _dot(h_scr[...], w_ref[...])

    @pl.when(j == 0)
    def _():
        for s in range(CA_HEADS):
            mk_ref[:, s * CA_D:(s + 1) * CA_D] = _rms(u[:, s * CA_D:(s + 1) * CA_D], gk_ref[...])

    @pl.when(j == 1)
    def _():
        mv_ref[...] = u


def _memkv(mem, g, w, gk):
    n = mem.shape[0]
    tm = min(512, n)
    return pl.pallas_call(
        _memkv_kernel,
        out_shape=(jax.ShapeDtypeStruct((n, D), F32), jax.ShapeDtypeStruct((n, D), F32)),
        grid=(n // tm, 2),
        in_specs=[pl.BlockSpec((tm, D), lambda i, j: (i, 0)),
                  pl.BlockSpec((1, D), lambda i, j: (0, 0)),
                  pl.BlockSpec((D, D), lambda i, j: (0, j)),
                  pl.BlockSpec((1, CA_D), lambda i, j: (0, 0))],
        out_specs=(pl.BlockSpec((tm, D), lambda i, j: (i, 0)), pl.BlockSpec((tm, D), lambda i, j: (i, 0))),
        scratch_shapes=[pltpu.VMEM((tm, D), BF16)],
        compiler_params=_params(2),
        name="mem_kv",
    )(mem, g, w, gk)


def _lambda(lam_ref, lam0):
    lv = lam_ref[...]
    s1 = jnp.sum(lv[0:1] * lv[1:2], axis=-1, keepdims=True)
    s2 = jnp.sum(lv[2:3] * lv[3:4], axis=-1, keepdims=True)
    return jnp.exp(s1) - jnp.exp(s2) + lam0


def _softmax_step(state, s, v_bf16):
    m, l, acc = state
    m_new = jnp.maximum(m, jnp.max(s, axis=-1, keepdims=True))
    alpha = jnp.exp2(m - m_new)
    p = jnp.exp2(s - jnp.tile(m_new, (1, s.shape[1] // m_new.shape[1])))
    l = alpha * l + jnp.sum(p, axis=-1, keepdims=True)
    acc = alpha * acc + _dot(p.astype(BF16), v_bf16)
    return m_new, l, acc


def _attn_prompt_kernel(lam_ref, q_ref, k_ref, v_ref, bias_ref, gout_ref, o_ref, *stat_refs, tq, lam0):
    stats = (stat_refs[0:3], stat_refs[3:6])
    qi = pl.program_id(2)
    q = q_ref[...]
    lane = lax.broadcasted_iota(jnp.int32, q.shape, 1)
    qm = (jnp.where(lane < DK, q, jnp.zeros_like(q)), jnp.where(lane >= DK, q, jnp.zeros_like(q)))
    for m_ref, l_ref, acc_ref in stats:
        m_ref[...] = jnp.full(m_ref.shape, NEG, F32)
        l_ref[...] = jnp.zeros(l_ref.shape, F32)
        acc_ref[...] = jnp.zeros(acc_ref.shape, F32)

    def block(kb, bias_idx):
        start = pl.multiple_of(kb * tq, tq)
        k = k_ref[pl.ds(start, tq), :]
        v = v_ref[pl.ds(start, tq), :]
        old = [tuple(r[...] for r in stats[mi]) for mi in range(2)]
        new = []
        for mi in range(2):
            s = _dot_nt(qm[mi], k)
            if bias_idx is not None:
                s = s + bias_ref[mi, bias_idx]
            new.append(_softmax_step(old[mi], s, v))
        for mi in range(2):
            for r, val in zip(stats[mi], new[mi]):
                r[...] = val

    def loop(lo, hi, bias_idx):
        def body(kb, carry):
            block(kb, bias_idx)
            return carry
        lax.fori_loop(lo, hi, body, 0)

    loop(0, qi - 1, None)
    loop(jnp.maximum(qi - 1, 0), qi, 1)
    block(qi, 0)

    lam = _lambda(lam_ref, lam0)
    (_, l0, a0), (_, l1, a1) = stats
    o = a0[...] / l0[...] - lam * (a1[...] / l1[...])
    o_ref[...] = (_rms(o, gout_ref[...]) * (1.0 - lam0)).astype(o_ref.dtype)


def _attn_prompt(ub3, lam_vecs, bias, gout, *, lam0, tq):
    b, t, _ = ub3.shape
    kcol, vcol = (PB_K * TN) // DV, (PB_V * TN) // DV
    return pl.pallas_call(
        functools.partial(_attn_prompt_kernel, tq=tq, lam0=lam0),
        out_shape=jax.ShapeDtypeStruct((b, t, D), BF16),
        grid=(b, N_HEADS, t // tq),
        in_specs=[pl.BlockSpec((4, DK), lambda b, h, i: (0, 0)),
                  pl.BlockSpec((None, tq, DV), lambda b, h, i: (b, i, h)),
                  pl.BlockSpec((None, t, DV), lambda b, h, i: (b, 0, kcol + h)),
                  pl.BlockSpec((None, t, DV), lambda b, h, i: (b, 0, vcol + h)),
                  pl.BlockSpec((2, 2, tq, tq), lambda b, h, i: (h, 0, 0, 0)),
                  pl.BlockSpec((1, DV), lambda b, h, i: (0, 0))],
        out_specs=pl.BlockSpec((None, tq, DV), lambda b, h, i: (b, i, h)),
        scratch_shapes=[pltpu.VMEM((tq, DV), F32)] * 6,
        compiler_params=_params(3),
        name="diff_attn_prompt",
    )(lam_vecs, ub3, ub3, ub3, bias, gout)


def _head_queries(q, h):
    qh = q[:, h * DV:(h + 1) * DV]
    lane = lax.broadcasted_iota(jnp.int32, qh.shape, 1)
    return jnp.concatenate([jnp.where(lane < DK, qh, 0.0), jnp.where(lane >= DK, qh, 0.0)], axis=0).astype(BF16)


def _attn_sample_kernel(pt_ref, q_ref, kn_ref, vn_ref, bpast_ref, bnew_ref, lam_ref, gout_ref, *rest, npg, lam0):
    k_refs, v_refs = rest[:npg], rest[npg:2 * npg]
    o_ref, m_scr, l_scr, acc_scr = rest[2 * npg:]
    c = pl.program_id(1)
    last = c == pl.num_programs(1) - 1
    tdec = q_ref.shape[0]
    rows = 2 * tdec

    @pl.when(c == 0)
    def _():
        m_scr[...] = jnp.full(m_scr.shape, NEG, F32)
        l_scr[...] = jnp.zeros(l_scr.shape, F32)
        acc_scr[...] = jnp.zeros(acc_scr.shape, F32)

    q = q_ref[...]
    qms = [_head_queries(q, h) for h in range(N_HEADS)]

    def update(s, pv_fn):
        m_old = m_scr[...]
        m_new = jnp.maximum(m_old, jnp.max(s, axis=-1, keepdims=True))
        alpha = jnp.exp2(m_old - m_new)
        p = jnp.exp2(s - m_new[:, 0:1])
        l_scr[...] = alpha * l_scr[...] + jnp.sum(p, axis=-1, keepdims=True)
        acc_scr[...] = alpha * acc_scr[...] + pv_fn(p.astype(BF16))
        m_scr[...] = m_new

    near = last.astype(F32)
    s_pages = []
    for i in range(npg):
        s_i = jnp.concatenate([_dot(qms[h], k_refs[i][h * DV:(h + 1) * DV, :].astype(BF16))
                               for h in range(N_HEADS)], axis=0)
        if i == npg - 1:
            s_i = s_i + near * bpast_ref[...]
        s_pages.append(s_i)

    def pv_pages(p):
        heads = []
        for h in range(N_HEADS):
            out = None
            for i in range(npg):
                v_h = v_refs[i][pl.ds(h, PAGE, stride=N_HEADS), :].astype(BF16)
                part = _dot(p[h * rows:(h + 1) * rows, i * PAGE:(i + 1) * PAGE], v_h)
                out = part if out is None else out + part
            heads.append(out)
        return jnp.concatenate(heads, axis=0)

    update(jnp.concatenate(s_pages, axis=1), pv_pages)

    @pl.when(last)
    def _():
        pad = jnp.zeros((PAGE - tdec, D), F32)
        kn = jnp.concatenate([kn_ref[...], pad], axis=0).astype(BF16)
        vn = jnp.concatenate([vn_ref[...], pad], axis=0).astype(BF16)
        s_new = jnp.concatenate([_dot_nt(qms[h], kn[:, h * DV:(h + 1) * DV]) for h in range(N_HEADS)], axis=0)
        update(s_new + bnew_ref[...],
               lambda p: jnp.concatenate([_dot(p[h * rows:(h + 1) * rows], vn[:, h * DV:(h + 1) * DV])
                                          for h in range(N_HEADS)], axis=0))
        lam = _lambda(lam_ref, lam0)
        o_all = acc_scr[...] / l_scr[...]
        for h in range(N_HEADS):
            o = o_all[h * rows:h * rows + tdec] - lam * o_all[h * rows + tdec:(h + 1) * rows]
            o_ref[:, h * DV:(h + 1) * DV] = _rms(o, gout_ref[...]) * (1.0 - lam0)


def _attn_sample(page_table, q3, kn3, vn3, cache_k4, cache_v4, layer, bpast, bnew, lam_vecs, gout, *, lam0):
    bs, tdec, _ = q3.shape
    n_pages = page_table.shape[1]
    npg = math.gcd(n_pages, 16)
    pt = page_table.reshape(-1)

    def page_spec(i):
        return pl.BlockSpec((None, None, D, PAGE),
                            lambda b, c, pt_ref: (layer, pt_ref[b * n_pages + c * npg + i], 0, 0))

    per_b = pl.BlockSpec((None, tdec, D), lambda b, c, pt_ref: (b, 0, 0))
    whole = lambda shape: pl.BlockSpec(shape, lambda b, c, pt_ref: (0,) * len(shape))
    stat = pltpu.VMEM((N_HEADS * 2 * tdec, DV), F32)
    return pl.pallas_call(
        functools.partial(_attn_sample_kernel, npg=npg, lam0=lam0),
        out_shape=jax.ShapeDtypeStruct((bs, tdec, D), F32),
        grid_spec=pltpu.PrefetchScalarGridSpec(
            num_scalar_prefetch=1,
            grid=(bs, n_pages // npg),
            in_specs=[per_b, per_b, per_b,
                      whole((N_HEADS * 2 * tdec, PAGE)), whole((N_HEADS * 2 * tdec, PAGE)),
                      whole((4, DK)), whole((1, DV))]
                     + [page_spec(i) for i in range(npg)] + [page_spec(i) for i in range(npg)],
            out_specs=per_b,
            scratch_shapes=[stat, stat, stat]),
        compiler_params=_params(2),
        name="diff_attn_sample",
    )(pt, q3, kn3, vn3, bpast, bnew, lam_vecs, gout, *([cache_k4] * npg), *([cache_v4] * npg))


def _cross_kernel(cq_ref, mk_ref, mv_ref, o_ref):
    rows = cq_ref.shape[0]
    for h in range(CA_HEADS):
        cols = slice(h * CA_D, (h + 1) * CA_D)
        q = cq_ref[:, cols]
        if rows % 16:
            q = jnp.concatenate([q.astype(F32), jnp.zeros((16 - rows % 16, CA_D), F32)], axis=0)
        s = _dot_nt(q.astype(BF16), mk_ref[:, cols].astype(BF16))
        p = jnp.exp(s - jnp.max(s, axis=-1, keepdims=True))
        o = _dot(p.astype(BF16), mv_ref[:, cols].astype(BF16)) / jnp.sum(p, axis=-1, keepdims=True)
        o_ref[:, cols] = o[0:rows].astype(o_ref.dtype)


def _cross(cq3, col_block, mk3, mv3, out_dtype):
    b, t, _ = cq3.shape
    tm = min(512, t)
    return pl.pallas_call(
        _cross_kernel,
        out_shape=jax.ShapeDtypeStruct((b, t, D), out_dtype),
        grid=(b, t // tm),
        in_specs=[pl.BlockSpec((None, tm, D), lambda b, i: (b, i, col_block)),
                  pl.BlockSpec((None, MEM_LEN, D), lambda b, i: (b, 0, 0)),
                  pl.BlockSpec((None, MEM_LEN, D), lambda b, i: (b, 0, 0))],
        out_specs=pl.BlockSpec((None, tm, D), lambda b, i: (b, i, 0)),
        compiler_params=_params(2),
        name="mem_cross_attn",
    )(cq3, mk3, mv3)


def _ssd_kernel(xbc_ref, z_ref, dt_ref, h0_ref, c0_ref, cw_ref, cb_ref, alog_ref, dskip_ref, gout_ref,
                y_ref, hn_ref, cn_ref, buf, xc_scr, dt_scr, yb, h_scr, *, tv):
    L = SSM_L
    c = pl.program_id(1)
    head = 8

    @pl.when(c == 0)
    def _():
        buf[0:head, :] = jnp.zeros((head, CONV_CH), F32)
        buf[head - (CONV_K - 1):head, :] = c0_ref[...]
        h_scr[...] = h0_ref[...]
        if tv < L:
            buf[head + tv:, :] = jnp.zeros((L - tv, CONV_CH), F32)
            dt_scr[...] = jnp.zeros(dt_scr.shape, F32)

    buf[head:head + tv, :] = xbc_ref[...]
    cn_ref[...] = buf[head + tv - (CONV_K - 1):head + tv, :]

    cw = cw_ref[...]
    for j in range(CONV_CH // 512):
        cols = slice(j * 512, (j + 1) * 512)
        acc = cb_ref[:, cols] + cw[CONV_K - 1:CONV_K, cols] * buf[head:head + L, cols]
        for k in range(CONV_K - 1):
            acc = acc + cw[k:k + 1, cols] * buf[head - (CONV_K - 1) + k:head - (CONV_K - 1) + k + L, cols]
        xc_scr[:, cols] = acc * _sigmoid(acc)

    if tv == L:
        buf[0:head, :] = buf[L:L + head, :]
        dt = dt_ref[...]
    else:
        dt_scr[0:tv, :] = dt_ref[...]
        dt = dt_scr[...]

    a_neg = -jnp.exp(alog_ref[...])
    d_a = dt * a_neg
    row = lax.broadcasted_iota(jnp.int32, (L, L), 0)
    col = lax.broadcasted_iota(jnp.int32, (L, L), 1)
    causal = col <= row
    a_cum = _dot_split3(causal.astype(BF16), d_a)
    a_cum_t = a_cum.T
    dt_t = dt.T
    a_last = a_cum[L - 1:L, :]
    e_cum = jnp.exp(a_cum)
    to_end = jnp.exp(a_last - a_cum) * dt
    e_last = jnp.exp(a_last)
    lane = lax.broadcasted_iota(jnp.int32, (L, 128), 1)
    first = lane < SSM_P
    row_first = lax.broadcasted_iota(jnp.int32, (128, 128), 0) < SSM_P

    for g in range(SSM_G):
        b_g = xc_scr[:, SSM_INNER + g * SSM_N:SSM_INNER + (g + 1) * SSM_N].astype(BF16)
        c_g = xc_scr[:, SSM_INNER + SSM_G * SSM_N + g * SSM_N:SSM_INNER + SSM_G * SSM_N + (g + 1) * SSM_N].astype(BF16)
        cb = _dot_nt(c_g, b_g)
        for jj in (2 * g, 2 * g + 1):
            cols = slice(jj * 128, (jj + 1) * 128)
            x_pair = xc_scr[:, cols]
            x_bf = x_pair.astype(BF16)
            yw = []
            for hh in (2 * jj, 2 * jj + 1):
                seg = a_cum[:, hh:hh + 1] - a_cum_t[hh:hh + 1, :]
                w = cb * jnp.exp(jnp.where(causal, seg, NEG)) * dt_t[hh:hh + 1, :]
                yw.append(_dot(w.astype(BF16), x_bf))
            ha, hb = 2 * jj, 2 * jj + 1
            hp = h_scr[jj]
            y = jnp.where(first, yw[0], yw[1])
            y = y + _dot_nt(c_g, hp.astype(BF16)) * jnp.where(first, e_cum[:, ha:ha + 1], e_cum[:, hb:hb + 1])
            y = y + dskip_ref[:, cols] * x_pair
            te = jnp.where(first, to_end[:, ha:ha + 1], to_end[:, hb:hb + 1])
            upd = _dot_tn((x_pair * te).astype(BF16), b_g)
            h_scr[jj] = hp * jnp.where(row_first, e_last[:, ha:ha + 1], e_last[:, hb:hb + 1]) + upd
            yb[:, cols] = y[0:tv] * (z_ref[:, cols] * _sigmoid(z_ref[:, cols]))

    gw = SSM_INNER // SSM_G
    for g in range(SSM_G):
        cols = slice(g * gw, (g + 1) * gw)
        y_ref[:, cols] = _rms(yb[:, cols], gout_ref[:, cols]).astype(y_ref.dtype)
    hn_ref[...] = h_scr[...]


def _ssd(xbc_src, xbc_blk, z_src, z_blk, dt3, h0, c0, cw, cb, alog, dskip, gout, out_dtype):
    b, t, _ = dt3.shape
    tv = min(SSM_L, t)
    nc = t // tv
    seq = lambda width, blk: pl.BlockSpec((None, tv, width), lambda b, c: (b, c, blk))
    per_b = lambda shape: pl.BlockSpec((None,) + shape, lambda b, c: (b,) + (0,) * len(shape))
    whole = lambda shape: pl.BlockSpec(shape, lambda b, c: (0,) * len(shape))
    return pl.pallas_call(
        functools.partial(_ssd_kernel, tv=tv),
        out_shape=(jax.ShapeDtypeStruct((b, t, SSM_INNER), out_dtype),
                   jax.ShapeDtypeStruct((b, N_PAIRS, 128, SSM_N), F32),
                   jax.ShapeDtypeStruct((b, CONV_K - 1, CONV_CH), F32)),
        grid=(b, nc),
        in_specs=[seq(CONV_CH, xbc_blk), seq(SSM_INNER, z_blk), seq(128, 0),
                  per_b((N_PAIRS, 128, SSM_N)), per_b((CONV_K - 1, CONV_CH)),
                  whole((CONV_K, CONV_CH)), whole((1, CONV_CH)), whole((1, 128)),
                  whole((1, SSM_INNER)), whole((1, SSM_INNER))],
        out_specs=(seq(SSM_INNER, 0), per_b((N_PAIRS, 128, SSM_N)), per_b((CONV_K - 1, CONV_CH))),
        scratch_shapes=[pltpu.VMEM((SSM_L + 8, CONV_CH), F32), pltpu.VMEM((SSM_L, CONV_CH), F32),
                        pltpu.VMEM((SSM_L, 128), F32), pltpu.VMEM((tv, SSM_INNER), F32),
                        pltpu.VMEM((N_PAIRS, 128, SSM_N), F32)],
        compiler_params=_params(2),
        name="conv_ssd",
    )(xbc_src, z_src, dt3, h0, c0, cw, cb, alog, dskip, gout)


def _merge_kernel(x_ref, gmix_ref, wg_ref, oa_ref, ys_ref, oc_ref, wa_ref, ws_ref, wc_ref, wo_ref, o_ref):
    x = x_ref[...]
    h = _rms(x, gmix_ref[...]).astype(BF16)
    branches = ((oa_ref, wa_ref), (ys_ref, ws_ref), (oc_ref, wc_ref))
    merged = None
    for b, (act_ref, w_ref) in enumerate(branches):
        gate = _sigmoid(_dot(h, wg_ref[:, b * D:(b + 1) * D]))
        term = gate * _dot(act_ref[...].astype(BF16), w_ref[...])
        merged = term if merged is None else merged + term
    o_ref[...] = x + _dot(merged.astype(BF16), wo_ref[...])


def _merge(x, gmix, wg, oa, ys, oc, wa, ws, wc, wo):
    n = x.shape[0]
    tm = min(512, n)
    row = lambda width: pl.BlockSpec((tm, width), lambda i: (i, 0))
    whole = lambda shape: pl.BlockSpec(shape, lambda i: (0, 0), pipeline_mode=pl.Buffered(1))
    return pl.pallas_call(
        _merge_kernel,
        out_shape=jax.ShapeDtypeStruct((n, D), F32),
        grid=(n // tm,),
        in_specs=[row(D), whole((1, D)), whole((D, 3 * D)), row(D), row(SSM_INNER), row(D),
                  whole((D, D)), whole((SSM_INNER, D)), whole((D, D)), whole((D, D))],
        out_specs=row(D),
        compiler_params=_params(1),
        name="branch_merge",
    )(x, gmix, wg, oa, ys, oc, wa, ws, wc, wo)


def _mlp_kernel(x_ref, g_ref, wu_ref, wd_ref, o_ref, h_scr, acc):
    f = pl.program_id(1)

    @pl.when(f == 0)
    def _():
        x = x_ref[...]
        h_scr[...] = _rms(x, g_ref[...]).astype(BF16)
        acc[...] = x

    a = jnp.maximum(_dot(h_scr[...], wu_ref[...]), 0.0)
    acc[...] += _dot((a * a).astype(BF16), wd_ref[...])

    @pl.when(f == pl.num_programs(1) - 1)
    def _():
        o_ref[...] = acc[...]


def _mlp(x, g, wu, wd):
    n = x.shape[0]
    tm = min(1024, n)
    tf = 1024
    return pl.pallas_call(
        _mlp_kernel,
        out_shape=jax.ShapeDtypeStruct((n, D), F32),
        grid=(n // tm, D_FF // tf),
        in_specs=[pl.BlockSpec((tm, D), lambda i, f: (i, 0)),
                  pl.BlockSpec((1, D), lambda i, f: (0, 0)),
                  pl.BlockSpec((D, tf), lambda i, f: (0, f)),
                  pl.BlockSpec((tf, D), lambda i, f: (f, 0))],
        out_specs=pl.BlockSpec((tm, D), lambda i, f: (i, 0)),
        scratch_shapes=[pltpu.VMEM((tm, D), BF16), pltpu.VMEM((tm, D), F32)],
        compiler_params=_params(2),
        name="relu2_mlp",
    )(x, g, wu, wd)


def _lambda_init(layer):
    return 0.8 - 0.6 * math.exp(-0.3 * layer)


def _layer_weights(l, p):
    w_in = p['w_in'][l]
    cols = lambda off, width: w_in[:, off:off + width]
    w_main = jnp.concatenate([cols(OFF_Q, D), cols(OFF_CQ, D), cols(OFF_K, D), cols(OFF_V, D),
                              cols(OFF_XBC, CONV_CH), cols(OFF_Z, SSM_INNER)], axis=1).astype(BF16)
    w_gate = cols(OFF_G, 3 * D).astype(BF16)
    w_dt = jnp.pad(cols(OFF_DT, SSM_HEADS), ((0, 0), (0, 128 - SSM_HEADS))).astype(BF16)
    row = lambda v: v.reshape(1, -1).astype(F32)
    return dict(
        g_mix=row(p['g_mix'][l]), w_main=w_main, w_dt=w_dt, w_gate=w_gate,
        dt_bias=jnp.pad(row(p['dt_bias'][l]), ((0, 0), (0, 128 - SSM_HEADS))),
        gq=row(jnp.tile(p['g_da_q'][l], TN // DK)), gk=row(jnp.tile(p['g_da_k'][l], TN // DK)),
        gcq=row(p['g_ca_q'][l]),
        lam=jnp.stack([p['lambda_q1'][l], p['lambda_k1'][l], p['lambda_q2'][l], p['lambda_k2'][l]]).astype(F32),
        g_da_out=row(p['g_da_out'][l]),
        conv_w=p['conv_w'][l].astype(F32), conv_b=row(p['conv_b'][l]),
        a_log=jnp.pad(row(p['a_log'][l]), ((0, 0), (0, 128 - SSM_HEADS))),
        d_skip=row(jnp.repeat(p['d_skip'][l], SSM_P)), g_ssm_out=row(p['g_ssm_out'][l]),
        w_br_attn=p['w_br_attn'][l].astype(BF16), w_br_ssm=p['w_br_ssm'][l].astype(BF16),
        w_br_cross=p['w_br_cross'][l].astype(BF16), w_out=p['w_out'][l].astype(BF16),
        g_mlp=row(p['g_mlp'][l]), w_up=p['w_up'][l].astype(BF16), w_down=p['w_down'][l].astype(BF16),
        g_mem=row(p['g_mem'][l]), w_mem_kv=p['w_mem_kv'][l].astype(BF16), g_ca_k=row(p['g_ca_k'][l]),
    )


def _seg64_matrix():
    i = jnp.arange(SEG_W) // DK
    return jnp.where(i[:, None] == i[None, :], 1.0 / DK, 0.0).astype(BF16)


def _layer(x, bsz, w, s64, attn_fn, mem_k3, mem_v3, h0, c0, act_dtype):
    n = x.shape[0]
    t = n // bsz
    ub, k32, v32, uf, dt = _proj(x, w['g_mix'], w['w_main'], w['w_dt'], w['dt_bias'], w['gq'], w['gk'], w['gcq'], s64)
    ub3 = ub.reshape(bsz, t, -1)
    o_a = attn_fn(ub3, k32, v32)
    uf3 = uf.reshape(bsz, t, -1)
    y_n, h_new, c_new = _ssd(uf3, 0, uf3, ((PB_Z - PB_XBC) * TN) // SSM_INNER, dt.reshape(bsz, t, 128), h0, c0,
                             w['conv_w'], w['conv_b'], w['a_log'], w['d_skip'], w['g_ssm_out'], act_dtype)
    if act_dtype == BF16:
        cq3, cq_blk = ub3, (PB_CQ * TN) // D
    else:
        cq3, cq_blk = ub3[:, :, PB_CQ * TN:PB_K * TN].astype(F32), 0
    o_c = _cross(cq3, cq_blk, mem_k3, mem_v3, act_dtype)
    x = _merge(x, w['g_mix'], w['w_gate'], o_a.reshape(n, D), y_n.reshape(n, SSM_INNER), o_c.reshape(n, D),
               w['w_br_attn'], w['w_br_ssm'], w['w_br_cross'], w['w_out'])
    x = _mlp(x, w['g_mlp'], w['w_up'], w['w_down'])
    return x, k32, v32, h_new, c_new


def kernel(x_prompt, x_sample, cache_k, cache_v, cache_mem_k, cache_mem_v, state_ssm, state_conv, page_table, mem_prompt, rel_bias_table, g_mix, w_in, g_da_q, g_da_k, lambda_q1, lambda_k1, lambda_q2, lambda_k2, g_da_out, w_br_attn, conv_w, conv_b, dt_bias, a_log, d_skip, g_ssm_out, w_br_ssm, g_mem, w_mem_kv, g_ca_q, g_ca_k, w_br_cross, w_out, g_mlp, w_up, w_down):
    p = dict(g_mix=g_mix, w_in=w_in, g_da_q=g_da_q, g_da_k=g_da_k, lambda_q1=lambda_q1, lambda_k1=lambda_k1,
             lambda_q2=lambda_q2, lambda_k2=lambda_k2, g_da_out=g_da_out, w_br_attn=w_br_attn, conv_w=conv_w,
             conv_b=conv_b, dt_bias=dt_bias, a_log=a_log, d_skip=d_skip, g_ssm_out=g_ssm_out, w_br_ssm=w_br_ssm,
             g_mem=g_mem, w_mem_kv=w_mem_kv, g_ca_q=g_ca_q, g_ca_k=g_ca_k, w_br_cross=w_br_cross, w_out=w_out,
             g_mlp=g_mlp, w_up=w_up, w_down=w_down)
    depth = w_in.shape[0]
    bp, tp, _ = x_prompt.shape
    bs, ts, _ = x_sample.shape
    n_phys = cache_k.shape[1]
    past_len = page_table.shape[1] * PAGE
    tq = min(512, tp)
    table = rel_bias_table.astype(F32)
    s64 = _seg64_matrix()

    bias_p = jnp.stack([_bias_tiles(table, 0, (tq, tq)), _bias_tiles(table, tq, (tq, tq))], axis=1)
    as_heads = lambda a: a.reshape(N_HEADS * 2 * ts, PAGE)
    bias_past = as_heads(_bias_tiles(table, PAGE, (ts, PAGE)))
    bias_new = as_heads(_bias_tiles(table, 0, (ts, PAGE)))

    cache_k4 = jnp.transpose(cache_k, (0, 1, 3, 4, 2)).reshape(depth, n_phys, D, PAGE)
    cache_v4 = cache_v.reshape(depth, n_phys, PAGE * N_HEADS, DV)
    mem_flat = mem_prompt.reshape(bp * MEM_LEN, D)
    zero_h = jnp.zeros((bp, N_PAIRS, 128, SSM_N), F32)
    zero_c = jnp.zeros((bp, CONV_K - 1, CONV_CH), F32)

    xp = x_prompt.reshape(bp * tp, D)
    xs = x_sample.reshape(bs * ts, D)
    outs = {k: [] for k in ('kp', 'vp', 'mk', 'mv', 'hp', 'cp', 'ks', 'vs', 'hs', 'cs')}
    for l in range(depth):
        w = _layer_weights(l, p)
        lam0 = _lambda_init(l)

        mk, mv = _memkv(mem_flat, w['g_mem'], w['w_mem_kv'], w['g_ca_k'])
        attn_p = lambda ub3, k32, v32: _attn_prompt(ub3, w['lam'], bias_p, w['g_da_out'], lam0=lam0, tq=tq)
        xp, k, v, hn, cn = _layer(xp, bp, w, s64, attn_p, mk.reshape(bp, MEM_LEN, D), mv.reshape(bp, MEM_LEN, D),
                                  zero_h, zero_c, BF16)
        for key, val in zip(('kp', 'vp', 'mk', 'mv', 'hp', 'cp'), (k, v, mk, mv, hn, cn)):
            outs[key].append(val)

        def attn_s(ub3, k32, v32):
            q3 = ub3[:, :, 0:D].astype(F32)
            return _attn_sample(page_table, q3, k32.reshape(bs, ts, D), v32.reshape(bs, ts, D), cache_k4, cache_v4,
                                l, bias_past, bias_new, w['lam'], w['g_da_out'], lam0=lam0)

        xs, k, v, hn, cn = _layer(xs, bs, w, s64, attn_s, cache_mem_k[l].reshape(bs, MEM_LEN, D),
                                  cache_mem_v[l].reshape(bs, MEM_LEN, D),
                                  state_ssm[l].reshape(bs, N_PAIRS, 128, SSM_N), state_conv[l], F32)
        for key, val in zip(('ks', 'vs', 'hs', 'cs'), (k, v, hn, cn)):
            outs[key].append(val)

    st = lambda key, shape: jnp.stack(outs[key]).reshape((depth,) + shape)
    return (xp.reshape(bp, tp, D), xs.reshape(bs, ts, D),
            st('kp', (bp, tp, N_MAPS, DK)), st('vp', (bp, tp, N_HEADS, DV)),
            st('mk', (bp, MEM_LEN, CA_HEADS, CA_D)), st('mv', (bp, MEM_LEN, CA_HEADS, CA_D)),
            st('hp', (bp, SSM_HEADS, SSM_P, SSM_N)), st('cp', (bp, CONV_K - 1, CONV_CH)),
            st('ks', (bs, ts, N_MAPS, DK)), st('vs', (bs, ts, N_HEADS, DV)),
            st('hs', (bs, SSM_HEADS, SSM_P, SSM_N)), st('cs', (bs, CONV_K - 1, CONV_CH)))
```

```python
import functools
import math

import jax
import jax.numpy as jnp
from jax import lax
from jax.experimental import pallas as pl
from jax.experimental.pallas import tpu as pltpu

F32, BF16 = jnp.float32, jnp.bfloat16
EPS = 1e-5
NEG = -1e30
LOG2E = math.log2(math.e)

D = 1024
N_MAPS, DK = 16, 64
N_HEADS, DV = 8, 128
SSM_INNER, SSM_P, SSM_N, SSM_G = 2048, 64, 128, 8
SSM_HEADS = SSM_INNER // SSM_P
N_PAIRS = SSM_HEADS // 2
CONV_K, CONV_CH = 4, 4096
MEM_LEN, CA_HEADS, CA_D = 256, 4, 256
D_FF = 4096
PAGE = 128
REL_BUCKETS, REL_MAX_DIST = 32, 128
SSM_L = 128

OFF_Q, OFF_K, OFF_V, OFF_Z, OFF_XBC = 0, 1024, 2048, 3072, 5120
OFF_DT = OFF_XBC + CONV_CH
OFF_CQ = OFF_DT + SSM_HEADS
OFF_G = OFF_CQ + D

TN = 1024
SEG_W = 256
PB_Q, PB_CQ, PB_K, PB_V, PB_XBC, PB_Z, PB_END = 0, 1, 2, 3, 4, 8, 10
VMEM_LIMIT = 48 << 20


def _params(n_axes, vmem=VMEM_LIMIT):
    return pltpu.CompilerParams(dimension_semantics=("arbitrary",) * n_axes, vmem_limit_bytes=vmem)


def _rms(x, gain):
    ms = jnp.mean(x * x, axis=-1, keepdims=True)
    return x * lax.rsqrt(ms + EPS) * gain


def _sigmoid(x):
    return 1.0 / (1.0 + jnp.exp(-x))


def _softplus(x):
    return jnp.maximum(x, 0.0) + jnp.log1p(jnp.exp(-jnp.abs(x)))


def _dot(a, b):
    return jnp.dot(a, b, preferred_element_type=F32)


def _dot_nt(a, b):
    return lax.dot_general(a, b, (((1,), (1,)), ((), ())), preferred_element_type=F32)


def _dot_tn(a, b):
    return lax.dot_general(a, b, (((0,), (0,)), ((), ())), preferred_element_type=F32)


def _dot_split3(a01, x):
    hi = x.astype(BF16)
    r = x - hi.astype(F32)
    mid = r.astype(BF16)
    lo = (r - mid.astype(F32)).astype(BF16)
    return _dot(a01, hi) + _dot(a01, mid) + _dot(a01, lo)


def _bucket_thresholds():
    max_exact = REL_BUCKETS // 2
    ratio = REL_MAX_DIST / max_exact
    return [math.ceil(max_exact * ratio ** (k / (REL_BUCKETS - max_exact))) for k in range(1, REL_BUCKETS - max_exact)]


def _bias_kernel(tab_ref, o_ref, *, offs, shape):
    m = pl.program_id(0)
    i = pl.program_id(1)
    off = jnp.int32(offs[0])
    for idx in range(1, len(offs)):
        off = jnp.where(i == idx, offs[idx], off)
    r = lax.broadcasted_iota(jnp.int32, shape, 0)
    c = lax.broadcasted_iota(jnp.int32, shape, 1)
    n = off + r - c
    nn = jnp.maximum(n, 0)
    max_exact = REL_BUCKETS // 2
    large = jnp.full(shape, max_exact, jnp.int32)
    for t in _bucket_thresholds():
        large = large + (nn >= t).astype(jnp.int32)
    bucket = jnp.where(nn < max_exact, nn, large)
    val = jnp.zeros(shape, F32)
    for b in range(REL_BUCKETS):
        val = jnp.where(bucket == b, tab_ref[b, m], val)
    val = (val - tab_ref[REL_BUCKETS - 1, m]) * LOG2E
    o_ref[0, 0] = jnp.where(n >= 0, val, NEG)


def _bias_tiles(table, offs, shape):
    return pl.pallas_call(
        functools.partial(_bias_kernel, offs=offs, shape=shape),
        out_shape=jax.ShapeDtypeStruct((N_MAPS, len(offs)) + shape, F32),
        grid=(N_MAPS, len(offs)),
        in_specs=[pl.BlockSpec(memory_space=pltpu.SMEM)],
        out_specs=pl.BlockSpec((1, 1) + shape, lambda m, i: (m, i, 0, 0)),
        compiler_params=_params(2),
        name="rel_bias_tiles",
    )(table)


def _proj_kernel(x_ref, g_ref, w_ref, wdt_ref, dtb_ref, gq_ref, gk_ref, gcq_ref, s64_ref,
                 ub_ref, k32_ref, v32_ref, uf_ref, dt_ref, h_scr):
    j = pl.program_id(1)

    @pl.when(j == 0)
    def _():
        h = _rms(x_ref[...], g_ref[...]).astype(BF16)
        h_scr[...] = h
        dt_ref[...] = _softplus(_dot_nt(h, wdt_ref[...]) + dtb_ref[...])

    def proj():
        return _dot_nt(h_scr[...], w_ref[...])

    def seg64_norm(u, gain):
        sq = (u * u).astype(BF16)
        ms = jnp.concatenate([_dot(sq[:, c * SEG_W:(c + 1) * SEG_W], s64_ref[...]) for c in range(TN // SEG_W)],
                             axis=1)
        return u * lax.rsqrt(ms + EPS) * gain

    @pl.when(j == PB_Q)
    def _():
        ub_ref[...] = (seg64_norm(proj(), gq_ref[...]) * (DK ** -0.5 * LOG2E)).astype(BF16)

    @pl.when(j == PB_CQ)
    def _():
        u = proj()
        for s in range(TN // CA_D):
            seg = u[:, s * CA_D:(s + 1) * CA_D]
            ub_ref[:, s * CA_D:(s + 1) * CA_D] = (_rms(seg, gcq_ref[...]) * (CA_D ** -0.5)).astype(BF16)

    @pl.when(j == PB_K)
    def _():
        kn = seg64_norm(proj(), gk_ref[...])
        k32_ref[...] = kn
        ub_ref[...] = kn.astype(BF16)

    @pl.when(j == PB_V)
    def _():
        u = proj()
        v32_ref[...] = u
        ub_ref[...] = u.astype(BF16)

    @pl.when(j >= PB_XBC)
    def _():
        uf_ref[...] = proj()


def _proj(x, g, w, wdt, dtb, gq, gk, gcq, s64):
    n = x.shape[0]
    tm = min(1024, n)
    nb_b = PB_XBC
    nb_f = PB_END - PB_XBC
    row = lambda i, j: (i, 0)
    return pl.pallas_call(
        _proj_kernel,
        out_shape=(jax.ShapeDtypeStruct((n, nb_b * TN), BF16),
                   jax.ShapeDtypeStruct((n, D), F32),
                   jax.ShapeDtypeStruct((n, D), F32),
                   jax.ShapeDtypeStruct((n, nb_f * TN), F32),
                   jax.ShapeDtypeStruct((n, 128), F32)),
        grid=(n // tm, PB_END),
        in_specs=[pl.BlockSpec((tm, D), row),
                  pl.BlockSpec((1, D), lambda i, j: (0, 0)),
                  pl.BlockSpec((TN, D), lambda i, j: (j, 0)),
                  pl.BlockSpec((128, D), lambda i, j: (0, 0)),
                  pl.BlockSpec((1, 128), lambda i, j: (0, 0)),
                  pl.BlockSpec((1, TN), lambda i, j: (0, 0)),
                  pl.BlockSpec((1, TN), lambda i, j: (0, 0)),
                  pl.BlockSpec((1, CA_D), lambda i, j: (0, 0)),
                  pl.BlockSpec((SEG_W, SEG_W), lambda i, j: (0, 0))],
        out_specs=(pl.BlockSpec((tm, TN), lambda i, j: (i, jnp.minimum(j, nb_b - 1))),
                   pl.BlockSpec((tm, TN), row),
                   pl.BlockSpec((tm, TN), row),
                   pl.BlockSpec((tm, TN), lambda i, j: (i, jnp.clip(j - PB_XBC, 0, nb_f - 1))),
                   pl.BlockSpec((tm, 128), row)),
        scratch_shapes=[pltpu.VMEM((tm, D), BF16)],
        compiler_params=_params(2, 56 << 20),
        name="in_proj",
    )(x, g, w, wdt, dtb, gq, gk, gcq, s64)


def _memkv_kernel(m_ref, g_ref, w_ref, gk_ref, mk_ref, mv_ref, h_scr):
    j = pl.program_id(1)

    @pl.when(j == 0)
    def _():
        h_scr[...] = _rms(m_ref[...], g_ref[...]).astype(BF16)

    u = _dot(h_scr[...], w_ref[...])

    @pl.when(j == 0)
    def _():
        for s in range(CA_HEADS):
            mk_ref[:, s * CA_D:(s + 1) * CA_D] = _rms(u[:, s * CA_D:(s + 1) * CA_D], gk_ref[...])

    @pl.when(j == 1)
    def _():
        mv_ref[...] = u


def _memkv(mem, g, w, gk):
    n = mem.shape[0]
    tm = min(512, n)
    return pl.pallas_call(
        _memkv_kernel,
        out_shape=(jax.ShapeDtypeStruct((n, D), F32), jax.ShapeDtypeStruct((n, D), F32)),
        grid=(n // tm, 2),
        in_specs=[pl.BlockSpec((tm, D), lambda i, j: (i, 0)),
                  pl.BlockSpec((1, D), lambda i, j: (0, 0)),
                  pl.BlockSpec((D, D), lambda i, j: (0, j)),
                  pl.BlockSpec((1, CA_D), lambda i, j: (0, 0))],
        out_specs=(pl.BlockSpec((tm, D), lambda i, j: (i, 0)), pl.BlockSpec((tm, D), lambda i, j: (i, 0))),
        scratch_shapes=[pltpu.VMEM((tm, D), BF16)],
        compiler_params=_params(2),
        name="mem_kv",
    )(mem, g, w, gk)


def _lambda(lam_ref, lam0):
    lv = lam_ref[...]
    s1 = jnp.sum(lv[0:1] * lv[1:2], axis=-1, keepdims=True)
    s2 = jnp.sum(lv[2:3] * lv[3:4], axis=-1, keepdims=True)
    return jnp.exp(s1) - jnp.exp(s2) + lam0


def _softmax_step(state, s, v_bf16):
    m, l, acc = state
    m_new = jnp.maximum(m, jnp.max(s, axis=-1, keepdims=True))
    alpha = jnp.exp2(m - m_new)
    p = jnp.exp2(s - jnp.tile(m_new, (1, s.shape[1] // m_new.shape[1])))
    l = alpha * l + jnp.sum(p, axis=-1, keepdims=True)
    acc = alpha * acc + _dot(p.astype(BF16), v_bf16)
    return m_new, l, acc


def _attn_prompt_kernel(lam_ref, q_ref, k_ref, v_ref, bias_ref, gout_ref, o_ref, *stat_refs, tq, lam0):
    stats = (stat_refs[0:3], stat_refs[3:6])
    qi = pl.program_id(2)
    q = q_ref[...]
    lane = lax.broadcasted_iota(jnp.int32, q.shape, 1)
    qm = (jnp.where(lane < DK, q, jnp.zeros_like(q)), jnp.where(lane >= DK, q, jnp.zeros_like(q)))
    for m_ref, l_ref, acc_ref in stats:
        m_ref[...] = jnp.full(m_ref.shape, NEG, F32)
        l_ref[...] = jnp.zeros(l_ref.shape, F32)
        acc_ref[...] = jnp.zeros(acc_ref.shape, F32)

    def block(kb, bias_idx):
        start = pl.multiple_of(kb * tq, tq)
        k = k_ref[pl.ds(start, tq), :]
        v = v_ref[pl.ds(start, tq), :]
        old = [tuple(r[...] for r in stats[mi]) for mi in range(2)]
        new = []
        for mi in range(2):
            s = _dot_nt(qm[mi], k)
            if bias_idx is not None:
                s = s + bias_ref[mi, bias_idx]
            new.append(_softmax_step(old[mi], s, v))
        for mi in range(2):
            for r, val in zip(stats[mi], new[mi]):
                r[...] = val

    def loop(lo, hi, bias_idx):
        def body(kb, carry):
            block(kb, bias_idx)
            return carry
        lax.fori_loop(lo, hi, body, 0)

    loop(0, qi - 1, None)
    loop(jnp.maximum(qi - 1, 0), qi, 1)
    block(qi, 0)

    lam = _lambda(lam_ref, lam0)
    (_, l0, a0), (_, l1, a1) = stats
    o = a0[...] / l0[...] - lam * (a1[...] / l1[...])
    o_ref[...] = (_rms(o, gout_ref[...]) * (1.0 - lam0)).astype(o_ref.dtype)


def _attn_prompt(ub3, lam_vecs, bias, gout, *, lam0, tq):
    b, t, _ = ub3.shape
    kcol, vcol = (PB_K * TN) // DV, (PB_V * TN) // DV
    return pl.pallas_call(
        functools.partial(_attn_prompt_kernel, tq=tq, lam0=lam0),
        out_shape=jax.ShapeDtypeStruct((b, t, D), BF16),
        grid=(b, N_HEADS, t // tq),
        in_specs=[pl.BlockSpec((4, DK), lambda b, h, i: (0, 0)),
                  pl.BlockSpec((None, tq, DV), lambda b, h, i: (b, i, h)),
                  pl.BlockSpec((None, t, DV), lambda b, h, i: (b, 0, kcol + h)),
                  pl.BlockSpec((None, t, DV), lambda b, h, i: (b, 0, vcol + h)),
                  pl.BlockSpec((2, 2, tq, tq), lambda b, h, i: (h, 0, 0, 0)),
                  pl.BlockSpec((1, DV), lambda b, h, i: (0, 0))],
        out_specs=pl.BlockSpec((None, tq, DV), lambda b, h, i: (b, i, h)),
        scratch_shapes=[pltpu.VMEM((tq, DV), F32)] * 6,
        compiler_params=_params(3),
        name="diff_attn_prompt",
    )(lam_vecs, ub3, ub3, ub3, bias, gout)


def _head_queries(q, h):
    qh = q[:, h * DV:(h + 1) * DV]
    lane = lax.broadcasted_iota(jnp.int32, qh.shape, 1)
    return jnp.concatenate([jnp.where(lane < DK, qh, 0.0), jnp.where(lane >= DK, qh, 0.0)], axis=0).astype(BF16)


def _attn_sample_kernel(pt_ref, q_ref, kn_ref, vn_ref, bpast_ref, bnew_ref, lam_ref, gout_ref, *rest, npg, lam0):
    k_refs, v_refs = rest[:npg], rest[npg:2 * npg]
    o_ref, m_scr, l_scr, acc_scr = rest[2 * npg:]
    c = pl.program_id(1)
    last = c == pl.num_programs(1) - 1
    tdec = q_ref.shape[0]
    rows = 2 * tdec

    @pl.when(c == 0)
    def _():
        m_scr[...] = jnp.full(m_scr.shape, NEG, F32)
        l_scr[...] = jnp.zeros(l_scr.shape, F32)
        acc_scr[...] = jnp.zeros(acc_scr.shape, F32)

    q = q_ref[...]
    qms = [_head_queries(q, h) for h in range(N_HEADS)]

    def update(s, pv_fn):
        m_old = m_scr[...]
        m_new = jnp.maximum(m_old, jnp.max(s, axis=-1, keepdims=True))
        alpha = jnp.exp2(m_old - m_new)
        p = jnp.exp2(s - m_new[:, 0:1])
        l_scr[...] = alpha * l_scr[...] + jnp.sum(p, axis=-1, keepdims=True)
        acc_scr[...] = alpha * acc_scr[...] + pv_fn(p.astype(BF16))
        m_scr[...] = m_new

    near = last.astype(F32)
    s_pages = []
    for i in range(npg):
        s_i = jnp.concatenate([_dot(qms[h], k_refs[i][h * DV:(h + 1) * DV, :].astype(BF16))
                               for h in range(N_HEADS)], axis=0)
        if i == npg - 1:
            s_i = s_i + near * bpast_ref[...]
        s_pages.append(s_i)

    def pv_pages(p):
        heads = []
        for h in range(N_HEADS):
            out = None
            for i in range(npg):
                v_h = v_refs[i][pl.ds(h, PAGE, stride=N_HEADS), :].astype(BF16)
                part = _dot(p[h * rows:(h + 1) * rows, i * PAGE:(i + 1) * PAGE], v_h)
                out = part if out is None else out + part
            heads.append(out)
        return jnp.concatenate(heads, axis=0)

    update(jnp.concatenate(s_pages, axis=1), pv_pages)

    @pl.when(last)
    def _():
        pad = jnp.zeros((PAGE - tdec, D), F32)
        kn = jnp.concatenate([kn_ref[...], pad], axis=0).astype(BF16)
        vn = jnp.concatenate([vn_ref[...], pad], axis=0).astype(BF16)
        s_new = jnp.concatenate([_dot_nt(qms[h], kn[:, h * DV:(h + 1) * DV]) for h in range(N_HEADS)], axis=0)
        update(s_new + bnew_ref[...],
               lambda p: jnp.concatenate([_dot(p[h * rows:(h + 1) * rows], vn[:, h * DV:(h + 1) * DV])
                                          for h in range(N_HEADS)], axis=0))
        lam = _lambda(lam_ref, lam0)
        o_all = acc_scr[...] / l_scr[...]
        for h in range(N_HEADS):
            o = o_all[h * rows:h * rows + tdec] - lam * o_all[h * rows + tdec:(h + 1) * rows]
            o_ref[:, h * DV:(h + 1) * DV] = _rms(o, gout_ref[...]) * (1.0 - lam0)


def _attn_sample(page_table, q3, kn3, vn3, cache_k4, cache_v4, layer, bpast, bnew, lam_vecs, gout, *, lam0):
    bs, tdec, _ = q3.shape
    n_pages = page_table.shape[1]
    npg = math.gcd(n_pages, 16)
    pt = page_table.reshape(-1)

    def page_spec(i):
        return pl.BlockSpec((None, None, D, PAGE),
                            lambda b, c, pt_ref: (layer, pt_ref[b * n_pages + c * npg + i], 0, 0))

    per_b = pl.BlockSpec((None, tdec, D), lambda b, c, pt_ref: (b, 0, 0))
    whole = lambda shape: pl.BlockSpec(shape, lambda b, c, pt_ref: (0,) * len(shape))
    stat = pltpu.VMEM((N_HEADS * 2 * tdec, DV), F32)
    return pl.pallas_call(
        functools.partial(_attn_sample_kernel, npg=npg, lam0=lam0),
        out_shape=jax.ShapeDtypeStruct((bs, tdec, D), F32),
        grid_spec=pltpu.PrefetchScalarGridSpec(
            num_scalar_prefetch=1,
            grid=(bs, n_pages // npg),
            in_specs=[per_b, per_b, per_b,
                      whole((N_HEADS * 2 * tdec, PAGE)), whole((N_HEADS * 2 * tdec, PAGE)),
                      whole((4, DK)), whole((1, DV))]
                     + [page_spec(i) for i in range(npg)] + [page_spec(i) for i in range(npg)],
            out_specs=per_b,
            scratch_shapes=[stat, stat, stat]),
        compiler_params=_params(2),
        name="diff_attn_sample",
    )(pt, q3, kn3, vn3, bpast, bnew, lam_vecs, gout, *([cache_k4] * npg), *([cache_v4] * npg))


def _cross_kernel(cq_ref, mk_ref, mv_ref, o_ref):
    rows = cq_ref.shape[0]
    for h in range(CA_HEADS):
        cols = slice(h * CA_D, (h + 1) * CA_D)
        q = cq_ref[:, cols]
        if rows % 16:
            q = jnp.concatenate([q.astype(F32), jnp.zeros((16 - rows % 16, CA_D), F32)], axis=0)
        mem = (lambda r: r[:, h, :]) if len(mk_ref.shape) == 3 else (lambda r: r[:, cols])
        s = _dot_nt(q.astype(BF16), mem(mk_ref).astype(BF16))
        p = jnp.exp(s - jnp.max(s, axis=-1, keepdims=True))
        o = _dot(p.astype(BF16), mem(mv_ref).astype(BF16)) / jnp.sum(p, axis=-1, keepdims=True)
        o_ref[:, cols] = o[0:rows].astype(o_ref.dtype)


def _cross(cq3, col_block, mem_k, mem_v, layer, out_dtype):
    b, t, _ = cq3.shape
    tm = min(512, t)
    if mem_k.ndim == 3:
        mem_spec = pl.BlockSpec((None, MEM_LEN, D), lambda b, i: (b, 0, 0))
    else:
        mem_spec = pl.BlockSpec((None, None, MEM_LEN, CA_HEADS, CA_D), lambda b, i: (layer, b, 0, 0, 0))
    return pl.pallas_call(
        _cross_kernel,
        out_shape=jax.ShapeDtypeStruct((b, t, D), out_dtype),
        grid=(b, t // tm),
        in_specs=[pl.BlockSpec((None, tm, D), lambda b, i: (b, i, col_block)), mem_spec, mem_spec],
        out_specs=pl.BlockSpec((None, tm, D), lambda b, i: (b, i, 0)),
        compiler_params=_params(2),
        name="mem_cross_attn",
    )(cq3, mem_k, mem_v)


def _ssd_kernel(xbc_ref, z_ref, dt_ref, h0_ref, c0_ref, cw_ref, cb_ref, alog_ref, dskip_ref, gout_ref,
                y_ref, hn_ref, cn_ref, buf, xc_scr, dt_scr, yb, h_scr, *, tv):
    L = SSM_L
    c = pl.program_id(1)
    head = 8

    @pl.when(c == 0)
    def _():
        buf[0:head, :] = jnp.zeros((head, CONV_CH), F32)
        buf[head - (CONV_K - 1):head, :] = c0_ref[...]
        h_scr[...] = h0_ref[...]
        if tv < L:
            buf[head + tv:, :] = jnp.zeros((L - tv, CONV_CH), F32)
            dt_scr[...] = jnp.zeros(dt_scr.shape, F32)

    buf[head:head + tv, :] = xbc_ref[...]
    cn_ref[...] = buf[head + tv - (CONV_K - 1):head + tv, :]

    cw = cw_ref[...]
    for j in range(CONV_CH // 512):
        cols = slice(j * 512, (j + 1) * 512)
        acc = cb_ref[:, cols] + cw[CONV_K - 1:CONV_K, cols] * buf[head:head + L, cols]
        for k in range(CONV_K - 1):
            acc = acc + cw[k:k + 1, cols] * buf[head - (CONV_K - 1) + k:head - (CONV_K - 1) + k + L, cols]
        xc_scr[:, cols] = acc * _sigmoid(acc)

    if tv == L:
        buf[0:head, :] = buf[L:L + head, :]
        dt = dt_ref[...]
    else:
        dt_scr[0:tv, :] = dt_ref[...]
        dt = dt_scr[...]

    a_neg = -jnp.exp(alog_ref[...])
    d_a = dt * a_neg
    row = lax.broadcasted_iota(jnp.int32, (L, L), 0)
    col = lax.broadcasted_iota(jnp.int32, (L, L), 1)
    causal = col <= row
    a_cum = _dot_split3(causal.astype(BF16), d_a)
    a_cum_t = a_cum.T
    dt_t = dt.T
    a_last = a_cum[L - 1:L, :]
    e_cum = jnp.exp(a_cum)
    to_end = jnp.exp(a_last - a_cum) * dt
    e_last = jnp.exp(a_last)
    lane = lax.broadcasted_iota(jnp.int32, (L, 128), 1)
    first = lane < SSM_P
    row_first = lax.broadcasted_iota(jnp.int32, (128, 128), 0) < SSM_P

    for g in range(SSM_G):
        b_g = xc_scr[:, SSM_INNER + g * SSM_N:SSM_INNER + (g + 1) * SSM_N].astype(BF16)
        c_g = xc_scr[:, SSM_INNER + SSM_G * SSM_N + g * SSM_N:SSM_INNER + SSM_G * SSM_N + (g + 1) * SSM_N].astype(BF16)
        cb = _dot_nt(c_g, b_g)
        for jj in (2 * g, 2 * g + 1):
            cols = slice(jj * 128, (jj + 1) * 128)
            x_pair = xc_scr[:, cols]
            x_bf = x_pair.astype(BF16)
            yw = []
            for hh in (2 * jj, 2 * jj + 1):
                seg = a_cum[:, hh:hh + 1] - a_cum_t[hh:hh + 1, :]
                w = cb * jnp.exp(jnp.where(causal, seg, NEG)) * dt_t[hh:hh + 1, :]
                yw.append(_dot(w.astype(BF16), x_bf))
            ha, hb = 2 * jj, 2 * jj + 1
            hp = h_scr[jj]
            y = jnp.where(first, yw[0], yw[1])
            y = y + _dot_nt(c_g, hp.astype(BF16)) * jnp.where(first, e_cum[:, ha:ha + 1], e_cum[:, hb:hb + 1])
            y = y + dskip_ref[:, cols] * x_pair
            te = jnp.where(first, to_end[:, ha:ha + 1], to_end[:, hb:hb + 1])
            upd = _dot_tn((x_pair * te).astype(BF16), b_g)
            h_scr[jj] = hp * jnp.where(row_first, e_last[:, ha:ha + 1], e_last[:, hb:hb + 1]) + upd
            yb[:, cols] = y[0:tv] * (z_ref[:, cols] * _sigmoid(z_ref[:, cols]))

    gw = SSM_INNER // SSM_G
    for g in range(SSM_G):
        cols = slice(g * gw, (g + 1) * gw)
        y_ref[:, cols] = _rms(yb[:, cols], gout_ref[:, cols]).astype(y_ref.dtype)
    hn_ref[...] = h_scr[...]


def _ssd(xbc_src, xbc_blk, z_src, z_blk, dt3, h0, c0, layer, cw, cb, alog, dskip, gout, out_dtype):
    b, t, _ = dt3.shape
    tv = min(SSM_L, t)
    nc = t // tv
    seq = lambda width, blk: pl.BlockSpec((None, tv, width), lambda b, c: (b, c, blk))
    per_b = lambda shape: pl.BlockSpec((None,) + shape, lambda b, c: (b,) + (0,) * len(shape))
    state = lambda shape: pl.BlockSpec((None, None) + shape, lambda b, c: (layer, b) + (0,) * len(shape))
    whole = lambda shape: pl.BlockSpec(shape, lambda b, c: (0,) * len(shape))
    return pl.pallas_call(
        functools.partial(_ssd_kernel, tv=tv),
        out_shape=(jax.ShapeDtypeStruct((b, t, SSM_INNER), out_dtype),
                   jax.ShapeDtypeStruct((b, N_PAIRS, 128, SSM_N), F32),
                   jax.ShapeDtypeStruct((b, CONV_K - 1, CONV_CH), F32)),
        grid=(b, nc),
        in_specs=[seq(CONV_CH, xbc_blk), seq(SSM_INNER, z_blk), seq(128, 0),
                  state((N_PAIRS, 128, SSM_N)), state((CONV_K - 1, CONV_CH)),
                  whole((CONV_K, CONV_CH)), whole((1, CONV_CH)), whole((1, 128)),
                  whole((1, SSM_INNER)), whole((1, SSM_INNER))],
        out_specs=(seq(SSM_INNER, 0), per_b((N_PAIRS, 128, SSM_N)), per_b((CONV_K - 1, CONV_CH))),
        scratch_shapes=[pltpu.VMEM((SSM_L + 8, CONV_CH), F32), pltpu.VMEM((SSM_L, CONV_CH), F32),
                        pltpu.VMEM((SSM_L, 128), F32), pltpu.VMEM((tv, SSM_INNER), F32),
                        pltpu.VMEM((N_PAIRS, 128, SSM_N), F32)],
        compiler_params=_params(2),
        name="conv_ssd",
    )(xbc_src, z_src, dt3, h0, c0, cw, cb, alog, dskip, gout)


def _merge_kernel(x_ref, gmix_ref, wg_ref, oa_ref, ys_ref, oc_ref, wa_ref, ws_ref, wc_ref, wo_ref, o_ref):
    x = x_ref[...]
    h = _rms(x, gmix_ref[...]).astype(BF16)
    branches = ((oa_ref, wa_ref), (ys_ref, ws_ref), (oc_ref, wc_ref))
    merged = None
    for b, (act_ref, w_ref) in enumerate(branches):
        gate = _sigmoid(_dot_nt(h, wg_ref[b * D:(b + 1) * D, :]))
        term = gate * _dot(act_ref[...].astype(BF16), w_ref[...])
        merged = term if merged is None else merged + term
    o_ref[...] = x + _dot(merged.astype(BF16), wo_ref[...])


def _merge(x, gmix, wg, oa, ys, oc, wa, ws, wc, wo):
    n = x.shape[0]
    tm = min(512, n)
    row = lambda width: pl.BlockSpec((tm, width), lambda i: (i, 0))
    whole = lambda shape: pl.BlockSpec(shape, lambda i: (0, 0), pipeline_mode=pl.Buffered(1))
    return pl.pallas_call(
        _merge_kernel,
        out_shape=jax.ShapeDtypeStruct((n, D), F32),
        grid=(n // tm,),
        in_specs=[row(D), whole((1, D)), whole((3 * D, D)), row(D), row(SSM_INNER), row(D),
                  whole((D, D)), whole((SSM_INNER, D)), whole((D, D)), whole((D, D))],
        out_specs=row(D),
        compiler_params=_params(1),
        name="branch_merge",
    )(x, gmix, wg, oa, ys, oc, wa, ws, wc, wo)


def _mlp_kernel(x_ref, g_ref, wu_ref, wd_ref, o_ref, h_scr, acc):
    f = pl.program_id(1)

    @pl.when(f == 0)
    def _():
        x = x_ref[...]
        h_scr[...] = _rms(x, g_ref[...]).astype(BF16)
        acc[...] = x

    a = jnp.maximum(_dot(h_scr[...], wu_ref[...]), 0.0)
    acc[...] += _dot((a * a).astype(BF16), wd_ref[...])

    @pl.when(f == pl.num_programs(1) - 1)
    def _():
        o_ref[...] = acc[...]


def _mlp(x, g, wu, wd):
    n = x.shape[0]
    tm = min(1024, n)
    tf = 1024
    return pl.pallas_call(
        _mlp_kernel,
        out_shape=jax.ShapeDtypeStruct((n, D), F32),
        grid=(n // tm, D_FF // tf),
        in_specs=[pl.BlockSpec((tm, D), lambda i, f: (i, 0)),
                  pl.BlockSpec((1, D), lambda i, f: (0, 0)),
                  pl.BlockSpec((D, tf), lambda i, f: (0, f)),
                  pl.BlockSpec((tf, D), lambda i, f: (f, 0))],
        out_specs=pl.BlockSpec((tm, D), lambda i, f: (i, 0)),
        scratch_shapes=[pltpu.VMEM((tm, D), BF16), pltpu.VMEM((tm, D), F32)],
        compiler_params=_params(2),
        name="relu2_mlp",
    )(x, g, wu, wd)


def _lambda_init(layer):
    return 0.8 - 0.6 * math.exp(-0.3 * layer)


def _layer_weights(l, p):
    w_in_t = jnp.swapaxes(p['w_in'], 1, 2)[l]
    cols = lambda off, width: w_in_t[off:off + width]
    w_main = jnp.concatenate([cols(OFF_Q, D), cols(OFF_CQ, D), cols(OFF_K, D), cols(OFF_V, D),
                              cols(OFF_XBC, CONV_CH), cols(OFF_Z, SSM_INNER)], axis=0).astype(BF16)
    w_gate = cols(OFF_G, 3 * D).astype(BF16)
    w_dt = jnp.pad(cols(OFF_DT, SSM_HEADS), ((0, 128 - SSM_HEADS), (0, 0))).astype(BF16)
    row = lambda v: v.reshape(1, -1).astype(F32)
    return dict(
        g_mix=row(p['g_mix'][l]), w_main=w_main, w_dt=w_dt, w_gate=w_gate,
        dt_bias=jnp.pad(row(p['dt_bias'][l]), ((0, 0), (0, 128 - SSM_HEADS))),
        gq=row(jnp.tile(p['g_da_q'][l], TN // DK)), gk=row(jnp.tile(p['g_da_k'][l], TN // DK)),
        gcq=row(p['g_ca_q'][l]),
        lam=jnp.stack([p['lambda_q1'][l], p['lambda_k1'][l], p['lambda_q2'][l], p['lambda_k2'][l]]).astype(F32),
        g_da_out=row(p['g_da_out'][l]),
        conv_w=p['conv_w'][l].astype(F32), conv_b=row(p['conv_b'][l]),
        a_log=jnp.pad(row(p['a_log'][l]), ((0, 0), (0, 128 - SSM_HEADS))),
        d_skip=row(jnp.repeat(p['d_skip'][l], SSM_P)), g_ssm_out=row(p['g_ssm_out'][l]),
        w_br_attn=p['w_br_attn'][l].astype(BF16), w_br_ssm=p['w_br_ssm'][l].astype(BF16),
        w_br_cross=p['w_br_cross'][l].astype(BF16), w_out=p['w_out'][l].astype(BF16),
        g_mlp=row(p['g_mlp'][l]), w_up=p['w_up'][l].astype(BF16), w_down=p['w_down'][l].astype(BF16),
        g_mem=row(p['g_mem'][l]), w_mem_kv=p['w_mem_kv'][l].astype(BF16), g_ca_k=row(p['g_ca_k'][l]),
    )


def _seg64_matrix():
    i = jnp.arange(SEG_W) // DK
    return jnp.where(i[:, None] == i[None, :], 1.0 / DK, 0.0).astype(BF16)


def _layer(x, bsz, w, s64, attn_fn, mem_k, mem_v, h0, c0, state_layer, act_dtype):
    n = x.shape[0]
    t = n // bsz
    ub, k32, v32, uf, dt = _proj(x, w['g_mix'], w['w_main'], w['w_dt'], w['dt_bias'], w['gq'], w['gk'], w['gcq'], s64)
    ub3 = ub.reshape(bsz, t, -1)
    o_a = attn_fn(ub3, k32, v32)
    uf3 = uf.reshape(bsz, t, -1)
    y_n, h_new, c_new = _ssd(uf3, 0, uf3, ((PB_Z - PB_XBC) * TN) // SSM_INNER, dt.reshape(bsz, t, 128), h0, c0,
                             state_layer, w['conv_w'], w['conv_b'], w['a_log'], w['d_skip'], w['g_ssm_out'], act_dtype)
    if act_dtype == BF16:
        cq3, cq_blk = ub3, (PB_CQ * TN) // D
    else:
        cq3, cq_blk = ub3[:, :, PB_CQ * TN:PB_K * TN].astype(F32), 0
    o_c = _cross(cq3, cq_blk, mem_k, mem_v, state_layer, act_dtype)
    x = _merge(x, w['g_mix'], w['w_gate'], o_a.reshape(n, D), y_n.reshape(n, SSM_INNER), o_c.reshape(n, D),
               w['w_br_attn'], w['w_br_ssm'], w['w_br_cross'], w['w_out'])
    x = _mlp(x, w['g_mlp'], w['w_up'], w['w_down'])
    return x, k32, v32, h_new, c_new


def kernel(x_prompt, x_sample, cache_k, cache_v, cache_mem_k, cache_mem_v, state_ssm, state_conv, page_table, mem_prompt, rel_bias_table, g_mix, w_in, g_da_q, g_da_k, lambda_q1, lambda_k1, lambda_q2, lambda_k2, g_da_out, w_br_attn, conv_w, conv_b, dt_bias, a_log, d_skip, g_ssm_out, w_br_ssm, g_mem, w_mem_kv, g_ca_q, g_ca_k, w_br_cross, w_out, g_mlp, w_up, w_down):
    p = dict(g_mix=g_mix, w_in=w_in, g_da_q=g_da_q, g_da_k=g_da_k, lambda_q1=lambda_q1, lambda_k1=lambda_k1,
             lambda_q2=lambda_q2, lambda_k2=lambda_k2, g_da_out=g_da_out, w_br_attn=w_br_attn, conv_w=conv_w,
             conv_b=conv_b, dt_bias=dt_bias, a_log=a_log, d_skip=d_skip, g_ssm_out=g_ssm_out, w_br_ssm=w_br_ssm,
             g_mem=g_mem, w_mem_kv=w_mem_kv, g_ca_q=g_ca_q, g_ca_k=g_ca_k, w_br_cross=w_br_cross, w_out=w_out,
             g_mlp=g_mlp, w_up=w_up, w_down=w_down)
    depth = w_in.shape[0]
    bp, tp, _ = x_prompt.shape
    bs, ts, _ = x_sample.shape
    n_phys = cache_k.shape[1]
    past_len = page_table.shape[1] * PAGE
    tq = min(512, tp)
    table = rel_bias_table.astype(F32)
    s64 = _seg64_matrix()

    bias_p = _bias_tiles(table, (0, tq), (tq, tq))
    bias_s = _bias_tiles(table, (PAGE, 0), (ts, PAGE))
    bias_past = bias_s[:, 0].reshape(N_HEADS * 2 * ts, PAGE)
    bias_new = bias_s[:, 1].reshape(N_HEADS * 2 * ts, PAGE)

    cache_k4 = jnp.transpose(cache_k, (0, 1, 3, 4, 2)).reshape(depth, n_phys, D, PAGE)
    cache_v4 = cache_v.reshape(depth, n_phys, PAGE * N_HEADS, DV)
    mem_flat = mem_prompt.reshape(bp * MEM_LEN, D)
    zero_h = jnp.zeros((1, bp, N_PAIRS, 128, SSM_N), F32)
    zero_c = jnp.zeros((1, bp, CONV_K - 1, CONV_CH), F32)
    state_ssm5 = state_ssm.reshape(depth, bs, N_PAIRS, 128, SSM_N)

    xp = x_prompt.reshape(bp * tp, D)
    xs = x_sample.reshape(bs * ts, D)
    outs = {k: [] for k in ('kp', 'vp', 'mk', 'mv', 'hp', 'cp', 'ks', 'vs', 'hs', 'cs')}
    for l in range(depth):
        w = _layer_weights(l, p)
        lam0 = _lambda_init(l)

        mk, mv = _memkv(mem_flat, w['g_mem'], w['w_mem_kv'], w['g_ca_k'])
        attn_p = lambda ub3, k32, v32: _attn_prompt(ub3, w['lam'], bias_p, w['g_da_out'], lam0=lam0, tq=tq)
        xp, k, v, hn, cn = _layer(xp, bp, w, s64, attn_p, mk.reshape(bp, MEM_LEN, D), mv.reshape(bp, MEM_LEN, D),
                                  zero_h, zero_c, 0, BF16)
        for key, val in zip(('kp', 'vp', 'mk', 'mv', 'hp', 'cp'), (k, v, mk, mv, hn, cn)):
            outs[key].append(val)

        def attn_s(ub3, k32, v32):
            q3 = ub3[:, :, 0:D].astype(F32)
            return _attn_sample(page_table, q3, k32.reshape(bs, ts, D), v32.reshape(bs, ts, D), cache_k4, cache_v4,
                                l, bias_past, bias_new, w['lam'], w['g_da_out'], lam0=lam0)

        xs, k, v, hn, cn = _layer(xs, bs, w, s64, attn_s, cache_mem_k, cache_mem_v, state_ssm5, state_conv, l, F32)
        for key, val in zip(('ks', 'vs', 'hs', 'cs'), (k, v, hn, cn)):
            outs[key].append(val)

    st = lambda key, shape: jnp.stack(outs[key]).reshape((depth,) + shape)
    return (xp.reshape(bp, tp, D), xs.reshape(bs, ts, D),
            st('kp', (bp, tp, N_MAPS, DK)), st('vp', (bp, tp, N_HEADS, DV)),
            st('mk', (bp, MEM_LEN, CA_HEADS, CA_D)), st('mv', (bp, MEM_LEN, CA_HEADS, CA_D)),
            st('hp', (bp, SSM_HEADS, SSM_P, SSM_N)), st('cp', (bp, CONV_K - 1, CONV_CH)),
            st('ks', (bs, ts, N_MAPS, DK)), st('vs', (bs, ts, N_HEADS, DV)),
            st('hs', (bs, SSM_HEADS, SSM_P, SSM_N)), st('cs', (bs, CONV_K - 1, CONV_CH)))
```

```python
import functools
import math

import jax
import jax.numpy as jnp
from jax import lax
from jax.experimental import pallas as pl
from jax.experimental.pallas import tpu as pltpu

F32, BF16 = jnp.float32, jnp.bfloat16
EPS = 1e-5
NEG = -1e30
LOG2E = math.log2(math.e)

D = 1024
N_MAPS, DK = 16, 64
N_HEADS, DV = 8, 128
SSM_INNER, SSM_P, SSM_N, SSM_G = 2048, 64, 128, 8
SSM_HEADS = SSM_INNER // SSM_P
N_PAIRS = SSM_HEADS // 2
CONV_K, CONV_CH = 4, 4096
MEM_LEN, CA_HEADS, CA_D = 256, 4, 256
D_FF = 4096
PAGE = 128
REL_BUCKETS, REL_MAX_DIST = 32, 128
SSM_L = 128

OFF_Q, OFF_K, OFF_V, OFF_Z, OFF_XBC = 0, 1024, 2048, 3072, 5120
OFF_DT = OFF_XBC + CONV_CH
OFF_CQ = OFF_DT + SSM_HEADS
OFF_G = OFF_CQ + D

TN = 1024
SEG_W = 256
PB_Q, PB_CQ, PB_K, PB_V, PB_XBC, PB_Z, PB_END = 0, 1, 2, 3, 4, 8, 10
VMEM_LIMIT = 48 << 20


def _params(n_axes, vmem=VMEM_LIMIT):
    return pltpu.CompilerParams(dimension_semantics=("arbitrary",) * n_axes, vmem_limit_bytes=vmem)


def _rms(x, gain):
    ms = jnp.mean(x * x, axis=-1, keepdims=True)
    return x * lax.rsqrt(ms + EPS) * gain


def _sigmoid(x):
    return 1.0 / (1.0 + jnp.exp(-x))


def _softplus(x):
    return jnp.maximum(x, 0.0) + jnp.log1p(jnp.exp(-jnp.abs(x)))


def _dot(a, b):
    return jnp.dot(a, b, preferred_element_type=F32)


def _dot_nt(a, b):
    return lax.dot_general(a, b, (((1,), (1,)), ((), ())), preferred_element_type=F32)


def _dot_tn(a, b):
    return lax.dot_general(a, b, (((0,), (0,)), ((), ())), preferred_element_type=F32)


def _dot_split3(a01, x):
    hi = x.astype(BF16)
    r = x - hi.astype(F32)
    mid = r.astype(BF16)
    lo = (r - mid.astype(F32)).astype(BF16)
    return _dot(a01, hi) + _dot(a01, mid) + _dot(a01, lo)


def _bucket_thresholds():
    max_exact = REL_BUCKETS // 2
    ratio = REL_MAX_DIST / max_exact
    return [math.ceil(max_exact * ratio ** (k / (REL_BUCKETS - max_exact))) for k in range(1, REL_BUCKETS - max_exact)]


def _bias_kernel(tab_ref, o_ref, *, off, shape):
    m = pl.program_id(0)
    r = lax.broadcasted_iota(jnp.int32, shape, 0)
    c = lax.broadcasted_iota(jnp.int32, shape, 1)
    n = off + r - c
    nn = jnp.maximum(n, 0)
    max_exact = REL_BUCKETS // 2
    large = jnp.full(shape, max_exact, jnp.int32)
    for t in _bucket_thresholds():
        large = large + (nn >= t).astype(jnp.int32)
    bucket = jnp.where(nn < max_exact, nn, large)
    val = jnp.zeros(shape, F32)
    for b in range(REL_BUCKETS):
        val = jnp.where(bucket == b, tab_ref[b, m], val)
    val = (val - tab_ref[REL_BUCKETS - 1, m]) * LOG2E
    o_ref[0] = jnp.where(n >= 0, val, NEG)


def _bias_tiles(table, off, shape):
    return pl.pallas_call(
        functools.partial(_bias_kernel, off=off, shape=shape),
        out_shape=jax.ShapeDtypeStruct((N_MAPS,) + shape, F32),
        grid=(N_MAPS,),
        in_specs=[pl.BlockSpec(memory_space=pltpu.SMEM)],
        out_specs=pl.BlockSpec((1,) + shape, lambda m: (m, 0, 0)),
        compiler_params=_params(1),
        name="rel_bias_tiles",
    )(table)


def _proj_attn_kernel(x_ref, g_ref, w_ref, wdt_ref, dtb_ref, gq_ref, gk_ref, gcq_ref, s64_ref,
                      ub_ref, k32_ref, v32_ref, dt_ref):
    h = _rms(x_ref[...], g_ref[...]).astype(BF16)
    dt_ref[...] = _softplus(_dot_nt(h, wdt_ref[...]) + dtb_ref[...])

    def proj(blk):
        return _dot_nt(h, w_ref[blk * TN:(blk + 1) * TN, :])

    def seg64_norm(u, gain):
        sq = (u * u).astype(BF16)
        ms = jnp.concatenate([_dot(sq[:, c * SEG_W:(c + 1) * SEG_W], s64_ref[...]) for c in range(TN // SEG_W)],
                             axis=1)
        return u * lax.rsqrt(ms + EPS) * gain

    def out(blk):
        return slice(blk * TN, (blk + 1) * TN)

    ub_ref[:, out(PB_Q)] = (seg64_norm(proj(PB_Q), gq_ref[...]) * (DK ** -0.5 * LOG2E)).astype(BF16)
    u = proj(PB_CQ)
    for s in range(TN // CA_D):
        seg = u[:, s * CA_D:(s + 1) * CA_D]
        ub_ref[:, PB_CQ * TN + s * CA_D:PB_CQ * TN + (s + 1) * CA_D] = (
            _rms(seg, gcq_ref[...]) * (CA_D ** -0.5)).astype(BF16)
    kn = seg64_norm(proj(PB_K), gk_ref[...])
    k32_ref[...] = kn
    ub_ref[:, out(PB_K)] = kn.astype(BF16)
    u = proj(PB_V)
    v32_ref[...] = u
    ub_ref[:, out(PB_V)] = u.astype(BF16)


def _proj_ssm_kernel(x_ref, g_ref, w_ref, uf_ref):
    h = _rms(x_ref[...], g_ref[...]).astype(BF16)
    for blk in range(uf_ref.shape[1] // TN):
        uf_ref[:, blk * TN:(blk + 1) * TN] = _dot_nt(h, w_ref[blk * TN:(blk + 1) * TN, :])


def _proj(x, g, w_attn, w_ssm, wdt, dtb, gq, gk, gcq, s64):
    n = x.shape[0]
    tm = min(512, n)
    row = lambda width: pl.BlockSpec((tm, width), lambda i: (i, 0))
    whole = lambda shape: pl.BlockSpec(shape, lambda i: (0, 0), pipeline_mode=pl.Buffered(1))
    ub, k32, v32, dt = pl.pallas_call(
        _proj_attn_kernel,
        out_shape=(jax.ShapeDtypeStruct((n, PB_XBC * TN), BF16),
                   jax.ShapeDtypeStruct((n, D), F32),
                   jax.ShapeDtypeStruct((n, D), F32),
                   jax.ShapeDtypeStruct((n, 128), F32)),
        grid=(n // tm,),
        in_specs=[row(D), whole((1, D)), whole((PB_XBC * TN, D)), whole((128, D)), whole((1, 128)),
                  whole((1, TN)), whole((1, TN)), whole((1, CA_D)), whole((SEG_W, SEG_W))],
        out_specs=(row(PB_XBC * TN), row(D), row(D), row(128)),
        compiler_params=_params(1),
        name="in_proj_attn",
    )(x, g, w_attn, wdt, dtb, gq, gk, gcq, s64)
    ssm_w = (PB_END - PB_XBC) * TN
    uf = pl.pallas_call(
        _proj_ssm_kernel,
        out_shape=jax.ShapeDtypeStruct((n, ssm_w), F32),
        grid=(n // tm,),
        in_specs=[row(D), whole((1, D)), whole((ssm_w, D))],
        out_specs=row(ssm_w),
        compiler_params=_params(1, 56 << 20),
        name="in_proj_ssm",
    )(x, g, w_ssm)
    return ub, k32, v32, uf, dt


def _memkv_kernel(m_ref, g_ref, w_ref, gk_ref, mk_ref, mv_ref, h_scr):
    j = pl.program_id(1)

    @pl.when(j == 0)
    def _():
        h_scr[...] = _rms(m_ref[...], g_ref[...]).astype(BF16)

    u = _dot(h_scr[...], w_ref[...])

    @pl.when(j == 0)
    def _():
        for s in range(CA_HEADS):
            mk_ref[:, s * CA_D:(s + 1) * CA_D] = _rms(u[:, s * CA_D:(s + 1) * CA_D], gk_ref[...])

    @pl.when(j == 1)
    def _():
        mv_ref[...] = u


def _memkv(mem, g, w, gk):
    n = mem.shape[0]
    tm = min(512, n)
    return pl.pallas_call(
        _memkv_kernel,
        out_shape=(jax.ShapeDtypeStruct((n, D), F32), jax.ShapeDtypeStruct((n, D), F32)),
        grid=(n // tm, 2),
        in_specs=[pl.BlockSpec((tm, D), lambda i, j: (i, 0)),
                  pl.BlockSpec((1, D), lambda i, j: (0, 0)),
                  pl.BlockSpec((D, D), lambda i, j: (0, j)),
                  pl.BlockSpec((1, CA_D), lambda i, j: (0, 0))],
        out_specs=(pl.BlockSpec((tm, D), lambda i, j: (i, 0)), pl.BlockSpec((tm, D), lambda i, j: (i, 0))),
        scratch_shapes=[pltpu.VMEM((tm, D), BF16)],
        compiler_params=_params(2),
        name="mem_kv",
    )(mem, g, w, gk)


def _lambda(lam_ref, lam0):
    lv = lam_ref[...]
    s1 = jnp.sum(lv[0:1] * lv[1:2], axis=-1, keepdims=True)
    s2 = jnp.sum(lv[2:3] * lv[3:4], axis=-1, keepdims=True)
    return jnp.exp(s1) - jnp.exp(s2) + lam0


def _softmax_step(state, s, v_bf16):
    m, l, acc = state
    m_new = jnp.maximum(m, jnp.max(s, axis=-1, keepdims=True))
    alpha = jnp.exp2(m - m_new)
    p = jnp.exp2(s - jnp.tile(m_new, (1, s.shape[1] // m_new.shape[1])))
    l = alpha * l + jnp.sum(p, axis=-1, keepdims=True)
    acc = alpha * acc + _dot(p.astype(BF16), v_bf16)
    return m_new, l, acc


def _attn_prompt_kernel(lam_ref, q_ref, k_ref, v_ref, bdiag_ref, bnear_ref, gout_ref, o_ref, *stat_refs, tq, lam0):
    stats = (stat_refs[0:3], stat_refs[3:6])
    qi = pl.program_id(2)
    q = q_ref[...]
    lane = lax.broadcasted_iota(jnp.int32, q.shape, 1)
    qm = (jnp.where(lane < DK, q, jnp.zeros_like(q)), jnp.where(lane >= DK, q, jnp.zeros_like(q)))
    for m_ref, l_ref, acc_ref in stats:
        m_ref[...] = jnp.full(m_ref.shape, NEG, F32)
        l_ref[...] = jnp.zeros(l_ref.shape, F32)
        acc_ref[...] = jnp.zeros(acc_ref.shape, F32)

    def update(rows, key_start, key_len, bias_fn):
        k = k_ref[pl.ds(key_start, key_len), :]
        v = v_ref[pl.ds(key_start, key_len), :]
        for mi in range(2):
            s = _dot_nt(qm[mi][rows], k)
            if bias_fn is not None:
                s = s + bias_fn(mi)
            new = _softmax_step(tuple(r[rows, :] for r in stats[mi]), s, v)
            for r, val in zip(stats[mi], new):
                r[rows, :] = val

    def loop(lo, hi, bias_fn):
        def body(kb, carry):
            update(slice(0, tq), pl.multiple_of(kb * tq, tq), tq, bias_fn)
            return carry
        lax.fori_loop(lo, hi, body, 0)

    loop(0, qi - 1, None)
    loop(jnp.maximum(qi - 1, 0), qi, lambda mi: bnear_ref[mi])
    update(slice(0, tq), pl.multiple_of(qi * tq, tq), tq, lambda mi: bdiag_ref[mi])

    lam = _lambda(lam_ref, lam0)
    (_, l0, a0), (_, l1, a1) = stats
    o = a0[...] / l0[...] - lam * (a1[...] / l1[...])
    o_ref[...] = (_rms(o, gout_ref[...]) * (1.0 - lam0)).astype(o_ref.dtype)


def _attn_prompt(ub3, lam_vecs, bias_diag, bias_near, gout, *, lam0, tq):
    b, t, _ = ub3.shape
    kcol, vcol = (PB_K * TN) // DV, (PB_V * TN) // DV
    bias_spec = pl.BlockSpec((2, tq, tq), lambda h, b, i: (h, 0, 0))
    return pl.pallas_call(
        functools.partial(_attn_prompt_kernel, tq=tq, lam0=lam0),
        out_shape=jax.ShapeDtypeStruct((b, t, D), BF16),
        grid=(N_HEADS, b, t // tq),
        in_specs=[pl.BlockSpec((4, DK), lambda h, b, i: (0, 0)),
                  pl.BlockSpec((None, tq, DV), lambda h, b, i: (b, i, h)),
                  pl.BlockSpec((None, t, DV), lambda h, b, i: (b, 0, kcol + h)),
                  pl.BlockSpec((None, t, DV), lambda h, b, i: (b, 0, vcol + h)),
                  bias_spec, bias_spec,
                  pl.BlockSpec((1, DV), lambda h, b, i: (0, 0))],
        out_specs=pl.BlockSpec((None, tq, DV), lambda h, b, i: (b, i, h)),
        scratch_shapes=[pltpu.VMEM((tq, DV), F32)] * 6,
        compiler_params=_params(3),
        name="diff_attn_prompt",
    )(lam_vecs, ub3, ub3, ub3, bias_diag, bias_near, gout)


def _head_queries(q, h):
    qh = q[:, h * DV:(h + 1) * DV]
    lane = lax.broadcasted_iota(jnp.int32, qh.shape, 1)
    return jnp.concatenate([jnp.where(lane < DK, qh, 0.0), jnp.where(lane >= DK, qh, 0.0)], axis=0).astype(BF16)


def _attn_sample_kernel(pt_ref, q_ref, kn_ref, vn_ref, bpast_ref, bnew_ref, lam_ref, gout_ref, *rest, npg, lam0):
    k_refs, v_refs = rest[:npg], rest[npg:2 * npg]
    o_ref, m_scr, l_scr, acc_scr = rest[2 * npg:]
    c = pl.program_id(1)
    last = c == pl.num_programs(1) - 1
    tdec = q_ref.shape[0]
    rows = 2 * tdec

    @pl.when(c == 0)
    def _():
        m_scr[...] = jnp.full(m_scr.shape, NEG, F32)
        l_scr[...] = jnp.zeros(l_scr.shape, F32)
        acc_scr[...] = jnp.zeros(acc_scr.shape, F32)

    q = q_ref[...]
    qms = [_head_queries(q, h) for h in range(N_HEADS)]

    def update(s, pv_fn):
        m_old = m_scr[...]
        m_new = jnp.maximum(m_old, jnp.max(s, axis=-1, keepdims=True))
        alpha = jnp.exp2(m_old - m_new)
        p = jnp.exp2(s - m_new[:, 0:1])
        l_scr[...] = alpha * l_scr[...] + jnp.sum(p, axis=-1, keepdims=True)
        acc_scr[...] = alpha * acc_scr[...] + pv_fn(p.astype(BF16))
        m_scr[...] = m_new

    near = last.astype(F32)
    s_pages = []
    for i in range(npg):
        s_i = jnp.concatenate([_dot(qms[h], k_refs[i][h * DV:(h + 1) * DV, :].astype(BF16))
                               for h in range(N_HEADS)], axis=0)
        if i == npg - 1:
            s_i = s_i + near * bpast_ref[...]
        s_pages.append(s_i)

    def pv_pages(p):
        heads = []
        for h in range(N_HEADS):
            out = None
            for i in range(npg):
                v_h = v_refs[i][pl.ds(h, PAGE, stride=N_HEADS), :].astype(BF16)
                part = _dot(p[h * rows:(h + 1) * rows, i * PAGE:(i + 1) * PAGE], v_h)
                out = part if out is None else out + part
            heads.append(out)
        return jnp.concatenate(heads, axis=0)

    update(jnp.concatenate(s_pages, axis=1), pv_pages)

    @pl.when(last)
    def _():
        pad = jnp.zeros((PAGE - tdec, D), F32)
        kn = jnp.concatenate([kn_ref[...], pad], axis=0).astype(BF16)
        vn = jnp.concatenate([vn_ref[...], pad], axis=0).astype(BF16)
        s_new = jnp.concatenate([_dot_nt(qms[h], kn[:, h * DV:(h + 1) * DV]) for h in range(N_HEADS)], axis=0)
        update(s_new + bnew_ref[...],
               lambda p: jnp.concatenate([_dot(p[h * rows:(h + 1) * rows], vn[:, h * DV:(h + 1) * DV])
                                          for h in range(N_HEADS)], axis=0))
        lam = _lambda(lam_ref, lam0)
        o_all = acc_scr[...] / l_scr[...]
        for h in range(N_HEADS):
            o = o_all[h * rows:h * rows + tdec] - lam * o_all[h * rows + tdec:(h + 1) * rows]
            o_ref[:, h * DV:(h + 1) * DV] = _rms(o, gout_ref[...]) * (1.0 - lam0)


def _attn_sample(page_table, q3, kn3, vn3, cache_k4, cache_v4, layer, bpast, bnew, lam_vecs, gout, *, lam0):
    bs, tdec, _ = q3.shape
    n_pages = page_table.shape[1]
    npg = math.gcd(n_pages, 16)
    pt = page_table.reshape(-1)

    def page_spec(i):
        return pl.BlockSpec((None, None, D, PAGE),
                            lambda b, c, pt_ref: (layer, pt_ref[b * n_pages + c * npg + i], 0, 0))

    per_b = pl.BlockSpec((None, tdec, D), lambda b, c, pt_ref: (b, 0, 0))
    whole = lambda shape: pl.BlockSpec(shape, lambda b, c, pt_ref: (0,) * len(shape))
    stat = pltpu.VMEM((N_HEADS * 2 * tdec, DV), F32)
    return pl.pallas_call(
        functools.partial(_attn_sample_kernel, npg=npg, lam0=lam0),
        out_shape=jax.ShapeDtypeStruct((bs, tdec, D), F32),
        grid_spec=pltpu.PrefetchScalarGridSpec(
            num_scalar_prefetch=1,
            grid=(bs, n_pages // npg),
            in_specs=[per_b, per_b, per_b,
                      whole((N_HEADS * 2 * tdec, PAGE)), whole((N_HEADS * 2 * tdec, PAGE)),
                      whole((4, DK)), whole((1, DV))]
                     + [page_spec(i) for i in range(npg)] + [page_spec(i) for i in range(npg)],
            out_specs=per_b,
            scratch_shapes=[stat, stat, stat]),
        compiler_params=_params(2),
        name="diff_attn_sample",
    )(pt, q3, kn3, vn3, bpast, bnew, lam_vecs, gout, *([cache_k4] * npg), *([cache_v4] * npg))


def _cross_kernel(cq_ref, mk_ref, mv_ref, o_ref):
    rows = cq_ref.shape[0]
    for h in range(CA_HEADS):
        cols = slice(h * CA_D, (h + 1) * CA_D)
        q = cq_ref[:, cols]
        if rows % 16:
            q = jnp.concatenate([q.astype(F32), jnp.zeros((16 - rows % 16, CA_D), F32)], axis=0)
        s = _dot_nt(q.astype(BF16), mk_ref[:, cols].astype(BF16))
        p = jnp.exp(s - jnp.max(s, axis=-1, keepdims=True))
        o = _dot(p.astype(BF16), mv_ref[:, cols].astype(BF16)) / jnp.sum(p, axis=-1, keepdims=True)
        o_ref[:, cols] = o[0:rows].astype(o_ref.dtype)


def _cross(cq3, col_block, mem_k, mem_v, out_dtype):
    b, t, _ = cq3.shape
    tm = min(512, t)
    mem_spec = pl.BlockSpec((None, MEM_LEN, D), lambda b, i: (b, 0, 0))
    return pl.pallas_call(
        _cross_kernel,
        out_shape=jax.ShapeDtypeStruct((b, t, D), out_dtype),
        grid=(b, t // tm),
        in_specs=[pl.BlockSpec((None, tm, D), lambda b, i: (b, i, col_block)), mem_spec, mem_spec],
        out_specs=pl.BlockSpec((None, tm, D), lambda b, i: (b, i, 0)),
        compiler_params=_params(2),
        name="mem_cross_attn",
    )(cq3, mem_k, mem_v)


def _ssd_kernel(xbc_ref, z_ref, dt_ref, h0_ref, c0_ref, cw_ref, cb_ref, alog_ref, dskip_ref, gout_ref,
                y_ref, hn_ref, cn_ref, buf, xc_scr, dt_scr, yb, h_scr, *, tv):
    L = SSM_L
    c = pl.program_id(1)
    head = 8

    @pl.when(c == 0)
    def _():
        buf[0:head, :] = jnp.zeros((head, CONV_CH), F32)
        buf[head - (CONV_K - 1):head, :] = c0_ref[...]
        h_scr[...] = h0_ref[...]
        if tv < L:
            buf[head + tv:, :] = jnp.zeros((L - tv, CONV_CH), F32)
            dt_scr[...] = jnp.zeros(dt_scr.shape, F32)

    buf[head:head + tv, :] = xbc_ref[...]
    cn_ref[...] = buf[head + tv - (CONV_K - 1):head + tv, :]

    cw = cw_ref[...]
    for j in range(CONV_CH // 512):
        cols = slice(j * 512, (j + 1) * 512)
        acc = cb_ref[:, cols] + cw[CONV_K - 1:CONV_K, cols] * buf[head:head + L, cols]
        for k in range(CONV_K - 1):
            acc = acc + cw[k:k + 1, cols] * buf[head - (CONV_K - 1) + k:head - (CONV_K - 1) + k + L, cols]
        xc_scr[:, cols] = acc * _sigmoid(acc)

    if tv == L:
        buf[0:head, :] = buf[L:L + head, :]
        dt = dt_ref[...]
    else:
        dt_scr[0:tv, :] = dt_ref[...]
        dt = dt_scr[...]

    a_neg = -jnp.exp(alog_ref[...])
    d_a = dt * a_neg
    row = lax.broadcasted_iota(jnp.int32, (L, L), 0)
    col = lax.broadcasted_iota(jnp.int32, (L, L), 1)
    causal = col <= row
    a_cum = _dot_split3(causal.astype(BF16), d_a)
    a_cum_t = a_cum.T
    dt_t = dt.T
    a_last = a_cum[L - 1:L, :]
    e_cum = jnp.exp(a_cum)
    to_end = jnp.exp(a_last - a_cum) * dt
    e_last = jnp.exp(a_last)
    lane = lax.broadcasted_iota(jnp.int32, (L, 128), 1)
    first = lane < SSM_P
    row_first = lax.broadcasted_iota(jnp.int32, (128, 128), 0) < SSM_P

    for g in range(SSM_G):
        b_g = xc_scr[:, SSM_INNER + g * SSM_N:SSM_INNER + (g + 1) * SSM_N].astype(BF16)
        c_g = xc_scr[:, SSM_INNER + SSM_G * SSM_N + g * SSM_N:SSM_INNER + SSM_G * SSM_N + (g + 1) * SSM_N].astype(BF16)
        cb = _dot_nt(c_g, b_g)
        for jj in (2 * g, 2 * g + 1):
            cols = slice(jj * 128, (jj + 1) * 128)
            x_pair = xc_scr[:, cols]
            x_bf = x_pair.astype(BF16)
            yw = []
            for hh in (2 * jj, 2 * jj + 1):
                seg = a_cum[:, hh:hh + 1] - a_cum_t[hh:hh + 1, :]
                w = cb * jnp.exp(jnp.where(causal, seg, NEG)) * dt_t[hh:hh + 1, :]
                yw.append(_dot(w.astype(BF16), x_bf))
            ha, hb = 2 * jj, 2 * jj + 1
            hp = h_scr[jj]
            y = jnp.where(first, yw[0], yw[1])
            y = y + _dot_nt(c_g, hp.astype(BF16)) * jnp.where(first, e_cum[:, ha:ha + 1], e_cum[:, hb:hb + 1])
            y = y + dskip_ref[:, cols] * x_pair
            te = jnp.where(first, to_end[:, ha:ha + 1], to_end[:, hb:hb + 1])
            upd = _dot_tn((x_pair * te).astype(BF16), b_g)
            h_scr[jj] = hp * jnp.where(row_first, e_last[:, ha:ha + 1], e_last[:, hb:hb + 1]) + upd
            yb[:, cols] = y[0:tv] * (z_ref[:, cols] * _sigmoid(z_ref[:, cols]))

    gw = SSM_INNER // SSM_G
    for g in range(SSM_G):
        cols = slice(g * gw, (g + 1) * gw)
        y_ref[:, cols] = _rms(yb[:, cols], gout_ref[:, cols]).astype(y_ref.dtype)
    hn_ref[...] = h_scr[...]


def _ssd(xbc_src, xbc_blk, z_src, z_blk, dt3, h0, c0, layer, cw, cb, alog, dskip, gout, out_dtype):
    b, t, _ = dt3.shape
    tv = min(SSM_L, t)
    nc = t // tv
    seq = lambda width, blk: pl.BlockSpec((None, tv, width), lambda b, c: (b, c, blk))
    per_b = lambda shape: pl.BlockSpec((None,) + shape, lambda b, c: (b,) + (0,) * len(shape))
    state = lambda shape: pl.BlockSpec((None, None) + shape, lambda b, c: (layer, b) + (0,) * len(shape))
    whole = lambda shape: pl.BlockSpec(shape, lambda b, c: (0,) * len(shape))
    return pl.pallas_call(
        functools.partial(_ssd_kernel, tv=tv),
        out_shape=(jax.ShapeDtypeStruct((b, t, SSM_INNER), out_dtype),
                   jax.ShapeDtypeStruct((b, N_PAIRS, 128, SSM_N), F32),
                   jax.ShapeDtypeStruct((b, CONV_K - 1, CONV_CH), F32)),
        grid=(b, nc),
        in_specs=[seq(CONV_CH, xbc_blk), seq(SSM_INNER, z_blk), seq(128, 0),
                  state((N_PAIRS, 128, SSM_N)), state((CONV_K - 1, CONV_CH)),
                  whole((CONV_K, CONV_CH)), whole((1, CONV_CH)), whole((1, 128)),
                  whole((1, SSM_INNER)), whole((1, SSM_INNER))],
        out_specs=(seq(SSM_INNER, 0), per_b((N_PAIRS, 128, SSM_N)), per_b((CONV_K - 1, CONV_CH))),
        scratch_shapes=[pltpu.VMEM((SSM_L + 8, CONV_CH), F32), pltpu.VMEM((SSM_L, CONV_CH), F32),
                        pltpu.VMEM((SSM_L, 128), F32), pltpu.VMEM((tv, SSM_INNER), F32),
                        pltpu.VMEM((N_PAIRS, 128, SSM_N), F32)],
        compiler_params=_params(2),
        name="conv_ssd",
    )(xbc_src, z_src, dt3, h0, c0, cw, cb, alog, dskip, gout)


def _merge_kernel(x_ref, gmix_ref, wg_ref, oa_ref, ys_ref, oc_ref, wa_ref, ws_ref, wc_ref, wo_ref, o_ref):
    x = x_ref[...]
    h = _rms(x, gmix_ref[...]).astype(BF16)
    branches = ((oa_ref, wa_ref), (ys_ref, ws_ref), (oc_ref, wc_ref))
    merged = None
    for b, (act_ref, w_ref) in enumerate(branches):
        gate = _sigmoid(_dot_nt(h, wg_ref[b * D:(b + 1) * D, :]))
        term = gate * _dot(act_ref[...].astype(BF16), w_ref[...])
        merged = term if merged is None else merged + term
    o_ref[...] = x + _dot(merged.astype(BF16), wo_ref[...])


def _merge(x, gmix, wg, oa, ys, oc, wa, ws, wc, wo):
    n = x.shape[0]
    tm = min(512, n)
    row = lambda width: pl.BlockSpec((tm, width), lambda i: (i, 0))
    whole = lambda shape: pl.BlockSpec(shape, lambda i: (0, 0), pipeline_mode=pl.Buffered(1))
    return pl.pallas_call(
        _merge_kernel,
        out_shape=jax.ShapeDtypeStruct((n, D), F32),
        grid=(n // tm,),
        in_specs=[row(D), whole((1, D)), whole((3 * D, D)), row(D), row(SSM_INNER), row(D),
                  whole((D, D)), whole((SSM_INNER, D)), whole((D, D)), whole((D, D))],
        out_specs=row(D),
        compiler_params=_params(1),
        name="branch_merge",
    )(x, gmix, wg, oa, ys, oc, wa, ws, wc, wo)


def _mlp_kernel(x_ref, g_ref, wu_ref, wd_ref, o_ref, h_scr, acc):
    f = pl.program_id(1)

    @pl.when(f == 0)
    def _():
        x = x_ref[...]
        h_scr[...] = _rms(x, g_ref[...]).astype(BF16)
        acc[...] = x

    a = jnp.maximum(_dot(h_scr[...], wu_ref[...]), 0.0)
    acc[...] += _dot((a * a).astype(BF16), wd_ref[...])

    @pl.when(f == pl.num_programs(1) - 1)
    def _():
        o_ref[...] = acc[...]


def _mlp(x, g, wu, wd):
    n = x.shape[0]
    tm = min(1024, n)
    tf = 1024
    return pl.pallas_call(
        _mlp_kernel,
        out_shape=jax.ShapeDtypeStruct((n, D), F32),
        grid=(n // tm, D_FF // tf),
        in_specs=[pl.BlockSpec((tm, D), lambda i, f: (i, 0)),
                  pl.BlockSpec((1, D), lambda i, f: (0, 0)),
                  pl.BlockSpec((D, tf), lambda i, f: (0, f)),
                  pl.BlockSpec((tf, D), lambda i, f: (f, 0))],
        out_specs=pl.BlockSpec((tm, D), lambda i, f: (i, 0)),
        scratch_shapes=[pltpu.VMEM((tm, D), BF16), pltpu.VMEM((tm, D), F32)],
        compiler_params=_params(2),
        name="relu2_mlp",
    )(x, g, wu, wd)


def _lambda_init(layer):
    return 0.8 - 0.6 * math.exp(-0.3 * layer)


def _layer_weights(l, p):
    w_in_t = jnp.swapaxes(p['w_in'], 1, 2)[l]
    cols = lambda off, width: w_in_t[off:off + width]
    w_attn = jnp.concatenate([cols(OFF_Q, D), cols(OFF_CQ, D), cols(OFF_K, D), cols(OFF_V, D)], axis=0).astype(BF16)
    w_ssm = jnp.concatenate([cols(OFF_XBC, CONV_CH), cols(OFF_Z, SSM_INNER)], axis=0).astype(BF16)
    w_gate = cols(OFF_G, 3 * D).astype(BF16)
    w_dt = jnp.pad(cols(OFF_DT, SSM_HEADS), ((0, 128 - SSM_HEADS), (0, 0))).astype(BF16)
    row = lambda v: v.reshape(1, -1).astype(F32)
    return dict(
        g_mix=row(p['g_mix'][l]), w_attn=w_attn, w_ssm=w_ssm, w_dt=w_dt, w_gate=w_gate,
        dt_bias=jnp.pad(row(p['dt_bias'][l]), ((0, 0), (0, 128 - SSM_HEADS))),
        gq=row(jnp.tile(p['g_da_q'][l], TN // DK)), gk=row(jnp.tile(p['g_da_k'][l], TN // DK)),
        gcq=row(p['g_ca_q'][l]),
        lam=jnp.stack([p['lambda_q1'][l], p['lambda_k1'][l], p['lambda_q2'][l], p['lambda_k2'][l]]).astype(F32),
        g_da_out=row(p['g_da_out'][l]),
        conv_w=p['conv_w'][l].astype(F32), conv_b=row(p['conv_b'][l]),
        a_log=jnp.pad(row(p['a_log'][l]), ((0, 0), (0, 128 - SSM_HEADS))),
        d_skip=row(jnp.repeat(p['d_skip'][l], SSM_P)), g_ssm_out=row(p['g_ssm_out'][l]),
        w_br_attn=p['w_br_attn'][l].astype(BF16), w_br_ssm=p['w_br_ssm'][l].astype(BF16),
        w_br_cross=p['w_br_cross'][l].astype(BF16), w_out=p['w_out'][l].astype(BF16),
        g_mlp=row(p['g_mlp'][l]), w_up=p['w_up'][l].astype(BF16), w_down=p['w_down'][l].astype(BF16),
        g_mem=row(p['g_mem'][l]), w_mem_kv=p['w_mem_kv'][l].astype(BF16), g_ca_k=row(p['g_ca_k'][l]),
    )


def _seg64_matrix():
    i = jnp.arange(SEG_W) // DK
    return jnp.where(i[:, None] == i[None, :], 1.0 / DK, 0.0).astype(BF16)


def _layer(x, bsz, w, s64, attn_fn, mem_k, mem_v, h0, c0, state_layer, act_dtype):
    n = x.shape[0]
    t = n // bsz
    ub, k32, v32, uf, dt = _proj(x, w['g_mix'], w['w_attn'], w['w_ssm'], w['w_dt'], w['dt_bias'],
                                 w['gq'], w['gk'], w['gcq'], s64)
    ub3 = ub.reshape(bsz, t, -1)
    o_a = attn_fn(ub3, k32, v32)
    uf3 = uf.reshape(bsz, t, -1)
    y_n, h_new, c_new = _ssd(uf3, 0, uf3, ((PB_Z - PB_XBC) * TN) // SSM_INNER, dt.reshape(bsz, t, 128), h0, c0,
                             state_layer, w['conv_w'], w['conv_b'], w['a_log'], w['d_skip'], w['g_ssm_out'], act_dtype)
    if act_dtype == BF16:
        cq3, cq_blk = ub3, (PB_CQ * TN) // D
    else:
        cq3, cq_blk = ub3[:, :, PB_CQ * TN:PB_K * TN].astype(F32), 0
    o_c = _cross(cq3, cq_blk, mem_k, mem_v, act_dtype)
    x = _merge(x, w['g_mix'], w['w_gate'], o_a.reshape(n, D), y_n.reshape(n, SSM_INNER), o_c.reshape(n, D),
               w['w_br_attn'], w['w_br_ssm'], w['w_br_cross'], w['w_out'])
    x = _mlp(x, w['g_mlp'], w['w_up'], w['w_down'])
    return x, k32, v32, h_new, c_new


def kernel(x_prompt, x_sample, cache_k, cache_v, cache_mem_k, cache_mem_v, state_ssm, state_conv, page_table, mem_prompt, rel_bias_table, g_mix, w_in, g_da_q, g_da_k, lambda_q1, lambda_k1, lambda_q2, lambda_k2, g_da_out, w_br_attn, conv_w, conv_b, dt_bias, a_log, d_skip, g_ssm_out, w_br_ssm, g_mem, w_mem_kv, g_ca_q, g_ca_k, w_br_cross, w_out, g_mlp, w_up, w_down):
    p = dict(g_mix=g_mix, w_in=w_in, g_da_q=g_da_q, g_da_k=g_da_k, lambda_q1=lambda_q1, lambda_k1=lambda_k1,
             lambda_q2=lambda_q2, lambda_k2=lambda_k2, g_da_out=g_da_out, w_br_attn=w_br_attn, conv_w=conv_w,
             conv_b=conv_b, dt_bias=dt_bias, a_log=a_log, d_skip=d_skip, g_ssm_out=g_ssm_out, w_br_ssm=w_br_ssm,
             g_mem=g_mem, w_mem_kv=w_mem_kv, g_ca_q=g_ca_q, g_ca_k=g_ca_k, w_br_cross=w_br_cross, w_out=w_out,
             g_mlp=g_mlp, w_up=w_up, w_down=w_down)
    depth = w_in.shape[0]
    bp, tp, _ = x_prompt.shape
    bs, ts, _ = x_sample.shape
    n_phys = cache_k.shape[1]
    past_len = page_table.shape[1] * PAGE
    tq = min(512, tp)
    table = rel_bias_table.astype(F32)
    s64 = _seg64_matrix()

    bias_diag, bias_near = _bias_tiles(table, 0, (tq, tq)), _bias_tiles(table, tq, (tq, tq))
    as_heads = lambda a: a.reshape(N_HEADS * 2 * ts, PAGE)
    bias_past = as_heads(_bias_tiles(table, PAGE, (ts, PAGE)))
    bias_new = as_heads(_bias_tiles(table, 0, (ts, PAGE)))

    cache_k4 = jnp.transpose(cache_k, (0, 1, 3, 4, 2)).reshape(depth, n_phys, D, PAGE)
    cache_v4 = cache_v.reshape(depth, n_phys, PAGE * N_HEADS, DV)
    mem_flat = mem_prompt.reshape(bp * MEM_LEN, D)
    zero_h = jnp.zeros((1, bp, N_PAIRS, 128, SSM_N), F32)
    zero_c = jnp.zeros((1, bp, CONV_K - 1, CONV_CH), F32)
    state_ssm5 = state_ssm.reshape(depth, bs, N_PAIRS, 128, SSM_N)

    xp = x_prompt.reshape(bp * tp, D)
    xs = x_sample.reshape(bs * ts, D)
    outs = {k: [] for k in ('kp', 'vp', 'mk', 'mv', 'hp', 'cp', 'ks', 'vs', 'hs', 'cs')}
    for l in range(depth):
        w = _layer_weights(l, p)
        lam0 = _lambda_init(l)

        mk, mv = _memkv(mem_flat, w['g_mem'], w['w_mem_kv'], w['g_ca_k'])
        attn_p = lambda ub3, k32, v32: _attn_prompt(ub3, w['lam'], bias_diag, bias_near, w['g_da_out'],
                                                    lam0=lam0, tq=tq)
        xp, k, v, hn, cn = _layer(xp, bp, w, s64, attn_p, mk.reshape(bp, MEM_LEN, D), mv.reshape(bp, MEM_LEN, D),
                                  zero_h, zero_c, 0, BF16)
        for key, val in zip(('kp', 'vp', 'mk', 'mv', 'hp', 'cp'), (k, v, mk, mv, hn, cn)):
            outs[key].append(val)

        def attn_s(ub3, k32, v32):
            q3 = ub3[:, :, 0:D].astype(F32)
            return _attn_sample(page_table, q3, k32.reshape(bs, ts, D), v32.reshape(bs, ts, D), cache_k4, cache_v4,
                                l, bias_past, bias_new, w['lam'], w['g_da_out'], lam0=lam0)

        xs, k, v, hn, cn = _layer(xs, bs, w, s64, attn_s, cache_mem_k[l].reshape(bs, MEM_LEN, D),
                                  cache_mem_v[l].reshape(bs, MEM_LEN, D), state_ssm5, state_conv, l, F32)
        for key, val in zip(('ks', 'vs', 'hs', 'cs'), (k, v, hn, cn)):
            outs[key].append(val)

    st = lambda key, shape: jnp.stack(outs[key]).reshape((depth,) + shape)
    return (xp.reshape(bp, tp, D), xs.reshape(bs, ts, D),
            st('kp', (bp, tp, N_MAPS, DK)), st('vp', (bp, tp, N_HEADS, DV)),
            st('mk', (bp, MEM_LEN, CA_HEADS, CA_D)), st('mv', (bp, MEM_LEN, CA_HEADS, CA_D)),
            st('hp', (bp, SSM_HEADS, SSM_P, SSM_N)), st('cp', (bp, CONV_K - 1, CONV_CH)),
            st('ks', (bs, ts, N_MAPS, DK)), st('vs', (bs, ts, N_HEADS, DV)),
            st('hs', (bs, SSM_HEADS, SSM_P, SSM_N)), st('cs', (bs, CONV_K - 1, CONV_CH)))
```

```python
import functools
import math

import jax
import jax.numpy as jnp
from jax import lax
from jax.experimental import pallas as pl
from jax.experimental.pallas import tpu as pltpu

F32, BF16 = jnp.float32, jnp.bfloat16
EPS = 1e-5
NEG = -1e30
LOG2E = math.log2(math.e)

D = 1024
N_MAPS, DK = 16, 64
N_HEADS, DV = 8, 128
SSM_INNER, SSM_P, SSM_N, SSM_G = 2048, 64, 128, 8
SSM_HEADS = SSM_INNER // SSM_P
N_PAIRS = SSM_HEADS // 2
CONV_K, CONV_CH = 4, 4096
MEM_LEN, CA_HEADS, CA_D = 256, 4, 256
D_FF = 4096
PAGE = 128
REL_BUCKETS, REL_MAX_DIST = 32, 128
SSM_L = 128

OFF_Q, OFF_K, OFF_V, OFF_Z, OFF_XBC = 0, 1024, 2048, 3072, 5120
OFF_DT = OFF_XBC + CONV_CH
OFF_CQ = OFF_DT + SSM_HEADS
OFF_G = OFF_CQ + D

TN = 1024
SEG_W = 256
PB_Q, PB_CQ, PB_K, PB_V, PB_XBC, PB_Z, PB_END = 0, 1, 2, 3, 4, 8, 10
VMEM_LIMIT = 48 << 20


def _params(n_axes, vmem=VMEM_LIMIT):
    return pltpu.CompilerParams(dimension_semantics=("arbitrary",) * n_axes, vmem_limit_bytes=vmem)


def _rms(x, gain):
    ms = jnp.mean(x * x, axis=-1, keepdims=True)
    return x * lax.rsqrt(ms + EPS) * gain


def _sigmoid(x):
    return 1.0 / (1.0 + jnp.exp(-x))


def _softplus(x):
    return jnp.maximum(x, 0.0) + jnp.log1p(jnp.exp(-jnp.abs(x)))


def _dot(a, b):
    return jnp.dot(a, b, preferred_element_type=F32)


def _dot_nt(a, b):
    return lax.dot_general(a, b, (((1,), (1,)), ((), ())), preferred_element_type=F32)


def _dot_tn(a, b):
    return lax.dot_general(a, b, (((0,), (0,)), ((), ())), preferred_element_type=F32)


def _dot_split3(a01, x):
    hi = x.astype(BF16)
    r = x - hi.astype(F32)
    mid = r.astype(BF16)
    lo = (r - mid.astype(F32)).astype(BF16)
    return _dot(a01, hi) + _dot(a01, mid) + _dot(a01, lo)


def _bucket_thresholds():
    max_exact = REL_BUCKETS // 2
    ratio = REL_MAX_DIST / max_exact
    return [math.ceil(max_exact * ratio ** (k / (REL_BUCKETS - max_exact))) for k in range(1, REL_BUCKETS - max_exact)]


def _bias_kernel(tab_ref, o_ref, *, off, shape):
    m = pl.program_id(0)
    r = lax.broadcasted_iota(jnp.int32, shape, 0)
    c = lax.broadcasted_iota(jnp.int32, shape, 1)
    n = off + r - c
    nn = jnp.maximum(n, 0)
    max_exact = REL_BUCKETS // 2
    large = jnp.full(shape, max_exact, jnp.int32)
    for t in _bucket_thresholds():
        large = large + (nn >= t).astype(jnp.int32)
    bucket = jnp.where(nn < max_exact, nn, large)
    val = jnp.zeros(shape, F32)
    for b in range(REL_BUCKETS):
        val = jnp.where(bucket == b, tab_ref[b, m], val)
    val = (val - tab_ref[REL_BUCKETS - 1, m]) * LOG2E
    o_ref[0] = jnp.where(n >= 0, val, NEG)


def _bias_tiles(table, off, shape):
    return pl.pallas_call(
        functools.partial(_bias_kernel, off=off, shape=shape),
        out_shape=jax.ShapeDtypeStruct((N_MAPS,) + shape, F32),
        grid=(N_MAPS,),
        in_specs=[pl.BlockSpec(memory_space=pltpu.SMEM)],
        out_specs=pl.BlockSpec((1,) + shape, lambda m: (m, 0, 0)),
        compiler_params=_params(1),
        name="rel_bias_tiles",
    )(table)


def _proj_attn_kernel(x_ref, g_ref, w_ref, wdt_ref, dtb_ref, gq_ref, gk_ref, gcq_ref, s64_ref,
                      ub_ref, k32_ref, v32_ref, dt_ref):
    h = _rms(x_ref[...], g_ref[...]).astype(BF16)
    dt_ref[...] = _softplus(_dot_nt(h, wdt_ref[...]) + dtb_ref[...])

    def proj(blk):
        return _dot_nt(h, w_ref[blk * TN:(blk + 1) * TN, :])

    def seg64_norm(u, gain):
        sq = (u * u).astype(BF16)
        ms = jnp.concatenate([_dot(sq[:, c * SEG_W:(c + 1) * SEG_W], s64_ref[...]) for c in range(TN // SEG_W)],
                             axis=1)
        return u * lax.rsqrt(ms + EPS) * gain

    def out(blk):
        return slice(blk * TN, (blk + 1) * TN)

    ub_ref[:, out(PB_Q)] = (seg64_norm(proj(PB_Q), gq_ref[...]) * (DK ** -0.5 * LOG2E)).astype(BF16)
    u = proj(PB_CQ)
    for s in range(TN // CA_D):
        seg = u[:, s * CA_D:(s + 1) * CA_D]
        ub_ref[:, PB_CQ * TN + s * CA_D:PB_CQ * TN + (s + 1) * CA_D] = (
            _rms(seg, gcq_ref[...]) * (CA_D ** -0.5)).astype(BF16)
    kn = seg64_norm(proj(PB_K), gk_ref[...])
    k32_ref[...] = kn if k32_ref.shape == kn.shape else kn.T
    ub_ref[:, out(PB_K)] = kn.astype(BF16)
    u = proj(PB_V)
    v32_ref[...] = u
    ub_ref[:, out(PB_V)] = u.astype(BF16)


def _proj_ssm_kernel(x_ref, g_ref, w_ref, uf_ref):
    h = _rms(x_ref[...], g_ref[...]).astype(BF16)
    for blk in range(uf_ref.shape[1] // TN):
        uf_ref[:, blk * TN:(blk + 1) * TN] = _dot_nt(h, w_ref[blk * TN:(blk + 1) * TN, :])


def _proj(x, g, w_attn, w_ssm, wdt, dtb, gq, gk, gcq, s64, k_seq_len):
    n = x.shape[0]
    tm = min(512, n)
    if k_seq_len is None:
        k_shape, k_spec = (n, D), pl.BlockSpec((tm, D), lambda i: (i, 0))
    else:
        per_seq = k_seq_len // tm
        k_shape = (n // k_seq_len, D, k_seq_len)
        k_spec = pl.BlockSpec((None, D, tm), lambda i: (i // per_seq, 0, i % per_seq))
    row = lambda width: pl.BlockSpec((tm, width), lambda i: (i, 0))
    whole = lambda shape: pl.BlockSpec(shape, lambda i: (0, 0), pipeline_mode=pl.Buffered(1))
    ub, k32, v32, dt = pl.pallas_call(
        _proj_attn_kernel,
        out_shape=(jax.ShapeDtypeStruct((n, PB_XBC * TN), BF16),
                   jax.ShapeDtypeStruct(k_shape, F32),
                   jax.ShapeDtypeStruct((n, D), F32),
                   jax.ShapeDtypeStruct((n, 128), F32)),
        grid=(n // tm,),
        in_specs=[row(D), whole((1, D)), whole((PB_XBC * TN, D)), whole((128, D)), whole((1, 128)),
                  whole((1, TN)), whole((1, TN)), whole((1, CA_D)), whole((SEG_W, SEG_W))],
        out_specs=(row(PB_XBC * TN), k_spec, row(D), row(128)),
        compiler_params=_params(1),
        name="in_proj_attn",
    )(x, g, w_attn, wdt, dtb, gq, gk, gcq, s64)
    ssm_w = (PB_END - PB_XBC) * TN
    uf = pl.pallas_call(
        _proj_ssm_kernel,
        out_shape=jax.ShapeDtypeStruct((n, ssm_w), F32),
        grid=(n // tm,),
        in_specs=[row(D), whole((1, D)), whole((ssm_w, D))],
        out_specs=row(ssm_w),
        compiler_params=_params(1, 56 << 20),
        name="in_proj_ssm",
    )(x, g, w_ssm)
    return ub, k32, v32, uf, dt


def _memkv_kernel(m_ref, g_ref, w_ref, gk_ref, mk_ref, mv_ref, h_scr):
    j = pl.program_id(1)

    @pl.when(j == 0)
    def _():
        h_scr[...] = _rms(m_ref[...], g_ref[...]).astype(BF16)

    u = _dot(h_scr[...], w_ref[...])

    @pl.when(j == 0)
    def _():
        for s in range(CA_HEADS):
            mk_ref[:, s * CA_D:(s + 1) * CA_D] = _rms(u[:, s * CA_D:(s + 1) * CA_D], gk_ref[...])

    @pl.when(j == 1)
    def _():
        mv_ref[...] = u


def _memkv(mem, g, w, gk):
    n = mem.shape[0]
    tm = min(512, n)
    return pl.pallas_call(
        _memkv_kernel,
        out_shape=(jax.ShapeDtypeStruct((n, D), F32), jax.ShapeDtypeStruct((n, D), F32)),
        grid=(n // tm, 2),
        in_specs=[pl.BlockSpec((tm, D), lambda i, j: (i, 0)),
                  pl.BlockSpec((1, D), lambda i, j: (0, 0)),
                  pl.BlockSpec((D, D), lambda i, j: (0, j)),
                  pl.BlockSpec((1, CA_D), lambda i, j: (0, 0))],
        out_specs=(pl.BlockSpec((tm, D), lambda i, j: (i, 0)), pl.BlockSpec((tm, D), lambda i, j: (i, 0))),
        scratch_shapes=[pltpu.VMEM((tm, D), BF16)],
        compiler_params=_params(2),
        name="mem_kv",
    )(mem, g, w, gk)


def _lambda(lam_ref, lam0):
    lv = lam_ref[...]
    s1 = jnp.sum(lv[0:1] * lv[1:2], axis=-1, keepdims=True)
    s2 = jnp.sum(lv[2:3] * lv[3:4], axis=-1, keepdims=True)
    return jnp.exp(s1) - jnp.exp(s2) + lam0


def _softmax_step(state, s, v_bf16):
    m, l, acc = state
    m_new = jnp.maximum(m, jnp.max(s, axis=-1, keepdims=True))
    alpha = jnp.exp2(m - m_new)
    p = jnp.exp2(s - jnp.tile(m_new, (1, s.shape[1] // m_new.shape[1])))
    l = alpha * l + jnp.sum(p, axis=-1, keepdims=True)
    acc = alpha * acc + _dot(p.astype(BF16), v_bf16)
    return m_new, l, acc


def _attn_prompt_kernel(lam_ref, q_ref, k_ref, v_ref, bdiag_ref, bnear_ref, gout_ref, o_ref, m_ref, l_ref, acc_ref,
                        *, tq, lam0):
    qi = pl.program_id(2)
    q = q_ref[...]
    lane = lax.broadcasted_iota(jnp.int32, q.shape, 1)
    qm = jnp.concatenate([jnp.where(lane < DK, q, jnp.zeros_like(q)), jnp.where(lane >= DK, q, jnp.zeros_like(q))],
                         axis=0)
    m_ref[...] = jnp.full(m_ref.shape, NEG, F32)
    l_ref[...] = jnp.zeros(l_ref.shape, F32)
    acc_ref[...] = jnp.zeros(acc_ref.shape, F32)

    def update(kb, bias_ref):
        start = pl.multiple_of(kb * tq, tq)
        s = _dot_nt(qm, k_ref[pl.ds(start, tq), :])
        if bias_ref is not None:
            s = s + bias_ref[...]
        m_new, l_new, acc_new = _softmax_step((m_ref[...], l_ref[...], acc_ref[...]), s, v_ref[pl.ds(start, tq), :])
        m_ref[...] = m_new
        l_ref[...] = l_new
        acc_ref[...] = acc_new

    def loop(lo, hi, bias_ref):
        def body(kb, carry):
            update(kb, bias_ref)
            return carry
        lax.fori_loop(lo, hi, body, 0)

    loop(0, qi - 1, None)
    loop(jnp.maximum(qi - 1, 0), qi, bnear_ref)
    update(qi, bdiag_ref)

    lam = _lambda(lam_ref, lam0)
    o_maps = acc_ref[...] / l_ref[...]
    o = o_maps[0:tq] - lam * o_maps[tq:2 * tq]
    o_ref[...] = (_rms(o, gout_ref[...]) * (1.0 - lam0)).astype(o_ref.dtype)


def _attn_prompt(ub3, lam_vecs, bias_diag, bias_near, gout, *, lam0, tq):
    b, t, _ = ub3.shape
    kcol, vcol = (PB_K * TN) // DV, (PB_V * TN) // DV
    bias_spec = pl.BlockSpec((2 * tq, tq), lambda h, b, i: (h, 0))
    bias_diag, bias_near = (a.reshape(N_MAPS * tq, tq) for a in (bias_diag, bias_near))
    return pl.pallas_call(
        functools.partial(_attn_prompt_kernel, tq=tq, lam0=lam0),
        out_shape=jax.ShapeDtypeStruct((b, t, D), BF16),
        grid=(N_HEADS, b, t // tq),
        in_specs=[pl.BlockSpec((4, DK), lambda h, b, i: (0, 0)),
                  pl.BlockSpec((None, tq, DV), lambda h, b, i: (b, i, h)),
                  pl.BlockSpec((None, t, DV), lambda h, b, i: (b, 0, kcol + h)),
                  pl.BlockSpec((None, t, DV), lambda h, b, i: (b, 0, vcol + h)),
                  bias_spec, bias_spec,
                  pl.BlockSpec((1, DV), lambda h, b, i: (0, 0))],
        out_specs=pl.BlockSpec((None, tq, DV), lambda h, b, i: (b, i, h)),
        scratch_shapes=[pltpu.VMEM((2 * tq, DV), F32)] * 3,
        compiler_params=_params(3),
        name="diff_attn_prompt",
    )(lam_vecs, ub3, ub3, ub3, bias_diag, bias_near, gout)


def _head_queries(q, h):
    qh = q[:, h * DV:(h + 1) * DV]
    lane = lax.broadcasted_iota(jnp.int32, qh.shape, 1)
    return jnp.concatenate([jnp.where(lane < DK, qh, 0.0), jnp.where(lane >= DK, qh, 0.0)], axis=0).astype(BF16)


def _attn_sample_kernel(pt_ref, q_ref, kn_ref, vn_ref, bpast_ref, bnew_ref, lam_ref, gout_ref, *rest, npg, lam0):
    k_refs, v_refs = rest[:npg], rest[npg:2 * npg]
    o_ref, m_scr, l_scr, acc_scr = rest[2 * npg:]
    c = pl.program_id(1)
    last = c == pl.num_programs(1) - 1
    tdec = q_ref.shape[0]
    rows = 2 * tdec

    @pl.when(c == 0)
    def _():
        m_scr[...] = jnp.full(m_scr.shape, NEG, F32)
        l_scr[...] = jnp.zeros(l_scr.shape, F32)
        acc_scr[...] = jnp.zeros(acc_scr.shape, F32)

    q = q_ref[...]
    qms = [_head_queries(q, h) for h in range(N_HEADS)]

    def update(s, pv_fn):
        m_old = m_scr[...]
        m_new = jnp.maximum(m_old, jnp.max(s, axis=-1, keepdims=True))
        alpha = jnp.exp2(m_old - m_new)
        p = jnp.exp2(s - m_new[:, 0:1])
        l_scr[...] = alpha * l_scr[...] + jnp.sum(p, axis=-1, keepdims=True)
        acc_scr[...] = alpha * acc_scr[...] + pv_fn(p.astype(BF16))
        m_scr[...] = m_new

    near = last.astype(F32)
    s_pages = []
    for i in range(npg):
        s_i = jnp.concatenate([_dot(qms[h], k_refs[i][h * DV:(h + 1) * DV, :].astype(BF16))
                               for h in range(N_HEADS)], axis=0)
        if i == npg - 1:
            s_i = s_i + near * bpast_ref[...]
        s_pages.append(s_i)

    def pv_pages(p):
        heads = []
        for h in range(N_HEADS):
            out = None
            for i in range(npg):
                v_h = v_refs[i][pl.ds(h, PAGE, stride=N_HEADS), :].astype(BF16)
                part = _dot(p[h * rows:(h + 1) * rows, i * PAGE:(i + 1) * PAGE], v_h)
                out = part if out is None else out + part
            heads.append(out)
        return jnp.concatenate(heads, axis=0)

    update(jnp.concatenate(s_pages, axis=1), pv_pages)

    @pl.when(last)
    def _():
        pad = jnp.zeros((PAGE - tdec, D), F32)
        kn = jnp.concatenate([kn_ref[...], pad], axis=0).astype(BF16)
        vn = jnp.concatenate([vn_ref[...], pad], axis=0).astype(BF16)
        s_new = jnp.concatenate([_dot_nt(qms[h], kn[:, h * DV:(h + 1) * DV]) for h in range(N_HEADS)], axis=0)
        update(s_new + bnew_ref[...],
               lambda p: jnp.concatenate([_dot(p[h * rows:(h + 1) * rows], vn[:, h * DV:(h + 1) * DV])
                                          for h in range(N_HEADS)], axis=0))
        lam = _lambda(lam_ref, lam0)
        o_all = acc_scr[...] / l_scr[...]
        for h in range(N_HEADS):
            o = o_all[h * rows:h * rows + tdec] - lam * o_all[h * rows + tdec:(h + 1) * rows]
            o_ref[:, h * DV:(h + 1) * DV] = _rms(o, gout_ref[...]) * (1.0 - lam0)


def _attn_sample(page_table, q3, kn3, vn3, cache_k4, cache_v4, layer, bpast, bnew, lam_vecs, gout, *, lam0):
    bs, tdec, _ = q3.shape
    n_pages = page_table.shape[1]
    npg = math.gcd(n_pages, 16)
    pt = page_table.reshape(-1)

    def page_spec(i):
        return pl.BlockSpec((None, None, D, PAGE),
                            lambda b, c, pt_ref: (layer, pt_ref[b * n_pages + c * npg + i], 0, 0))

    per_b = pl.BlockSpec((None, tdec, D), lambda b, c, pt_ref: (b, 0, 0))
    whole = lambda shape: pl.BlockSpec(shape, lambda b, c, pt_ref: (0,) * len(shape))
    stat = pltpu.VMEM((N_HEADS * 2 * tdec, DV), F32)
    return pl.pallas_call(
        functools.partial(_attn_sample_kernel, npg=npg, lam0=lam0),
        out_shape=jax.ShapeDtypeStruct((bs, tdec, D), F32),
        grid_spec=pltpu.PrefetchScalarGridSpec(
            num_scalar_prefetch=1,
            grid=(bs, n_pages // npg),
            in_specs=[per_b, per_b, per_b,
                      whole((N_HEADS * 2 * tdec, PAGE)), whole((N_HEADS * 2 * tdec, PAGE)),
                      whole((4, DK)), whole((1, DV))]
                     + [page_spec(i) for i in range(npg)] + [page_spec(i) for i in range(npg)],
            out_specs=per_b,
            scratch_shapes=[stat, stat, stat]),
        compiler_params=_params(2),
        name="diff_attn_sample",
    )(pt, q3, kn3, vn3, bpast, bnew, lam_vecs, gout, *([cache_k4] * npg), *([cache_v4] * npg))


def _cross_kernel(cq_ref, mk_ref, mv_ref, o_ref):
    rows = cq_ref.shape[0]
    for h in range(CA_HEADS):
        cols = slice(h * CA_D, (h + 1) * CA_D)
        q = cq_ref[:, cols]
        if rows % 16:
            q = jnp.concatenate([q.astype(F32), jnp.zeros((16 - rows % 16, CA_D), F32)], axis=0)
        s = _dot_nt(q.astype(BF16), mk_ref[:, cols].astype(BF16))
        p = jnp.exp(s - jnp.max(s, axis=-1, keepdims=True))
        o = _dot(p.astype(BF16), mv_ref[:, cols].astype(BF16)) / jnp.sum(p, axis=-1, keepdims=True)
        o_ref[:, cols] = o[0:rows].astype(o_ref.dtype)


def _cross(cq3, col_block, mem_k, mem_v, out_dtype):
    b, t, _ = cq3.shape
    tm = min(512, t)
    mem_spec = pl.BlockSpec((None, MEM_LEN, D), lambda b, i: (b, 0, 0))
    return pl.pallas_call(
        _cross_kernel,
        out_shape=jax.ShapeDtypeStruct((b, t, D), out_dtype),
        grid=(b, t // tm),
        in_specs=[pl.BlockSpec((None, tm, D), lambda b, i: (b, i, col_block)), mem_spec, mem_spec],
        out_specs=pl.BlockSpec((None, tm, D), lambda b, i: (b, i, 0)),
        compiler_params=_params(2),
        name="mem_cross_attn",
    )(cq3, mem_k, mem_v)


def _ssd_kernel(xbc_ref, z_ref, dt_ref, h0_ref, c0_ref, cw_ref, cb_ref, alog_ref, dskip_ref, gout_ref,
                y_ref, hn_ref, cn_ref, buf, xc_scr, dt_scr, yb, h_scr, *, tv):
    L = SSM_L
    c = pl.program_id(1)
    head = 8

    @pl.when(c == 0)
    def _():
        buf[0:head, :] = jnp.zeros((head, CONV_CH), F32)
        buf[head - (CONV_K - 1):head, :] = c0_ref[...]
        h_scr[...] = h0_ref[...]
        if tv < L:
            buf[head + tv:, :] = jnp.zeros((L - tv, CONV_CH), F32)
            dt_scr[...] = jnp.zeros(dt_scr.shape, F32)

    buf[head:head + tv, :] = xbc_ref[...]
    cn_ref[...] = buf[head + tv - (CONV_K - 1):head + tv, :]

    cw = cw_ref[...]
    for j in range(CONV_CH // 512):
        cols = slice(j * 512, (j + 1) * 512)
        acc = cb_ref[:, cols] + cw[CONV_K - 1:CONV_K, cols] * buf[head:head + L, cols]
        for k in range(CONV_K - 1):
            acc = acc + cw[k:k + 1, cols] * buf[head - (CONV_K - 1) + k:head - (CONV_K - 1) + k + L, cols]
        xc_scr[:, cols] = acc * _sigmoid(acc)

    if tv == L:
        buf[0:head, :] = buf[L:L + head, :]
        dt = dt_ref[...]
    else:
        dt_scr[0:tv, :] = dt_ref[...]
        dt = dt_scr[...]

    a_neg = -jnp.exp(alog_ref[...])
    d_a = dt * a_neg
    row = lax.broadcasted_iota(jnp.int32, (L, L), 0)
    col = lax.broadcasted_iota(jnp.int32, (L, L), 1)
    causal = col <= row
    a_cum = _dot_split3(causal.astype(BF16), d_a)
    a_cum_t = a_cum.T
    dt_t = dt.T
    a_last = a_cum[L - 1:L, :]
    e_cum = jnp.exp(a_cum)
    to_end = jnp.exp(a_last - a_cum) * dt
    e_last = jnp.exp(a_last)
    lane = lax.broadcasted_iota(jnp.int32, (L, 128), 1)
    first = lane < SSM_P
    row_first = lax.broadcasted_iota(jnp.int32, (128, 128), 0) < SSM_P

    for g in range(SSM_G):
        b_g = xc_scr[:, SSM_INNER + g * SSM_N:SSM_INNER + (g + 1) * SSM_N].astype(BF16)
        c_g = xc_scr[:, SSM_INNER + SSM_G * SSM_N + g * SSM_N:SSM_INNER + SSM_G * SSM_N + (g + 1) * SSM_N].astype(BF16)
        cb = _dot_nt(c_g, b_g)
        for jj in (2 * g, 2 * g + 1):
            cols = slice(jj * 128, (jj + 1) * 128)
            x_pair = xc_scr[:, cols]
            x_bf = x_pair.astype(BF16)
            yw = []
            for hh in (2 * jj, 2 * jj + 1):
                seg = a_cum[:, hh:hh + 1] - a_cum_t[hh:hh + 1, :]
                w = cb * jnp.exp(jnp.where(causal, seg, NEG)) * dt_t[hh:hh + 1, :]
                yw.append(_dot(w.astype(BF16), x_bf))
            ha, hb = 2 * jj, 2 * jj + 1
            hp = h_scr[jj]
            y = jnp.where(first, yw[0], yw[1])
            y = y + _dot_nt(c_g, hp.astype(BF16)) * jnp.where(first, e_cum[:, ha:ha + 1], e_cum[:, hb:hb + 1])
            y = y + dskip_ref[:, cols] * x_pair
            te = jnp.where(first, to_end[:, ha:ha + 1], to_end[:, hb:hb + 1])
            upd = _dot_tn((x_pair * te).astype(BF16), b_g)
            h_scr[jj] = hp * jnp.where(row_first, e_last[:, ha:ha + 1], e_last[:, hb:hb + 1]) + upd
            yb[:, cols] = y[0:tv] * (z_ref[:, cols] * _sigmoid(z_ref[:, cols]))

    gw = SSM_INNER // SSM_G
    for g in range(SSM_G):
        cols = slice(g * gw, (g + 1) * gw)
        y_ref[:, cols] = _rms(yb[:, cols], gout_ref[:, cols]).astype(y_ref.dtype)
    hn_ref[...] = h_scr[...]


def _ssd(xbc_src, xbc_blk, z_src, z_blk, dt3, h0, c0, layer, cw, cb, alog, dskip, gout, out_dtype):
    b, t, _ = dt3.shape
    tv = min(SSM_L, t)
    nc = t // tv
    seq = lambda width, blk: pl.BlockSpec((None, tv, width), lambda b, c: (b, c, blk))
    per_b = lambda shape: pl.BlockSpec((None,) + shape, lambda b, c: (b,) + (0,) * len(shape))
    state = lambda shape: pl.BlockSpec((None, None) + shape, lambda b, c: (layer, b) + (0,) * len(shape))
    whole = lambda shape: pl.BlockSpec(shape, lambda b, c: (0,) * len(shape))
    return pl.pallas_call(
        functools.partial(_ssd_kernel, tv=tv),
        out_shape=(jax.ShapeDtypeStruct((b, t, SSM_INNER), out_dtype),
                   jax.ShapeDtypeStruct((b, N_PAIRS, 128, SSM_N), F32),
                   jax.ShapeDtypeStruct((b, CONV_K - 1, CONV_CH), F32)),
        grid=(b, nc),
        in_specs=[seq(CONV_CH, xbc_blk), seq(SSM_INNER, z_blk), seq(128, 0),
                  state((N_PAIRS, 128, SSM_N)), state((CONV_K - 1, CONV_CH)),
                  whole((CONV_K, CONV_CH)), whole((1, CONV_CH)), whole((1, 128)),
                  whole((1, SSM_INNER)), whole((1, SSM_INNER))],
        out_specs=(seq(SSM_INNER, 0), per_b((N_PAIRS, 128, SSM_N)), per_b((CONV_K - 1, CONV_CH))),
        scratch_shapes=[pltpu.VMEM((SSM_L + 8, CONV_CH), F32), pltpu.VMEM((SSM_L, CONV_CH), F32),
                        pltpu.VMEM((SSM_L, 128), F32), pltpu.VMEM((tv, SSM_INNER), F32),
                        pltpu.VMEM((N_PAIRS, 128, SSM_N), F32)],
        compiler_params=_params(2),
        name="conv_ssd",
    )(xbc_src, z_src, dt3, h0, c0, cw, cb, alog, dskip, gout)


def _merge_kernel(x_ref, gmix_ref, wg_ref, oa_ref, ys_ref, oc_ref, wa_ref, ws_ref, wc_ref, wo_ref, o_ref):
    x = x_ref[...]
    h = _rms(x, gmix_ref[...]).astype(BF16)
    branches = ((oa_ref, wa_ref), (ys_ref, ws_ref), (oc_ref, wc_ref))
    merged = None
    for b, (act_ref, w_ref) in enumerate(branches):
        gate = _sigmoid(_dot_nt(h, wg_ref[b * D:(b + 1) * D, :]))
        term = gate * _dot(act_ref[...].astype(BF16), w_ref[...])
        merged = term if merged is None else merged + term
    o_ref[...] = x + _dot(merged.astype(BF16), wo_ref[...])


def _merge(x, gmix, wg, oa, ys, oc, wa, ws, wc, wo):
    n = x.shape[0]
    tm = min(512, n)
    row = lambda width: pl.BlockSpec((tm, width), lambda i: (i, 0))
    whole = lambda shape: pl.BlockSpec(shape, lambda i: (0, 0), pipeline_mode=pl.Buffered(1))
    return pl.pallas_call(
        _merge_kernel,
        out_shape=jax.ShapeDtypeStruct((n, D), F32),
        grid=(n // tm,),
        in_specs=[row(D), whole((1, D)), whole((3 * D, D)), row(D), row(SSM_INNER), row(D),
                  whole((D, D)), whole((SSM_INNER, D)), whole((D, D)), whole((D, D))],
        out_specs=row(D),
        compiler_params=_params(1),
        name="branch_merge",
    )(x, gmix, wg, oa, ys, oc, wa, ws, wc, wo)


def _mlp_kernel(x_ref, g_ref, wu_ref, wd_ref, o_ref, h_scr, acc):
    f = pl.program_id(1)

    @pl.when(f == 0)
    def _():
        x = x_ref[...]
        h_scr[...] = _rms(x, g_ref[...]).astype(BF16)
        acc[...] = x

    a = jnp.maximum(_dot(h_scr[...], wu_ref[...]), 0.0)
    acc[...] += _dot((a * a).astype(BF16), wd_ref[...])

    @pl.when(f == pl.num_programs(1) - 1)
    def _():
        o_ref[...] = acc[...]


def _mlp(x, g, wu, wd):
    n = x.shape[0]
    tm = min(1024, n)
    tf = 1024
    return pl.pallas_call(
        _mlp_kernel,
        out_shape=jax.ShapeDtypeStruct((n, D), F32),
        grid=(n // tm, D_FF // tf),
        in_specs=[pl.BlockSpec((tm, D), lambda i, f: (i, 0)),
                  pl.BlockSpec((1, D), lambda i, f: (0, 0)),
                  pl.BlockSpec((D, tf), lambda i, f: (0, f)),
                  pl.BlockSpec((tf, D), lambda i, f: (f, 0))],
        out_specs=pl.BlockSpec((tm, D), lambda i, f: (i, 0)),
        scratch_shapes=[pltpu.VMEM((tm, D), BF16), pltpu.VMEM((tm, D), F32)],
        compiler_params=_params(2),
        name="relu2_mlp",
    )(x, g, wu, wd)


def _lambda_init(layer):
    return 0.8 - 0.6 * math.exp(-0.3 * layer)


def _layer_weights(l, p):
    w_in_t = jnp.swapaxes(p['w_in'], 1, 2)[l]
    cols = lambda off, width: w_in_t[off:off + width]
    w_attn = jnp.concatenate([cols(OFF_Q, D), cols(OFF_CQ, D), cols(OFF_K, D), cols(OFF_V, D)], axis=0).astype(BF16)
    w_ssm = jnp.concatenate([cols(OFF_XBC, CONV_CH), cols(OFF_Z, SSM_INNER)], axis=0).astype(BF16)
    w_gate = cols(OFF_G, 3 * D).astype(BF16)
    w_dt = jnp.pad(cols(OFF_DT, SSM_HEADS), ((0, 128 - SSM_HEADS), (0, 0))).astype(BF16)
    row = lambda v: v.reshape(1, -1).astype(F32)
    return dict(
        g_mix=row(p['g_mix'][l]), w_attn=w_attn, w_ssm=w_ssm, w_dt=w_dt, w_gate=w_gate,
        dt_bias=jnp.pad(row(p['dt_bias'][l]), ((0, 0), (0, 128 - SSM_HEADS))),
        gq=row(jnp.tile(p['g_da_q'][l], TN // DK)), gk=row(jnp.tile(p['g_da_k'][l], TN // DK)),
        gcq=row(p['g_ca_q'][l]),
        lam=jnp.stack([p['lambda_q1'][l], p['lambda_k1'][l], p['lambda_q2'][l], p['lambda_k2'][l]]).astype(F32),
        g_da_out=row(p['g_da_out'][l]),
        conv_w=p['conv_w'][l].astype(F32), conv_b=row(p['conv_b'][l]),
        a_log=jnp.pad(row(p['a_log'][l]), ((0, 0), (0, 128 - SSM_HEADS))),
        d_skip=row(jnp.repeat(p['d_skip'][l], SSM_P)), g_ssm_out=row(p['g_ssm_out'][l]),
        w_br_attn=p['w_br_attn'][l].astype(BF16), w_br_ssm=p['w_br_ssm'][l].astype(BF16),
        w_br_cross=p['w_br_cross'][l].astype(BF16), w_out=p['w_out'][l].astype(BF16),
        g_mlp=row(p['g_mlp'][l]), w_up=p['w_up'][l].astype(BF16), w_down=p['w_down'][l].astype(BF16),
        g_mem=row(p['g_mem'][l]), w_mem_kv=p['w_mem_kv'][l].astype(BF16), g_ca_k=row(p['g_ca_k'][l]),
    )


def _seg64_matrix():
    i = jnp.arange(SEG_W) // DK
    return jnp.where(i[:, None] == i[None, :], 1.0 / DK, 0.0).astype(BF16)


def _layer(x, bsz, w, s64, attn_fn, mem_k, mem_v, h0, c0, state_layer, act_dtype):
    n = x.shape[0]
    t = n // bsz
    ub, k32, v32, uf, dt = _proj(x, w['g_mix'], w['w_attn'], w['w_ssm'], w['w_dt'], w['dt_bias'],
                                 w['gq'], w['gk'], w['gcq'], s64, t if act_dtype == BF16 else None)
    ub3 = ub.reshape(bsz, t, -1)
    o_a = attn_fn(ub3, k32, v32)
    uf3 = uf.reshape(bsz, t, -1)
    y_n, h_new, c_new = _ssd(uf3, 0, uf3, ((PB_Z - PB_XBC) * TN) // SSM_INNER, dt.reshape(bsz, t, 128), h0, c0,
                             state_layer, w['conv_w'], w['conv_b'], w['a_log'], w['d_skip'], w['g_ssm_out'], act_dtype)
    if act_dtype == BF16:
        cq3, cq_blk = ub3, (PB_CQ * TN) // D
    else:
        cq3, cq_blk = ub3[:, :, PB_CQ * TN:PB_K * TN].astype(F32), 0
    o_c = _cross(cq3, cq_blk, mem_k, mem_v, act_dtype)
    x = _merge(x, w['g_mix'], w['w_gate'], o_a.reshape(n, D), y_n.reshape(n, SSM_INNER), o_c.reshape(n, D),
               w['w_br_attn'], w['w_br_ssm'], w['w_br_cross'], w['w_out'])
    x = _mlp(x, w['g_mlp'], w['w_up'], w['w_down'])
    return x, k32, v32, h_new, c_new


def kernel(x_prompt, x_sample, cache_k, cache_v, cache_mem_k, cache_mem_v, state_ssm, state_conv, page_table, mem_prompt, rel_bias_table, g_mix, w_in, g_da_q, g_da_k, lambda_q1, lambda_k1, lambda_q2, lambda_k2, g_da_out, w_br_attn, conv_w, conv_b, dt_bias, a_log, d_skip, g_ssm_out, w_br_ssm, g_mem, w_mem_kv, g_ca_q, g_ca_k, w_br_cross, w_out, g_mlp, w_up, w_down):
    p = dict(g_mix=g_mix, w_in=w_in, g_da_q=g_da_q, g_da_k=g_da_k, lambda_q1=lambda_q1, lambda_k1=lambda_k1,
             lambda_q2=lambda_q2, lambda_k2=lambda_k2, g_da_out=g_da_out, w_br_attn=w_br_attn, conv_w=conv_w,
             conv_b=conv_b, dt_bias=dt_bias, a_log=a_log, d_skip=d_skip, g_ssm_out=g_ssm_out, w_br_ssm=w_br_ssm,
             g_mem=g_mem, w_mem_kv=w_mem_kv, g_ca_q=g_ca_q, g_ca_k=g_ca_k, w_br_cross=w_br_cross, w_out=w_out,
             g_mlp=g_mlp, w_up=w_up, w_down=w_down)
    depth = w_in.shape[0]
    bp, tp, _ = x_prompt.shape
    bs, ts, _ = x_sample.shape
    n_phys = cache_k.shape[1]
    past_len = page_table.shape[1] * PAGE
    tq = min(512, tp)
    table = rel_bias_table.astype(F32)
    s64 = _seg64_matrix()

    bias_diag, bias_near = _bias_tiles(table, 0, (tq, tq)), _bias_tiles(table, tq, (tq, tq))
    as_heads = lambda a: a.reshape(N_HEADS * 2 * ts, PAGE)
    bias_past = as_heads(_bias_tiles(table, PAGE, (ts, PAGE)))
    bias_new = as_heads(_bias_tiles(table, 0, (ts, PAGE)))

    cache_k4 = jnp.transpose(cache_k, (0, 1, 3, 4, 2)).reshape(depth, n_phys, D, PAGE)
    cache_v4 = cache_v.reshape(depth, n_phys, PAGE * N_HEADS, DV)
    mem_flat = mem_prompt.reshape(bp * MEM_LEN, D)
    zero_h = jnp.zeros((1, bp, N_PAIRS, 128, SSM_N), F32)
    zero_c = jnp.zeros((1, bp, CONV_K - 1, CONV_CH), F32)
    state_ssm5 = state_ssm.reshape(depth, bs, N_PAIRS, 128, SSM_N)

    xp = x_prompt.reshape(bp * tp, D)
    xs = x_sample.reshape(bs * ts, D)
    outs = {k: [] for k in ('kp', 'vp', 'mk', 'mv', 'hp', 'cp', 'ks', 'vs', 'hs', 'cs')}
    for l in range(depth):
        w = _layer_weights(l, p)
        lam0 = _lambda_init(l)

        mk, mv = _memkv(mem_flat, w['g_mem'], w['w_mem_kv'], w['g_ca_k'])
        attn_p = lambda ub3, k32, v32: _attn_prompt(ub3, w['lam'], bias_diag, bias_near, w['g_da_out'],
                                                    lam0=lam0, tq=tq)
        xp, k, v, hn, cn = _layer(xp, bp, w, s64, attn_p, mk.reshape(bp, MEM_LEN, D), mv.reshape(bp, MEM_LEN, D),
                                  zero_h, zero_c, 0, BF16)
        for key, val in zip(('kp', 'vp', 'mk', 'mv', 'hp', 'cp'), (k, v, mk, mv, hn, cn)):
            outs[key].append(val)

        def attn_s(ub3, k32, v32):
            q3 = ub3[:, :, 0:D].astype(F32)
            return _attn_sample(page_table, q3, k32.reshape(bs, ts, D), v32.reshape(bs, ts, D), cache_k4, cache_v4,
                                l, bias_past, bias_new, w['lam'], w['g_da_out'], lam0=lam0)

        xs, k, v, hn, cn = _layer(xs, bs, w, s64, attn_s, cache_mem_k[l].reshape(bs, MEM_LEN, D),
                                  cache_mem_v[l].reshape(bs, MEM_LEN, D), state_ssm5, state_conv, l, F32)
        for key, val in zip(('ks', 'vs', 'hs', 'cs'), (k, v, hn, cn)):
            outs[key].append(val)

    st = lambda key, shape: jnp.stack(outs[key]).reshape((depth,) + shape)
    return (xp.reshape(bp, tp, D), xs.reshape(bs, ts, D),
            jnp.transpose(st('kp', (bp, N_MAPS, DK, tp)), (0, 1, 4, 2, 3)), st('vp', (bp, tp, N_HEADS, DV)),
            st('mk', (bp, MEM_LEN, CA_HEADS, CA_D)), st('mv', (bp, MEM_LEN, CA_HEADS, CA_D)),
            st('hp', (bp, SSM_HEADS, SSM_P, SSM_N)), st('cp', (bp, CONV_K - 1, CONV_CH)),
            st('ks', (bs, ts, N_MAPS, DK)), st('vs', (bs, ts, N_HEADS, DV)),
            st('hs', (bs, SSM_HEADS, SSM_P, SSM_N)), st('cs', (bs, CONV_K - 1, CONV_CH)))
```

```python
import functools
import math

import jax
import jax.numpy as jnp
from jax import lax
from jax.experimental import pallas as pl
from jax.experimental.pallas import tpu as pltpu

F32, BF16 = jnp.float32, jnp.bfloat16
EPS = 1e-5
NEG = -1e30
LOG2E = math.log2(math.e)

D = 1024
N_MAPS, DK = 16, 64
N_HEADS, DV = 8, 128
SSM_INNER, SSM_P, SSM_N, SSM_G = 2048, 64, 128, 8
SSM_HEADS = SSM_INNER // SSM_P
N_PAIRS = SSM_HEADS // 2
CONV_K, CONV_CH = 4, 4096
MEM_LEN, CA_HEADS, CA_D = 256, 4, 256
D_FF = 4096
PAGE = 128
REL_BUCKETS, REL_MAX_DIST = 32, 128
SSM_L = 128

OFF_Q, OFF_K, OFF_V, OFF_Z, OFF_XBC = 0, 1024, 2048, 3072, 5120
OFF_DT = OFF_XBC + CONV_CH
OFF_CQ = OFF_DT + SSM_HEADS
OFF_G = OFF_CQ + D

TN = 1024
SEG_W = 256
PB_Q, PB_CQ, PB_K, PB_V, PB_XBC, PB_Z, PB_END = 0, 1, 2, 3, 4, 8, 10
VMEM_LIMIT = 48 << 20


def _params(n_axes, vmem=VMEM_LIMIT):
    return pltpu.CompilerParams(dimension_semantics=("arbitrary",) * n_axes, vmem_limit_bytes=vmem)


def _rms(x, gain):
    ms = jnp.mean(x * x, axis=-1, keepdims=True)
    return x * lax.rsqrt(ms + EPS) * gain


def _sigmoid(x):
    return 1.0 / (1.0 + jnp.exp(-x))


def _softplus(x):
    return jnp.maximum(x, 0.0) + jnp.log1p(jnp.exp(-jnp.abs(x)))


def _dot(a, b):
    return jnp.dot(a, b, preferred_element_type=F32)


def _dot_nt(a, b):
    return lax.dot_general(a, b, (((1,), (1,)), ((), ())), preferred_element_type=F32)


def _dot_tn(a, b):
    return lax.dot_general(a, b, (((0,), (0,)), ((), ())), preferred_element_type=F32)


def _dot_split3(a01, x):
    hi = x.astype(BF16)
    r = x - hi.astype(F32)
    mid = r.astype(BF16)
    lo = (r - mid.astype(F32)).astype(BF16)
    return _dot(a01, hi) + _dot(a01, mid) + _dot(a01, lo)


def _bucket_thresholds():
    max_exact = REL_BUCKETS // 2
    ratio = REL_MAX_DIST / max_exact
    return [math.ceil(max_exact * ratio ** (k / (REL_BUCKETS - max_exact))) for k in range(1, REL_BUCKETS - max_exact)]


def _bias_kernel(tab_ref, o_ref, *, off, shape):
    m = pl.program_id(0)
    r = lax.broadcasted_iota(jnp.int32, shape, 0)
    c = lax.broadcasted_iota(jnp.int32, shape, 1)
    n = off + r - c
    nn = jnp.maximum(n, 0)
    max_exact = REL_BUCKETS // 2
    large = jnp.full(shape, max_exact, jnp.int32)
    for t in _bucket_thresholds():
        large = large + (nn >= t).astype(jnp.int32)
    bucket = jnp.where(nn < max_exact, nn, large)
    val = jnp.zeros(shape, F32)
    for b in range(REL_BUCKETS):
        val = jnp.where(bucket == b, tab_ref[b, m], val)
    val = (val - tab_ref[REL_BUCKETS - 1, m]) * LOG2E
    o_ref[0] = jnp.where(n >= 0, val, NEG)


def _bias_tiles(table, off, shape):
    return pl.pallas_call(
        functools.partial(_bias_kernel, off=off, shape=shape),
        out_shape=jax.ShapeDtypeStruct((N_MAPS,) + shape, F32),
        grid=(N_MAPS,),
        in_specs=[pl.BlockSpec(memory_space=pltpu.SMEM)],
        out_specs=pl.BlockSpec((1,) + shape, lambda m: (m, 0, 0)),
        compiler_params=_params(1),
        name="rel_bias_tiles",
    )(table)


def _proj_attn_kernel(x_ref, g_ref, w_ref, wdt_ref, dtb_ref, gq_ref, gk_ref, gcq_ref, s64_ref,
                      ub_ref, k32_ref, v32_ref, dt_ref):
    h = _rms(x_ref[...], g_ref[...]).astype(BF16)
    dt_ref[...] = _softplus(_dot_nt(h, wdt_ref[...]) + dtb_ref[...])

    def proj(blk):
        return _dot_nt(h, w_ref[blk * TN:(blk + 1) * TN, :])

    def seg64_norm(u, gain):
        sq = (u * u).astype(BF16)
        ms = jnp.concatenate([_dot(sq[:, c * SEG_W:(c + 1) * SEG_W], s64_ref[...]) for c in range(TN // SEG_W)],
                             axis=1)
        return u * lax.rsqrt(ms + EPS) * gain

    def out(blk):
        return slice(blk * TN, (blk + 1) * TN)

    ub_ref[:, out(PB_Q)] = (seg64_norm(proj(PB_Q), gq_ref[...]) * (DK ** -0.5 * LOG2E)).astype(BF16)
    u = proj(PB_CQ)
    for s in range(TN // CA_D):
        seg = u[:, s * CA_D:(s + 1) * CA_D]
        ub_ref[:, PB_CQ * TN + s * CA_D:PB_CQ * TN + (s + 1) * CA_D] = (
            _rms(seg, gcq_ref[...]) * (CA_D ** -0.5)).astype(BF16)
    kn = seg64_norm(proj(PB_K), gk_ref[...])
    k32_ref[...] = kn if k32_ref.shape == kn.shape else kn.T
    ub_ref[:, out(PB_K)] = kn.astype(BF16)
    u = proj(PB_V)
    v32_ref[...] = u
    ub_ref[:, out(PB_V)] = u.astype(BF16)


def _proj_ssm_kernel(x_ref, g_ref, w_ref, uf_ref):
    h = _rms(x_ref[...], g_ref[...]).astype(BF16)
    for blk in range(uf_ref.shape[1] // TN):
        uf_ref[:, blk * TN:(blk + 1) * TN] = _dot_nt(h, w_ref[blk * TN:(blk + 1) * TN, :])


def _proj(x, g, w_attn, w_ssm, wdt, dtb, gq, gk, gcq, s64, k_seq_len):
    n = x.shape[0]
    tm = min(512, n)
    if k_seq_len is None:
        k_shape, k_spec = (n, D), pl.BlockSpec((tm, D), lambda i: (i, 0))
    else:
        per_seq = k_seq_len // tm
        k_shape = (n // k_seq_len, D, k_seq_len)
        k_spec = pl.BlockSpec((None, D, tm), lambda i: (i // per_seq, 0, i % per_seq))
    row = lambda width: pl.BlockSpec((tm, width), lambda i: (i, 0))
    whole = lambda shape: pl.BlockSpec(shape, lambda i: (0, 0), pipeline_mode=pl.Buffered(1))
    ub, k32, v32, dt = pl.pallas_call(
        _proj_attn_kernel,
        out_shape=(jax.ShapeDtypeStruct((n, PB_XBC * TN), BF16),
                   jax.ShapeDtypeStruct(k_shape, F32),
                   jax.ShapeDtypeStruct((n, D), F32),
                   jax.ShapeDtypeStruct((n, 128), F32)),
        grid=(n // tm,),
        in_specs=[row(D), whole((1, D)), whole((PB_XBC * TN, D)), whole((128, D)), whole((1, 128)),
                  whole((1, TN)), whole((1, TN)), whole((1, CA_D)), whole((SEG_W, SEG_W))],
        out_specs=(row(PB_XBC * TN), k_spec, row(D), row(128)),
        compiler_params=_params(1),
        name="in_proj_attn",
    )(x, g, w_attn, wdt, dtb, gq, gk, gcq, s64)
    ssm_w = (PB_END - PB_XBC) * TN
    uf = pl.pallas_call(
        _proj_ssm_kernel,
        out_shape=jax.ShapeDtypeStruct((n, ssm_w), F32),
        grid=(n // tm,),
        in_specs=[row(D), whole((1, D)), whole((ssm_w, D))],
        out_specs=row(ssm_w),
        compiler_params=_params(1, 56 << 20),
        name="in_proj_ssm",
    )(x, g, w_ssm)
    return ub, k32, v32, uf, dt


def _memkv_kernel(m_ref, g_ref, w_ref, gk_ref, mk_ref, mv_ref, h_scr):
    j = pl.program_id(1)

    @pl.when(j == 0)
    def _():
        h_scr[...] = _rms(m_ref[...], g_ref[...]).astype(BF16)

    u = _dot(h_scr[...], w_ref[...])

    @pl.when(j == 0)
    def _():
        for s in range(CA_HEADS):
            mk_ref[:, s * CA_D:(s + 1) * CA_D] = _rms(u[:, s * CA_D:(s + 1) * CA_D], gk_ref[...])

    @pl.when(j == 1)
    def _():
        mv_ref[...] = u


def _memkv(mem, g, w, gk):
    n = mem.shape[0]
    tm = min(512, n)
    return pl.pallas_call(
        _memkv_kernel,
        out_shape=(jax.ShapeDtypeStruct((n, D), F32), jax.ShapeDtypeStruct((n, D), F32)),
        grid=(n // tm, 2),
        in_specs=[pl.BlockSpec((tm, D), lambda i, j: (i, 0)),
                  pl.BlockSpec((1, D), lambda i, j: (0, 0)),
                  pl.BlockSpec((D, D), lambda i, j: (0, j)),
                  pl.BlockSpec((1, CA_D), lambda i, j: (0, 0))],
        out_specs=(pl.BlockSpec((tm, D), lambda i, j: (i, 0)), pl.BlockSpec((tm, D), lambda i, j: (i, 0))),
        scratch_shapes=[pltpu.VMEM((tm, D), BF16)],
        compiler_params=_params(2),
        name="mem_kv",
    )(mem, g, w, gk)


def _lambda(lam_ref, lam0):
    lv = lam_ref[...]
    s1 = jnp.sum(lv[0:1] * lv[1:2], axis=-1, keepdims=True)
    s2 = jnp.sum(lv[2:3] * lv[3:4], axis=-1, keepdims=True)
    return jnp.exp(s1) - jnp.exp(s2) + lam0


def _softmax_step(state, s, v_bf16):
    m, l, acc = state
    m_new = jnp.maximum(m, jnp.max(s, axis=-1, keepdims=True))
    alpha = jnp.exp2(m - m_new)
    p = jnp.exp2(s - jnp.tile(m_new, (1, s.shape[1] // m_new.shape[1])))
    l = alpha * l + jnp.sum(p, axis=-1, keepdims=True)
    acc = alpha * acc + _dot(p.astype(BF16), v_bf16)
    return m_new, l, acc


def _attn_prompt_kernel(lam_ref, q_ref, k_ref, v_ref, bdiag_ref, bnear_ref, gout_ref, o_ref, m_ref, l_ref, acc_ref,
                        *, tq, lam0):
    qi = pl.program_id(2)
    q = q_ref[...]
    lane = lax.broadcasted_iota(jnp.int32, q.shape, 1)
    qm = jnp.concatenate([jnp.where(lane < DK, q, jnp.zeros_like(q)), jnp.where(lane >= DK, q, jnp.zeros_like(q))],
                         axis=0)
    m_ref[...] = jnp.full(m_ref.shape, NEG, F32)
    l_ref[...] = jnp.zeros(l_ref.shape, F32)
    acc_ref[...] = jnp.zeros(acc_ref.shape, F32)

    def update(kb, bias_ref):
        start = pl.multiple_of(kb * tq, tq)
        s = _dot_nt(qm, k_ref[pl.ds(start, tq), :])
        if bias_ref is not None:
            s = s + bias_ref[...]
        m_new, l_new, acc_new = _softmax_step((m_ref[...], l_ref[...], acc_ref[...]), s, v_ref[pl.ds(start, tq), :])
        m_ref[...] = m_new
        l_ref[...] = l_new
        acc_ref[...] = acc_new

    def loop(lo, hi, bias_ref):
        def body(kb, carry):
            update(kb, bias_ref)
            return carry
        lax.fori_loop(lo, hi, body, 0)

    loop(0, qi - 1, None)
    loop(jnp.maximum(qi - 1, 0), qi, bnear_ref)
    update(qi, bdiag_ref)

    lam = _lambda(lam_ref, lam0)
    o_maps = acc_ref[...] / l_ref[...]
    o = o_maps[0:tq] - lam * o_maps[tq:2 * tq]
    o_ref[...] = (_rms(o, gout_ref[...]) * (1.0 - lam0)).astype(o_ref.dtype)


def _attn_prompt(ub3, lam_vecs, bias_diag, bias_near, gout, *, lam0, tq):
    b, t, _ = ub3.shape
    kcol, vcol = (PB_K * TN) // DV, (PB_V * TN) // DV
    bias_spec = pl.BlockSpec((2 * tq, tq), lambda h, b, i: (h, 0))
    bias_diag, bias_near = (a.reshape(N_MAPS * tq, tq) for a in (bias_diag, bias_near))
    return pl.pallas_call(
        functools.partial(_attn_prompt_kernel, tq=tq, lam0=lam0),
        out_shape=jax.ShapeDtypeStruct((b, t, D), BF16),
        grid=(N_HEADS, b, t // tq),
        in_specs=[pl.BlockSpec((4, DK), lambda h, b, i: (0, 0)),
                  pl.BlockSpec((None, tq, DV), lambda h, b, i: (b, i, h)),
                  pl.BlockSpec((None, t, DV), lambda h, b, i: (b, 0, kcol + h)),
                  pl.BlockSpec((None, t, DV), lambda h, b, i: (b, 0, vcol + h)),
                  bias_spec, bias_spec,
                  pl.BlockSpec((1, DV), lambda h, b, i: (0, 0))],
        out_specs=pl.BlockSpec((None, tq, DV), lambda h, b, i: (b, i, h)),
        scratch_shapes=[pltpu.VMEM((2 * tq, DV), F32)] * 3,
        compiler_params=_params(3),
        name="diff_attn_prompt",
    )(lam_vecs, ub3, ub3, ub3, bias_diag, bias_near, gout)


def _head_queries(q, h):
    qh = q[:, h * DV:(h + 1) * DV]
    lane = lax.broadcasted_iota(jnp.int32, qh.shape, 1)
    return jnp.concatenate([jnp.where(lane < DK, qh, 0.0), jnp.where(lane >= DK, qh, 0.0)], axis=0).astype(BF16)


def _attn_sample_kernel(pt_ref, q_ref, kn_ref, vn_ref, bpast_ref, bnew_ref, lam_ref, gout_ref, *rest, npg, lam0):
    k_refs, v_refs = rest[:npg], rest[npg:2 * npg]
    o_ref, m_scr, l_scr, acc_scr = rest[2 * npg:]
    c = pl.program_id(1)
    last = c == pl.num_programs(1) - 1
    tdec = q_ref.shape[0]
    rows = 2 * tdec

    @pl.when(c == 0)
    def _():
        m_scr[...] = jnp.full(m_scr.shape, NEG, F32)
        l_scr[...] = jnp.zeros(l_scr.shape, F32)
        acc_scr[...] = jnp.zeros(acc_scr.shape, F32)

    q = q_ref[...]
    qms = [_head_queries(q, h) for h in range(N_HEADS)]

    def update(s, pv_fn):
        m_old = m_scr[...]
        m_new = jnp.maximum(m_old, jnp.max(s, axis=-1, keepdims=True))
        alpha = jnp.exp2(m_old - m_new)
        p = jnp.exp2(s - m_new[:, 0:1])
        l_scr[...] = alpha * l_scr[...] + jnp.sum(p, axis=-1, keepdims=True)
        acc_scr[...] = alpha * acc_scr[...] + pv_fn(p.astype(BF16))
        m_scr[...] = m_new

    near = last.astype(F32)
    s_pages = []
    for i in range(npg):
        s_i = jnp.concatenate([_dot(qms[h], k_refs[i][h * DV:(h + 1) * DV, :].astype(BF16))
                               for h in range(N_HEADS)], axis=0)
        if i == npg - 1:
            s_i = s_i + near * bpast_ref[...]
        s_pages.append(s_i)

    def pv_pages(p):
        heads = []
        for h in range(N_HEADS):
            out = None
            for i in range(npg):
                v_h = v_refs[i][pl.ds(h, PAGE, stride=N_HEADS), :].astype(BF16)
                part = _dot(p[h * rows:(h + 1) * rows, i * PAGE:(i + 1) * PAGE], v_h)
                out = part if out is None else out + part
            heads.append(out)
        return jnp.concatenate(heads, axis=0)

    update(jnp.concatenate(s_pages, axis=1), pv_pages)

    @pl.when(last)
    def _():
        pad = jnp.zeros((PAGE - tdec, D), F32)
        kn = jnp.concatenate([kn_ref[...], pad], axis=0).astype(BF16)
        vn = jnp.concatenate([vn_ref[...], pad], axis=0).astype(BF16)
        s_new = jnp.concatenate([_dot_nt(qms[h], kn[:, h * DV:(h + 1) * DV]) for h in range(N_HEADS)], axis=0)
        update(s_new + bnew_ref[...],
               lambda p: jnp.concatenate([_dot(p[h * rows:(h + 1) * rows], vn[:, h * DV:(h + 1) * DV])
                                          for h in range(N_HEADS)], axis=0))
        lam = _lambda(lam_ref, lam0)
        o_all = acc_scr[...] / l_scr[...]
        for h in range(N_HEADS):
            o = o_all[h * rows:h * rows + tdec] - lam * o_all[h * rows + tdec:(h + 1) * rows]
            o_ref[:, h * DV:(h + 1) * DV] = _rms(o, gout_ref[...]) * (1.0 - lam0)


def _attn_sample(page_table, q3, kn3, vn3, cache_k4, cache_v4, layer, bpast, bnew, lam_vecs, gout, *, lam0):
    bs, tdec, _ = q3.shape
    n_pages = page_table.shape[1]
    npg = math.gcd(n_pages, 16)
    pt = page_table.reshape(-1)

    def page_spec(i):
        return pl.BlockSpec((None, None, D, PAGE),
                            lambda b, c, pt_ref: (layer, pt_ref[b * n_pages + c * npg + i], 0, 0))

    per_b = pl.BlockSpec((None, tdec, D), lambda b, c, pt_ref: (b, 0, 0))
    whole = lambda shape: pl.BlockSpec(shape, lambda b, c, pt_ref: (0,) * len(shape))
    stat = pltpu.VMEM((N_HEADS * 2 * tdec, DV), F32)
    return pl.pallas_call(
        functools.partial(_attn_sample_kernel, npg=npg, lam0=lam0),
        out_shape=jax.ShapeDtypeStruct((bs, tdec, D), F32),
        grid_spec=pltpu.PrefetchScalarGridSpec(
            num_scalar_prefetch=1,
            grid=(bs, n_pages // npg),
            in_specs=[per_b, per_b, per_b,
                      whole((N_HEADS * 2 * tdec, PAGE)), whole((N_HEADS * 2 * tdec, PAGE)),
                      whole((4, DK)), whole((1, DV))]
                     + [page_spec(i) for i in range(npg)] + [page_spec(i) for i in range(npg)],
            out_specs=per_b,
            scratch_shapes=[stat, stat, stat]),
        compiler_params=_params(2),
        name="diff_attn_sample",
    )(pt, q3, kn3, vn3, bpast, bnew, lam_vecs, gout, *([cache_k4] * npg), *([cache_v4] * npg))


def _cross_kernel(cq_ref, mk_ref, mv_ref, o_ref):
    rows = cq_ref.shape[0]
    for h in range(CA_HEADS):
        cols = slice(h * CA_D, (h + 1) * CA_D)
        q = cq_ref[:, cols]
        if rows % 16:
            q = jnp.concatenate([q.astype(F32), jnp.zeros((16 - rows % 16, CA_D), F32)], axis=0)
        s = _dot_nt(q.astype(BF16), mk_ref[:, cols].astype(BF16))
        p = jnp.exp(s - jnp.max(s, axis=-1, keepdims=True))
        o = _dot(p.astype(BF16), mv_ref[:, cols].astype(BF16)) / jnp.sum(p, axis=-1, keepdims=True)
        o_ref[:, cols] = o[0:rows].astype(o_ref.dtype)


def _cross(cq3, col_block, mem_k, mem_v, out_dtype):
    b, t, _ = cq3.shape
    tm = min(512, t)
    mem_spec = pl.BlockSpec((None, MEM_LEN, D), lambda b, i: (b, 0, 0))
    return pl.pallas_call(
        _cross_kernel,
        out_shape=jax.ShapeDtypeStruct((b, t, D), out_dtype),
        grid=(b, t // tm),
        in_specs=[pl.BlockSpec((None, tm, D), lambda b, i: (b, i, col_block)), mem_spec, mem_spec],
        out_specs=pl.BlockSpec((None, tm, D), lambda b, i: (b, i, 0)),
        compiler_params=_params(2),
        name="mem_cross_attn",
    )(cq3, mem_k, mem_v)


def _conv_silu(window, cw, cb):
    acc = cb + cw[CONV_K - 1:CONV_K] * window(CONV_K - 1)
    for k in range(CONV_K - 1):
        acc = acc + cw[k:k + 1] * window(k)
    return acc * _sigmoid(acc)


def _ssd_kernel(xbc_ref, z_ref, dt_ref, h0_ref, c0_ref, cw_ref, cb_ref, alog_ref, dskip_ref, gout_ref,
                y_ref, hn_ref, cn_ref, buf, xc_scr, dt_scr, yb, h_scr, *, tv):
    L = SSM_L
    c = pl.program_id(1)
    head = 8

    @pl.when(c == 0)
    def _():
        buf[0:head, :] = jnp.zeros((head, CONV_CH), F32)
        buf[head - (CONV_K - 1):head, :] = c0_ref[...]
        h_scr[...] = h0_ref[...]
        if tv < L:
            buf[head + tv:, :] = jnp.zeros((L - tv, CONV_CH), F32)
            dt_scr[...] = jnp.zeros(dt_scr.shape, F32)

    buf[head:head + tv, :] = xbc_ref[...]
    cn_ref[...] = buf[head + tv - (CONV_K - 1):head + tv, :]

    first_tap = head - (CONV_K - 1)
    for j in range(CONV_CH // 512):
        cols = slice(j * 512, (j + 1) * 512)
        xc_scr[:, cols] = _conv_silu(lambda k: buf[first_tap + k:first_tap + k + L, cols],
                                     cw_ref[:, cols], cb_ref[:, cols])

    if tv == L:
        buf[0:head, :] = buf[L:L + head, :]
        dt = dt_ref[...]
    else:
        dt_scr[0:tv, :] = dt_ref[...]
        dt = dt_scr[...]

    def zs(cols):
        z = z_ref[:, cols]
        return z * _sigmoid(z)

    _ssd_scan(lambda cols: xc_scr[:, cols], zs, dt, alog_ref, dskip_ref, gout_ref, y_ref, hn_ref, yb, h_scr, tv)


def _ssd_scan(xc, zs, dt, alog_ref, dskip_ref, gout_ref, y_ref, hn_ref, yb, h_scr, tv):
    L = SSM_L
    a_neg = -jnp.exp(alog_ref[...])
    d_a = dt * a_neg
    row = lax.broadcasted_iota(jnp.int32, (L, L), 0)
    col = lax.broadcasted_iota(jnp.int32, (L, L), 1)
    causal = col <= row
    a_cum = _dot_split3(causal.astype(BF16), d_a)
    a_cum_t = a_cum.T
    dt_t = dt.T
    a_last = a_cum[L - 1:L, :]
    e_cum = jnp.exp(a_cum)
    to_end = jnp.exp(a_last - a_cum) * dt
    e_last = jnp.exp(a_last)
    lane = lax.broadcasted_iota(jnp.int32, (L, 128), 1)
    first = lane < SSM_P
    row_first = lax.broadcasted_iota(jnp.int32, (128, 128), 0) < SSM_P

    for g in range(SSM_G):
        b_g = xc(slice(SSM_INNER + g * SSM_N, SSM_INNER + (g + 1) * SSM_N)).astype(BF16)
        c_off = SSM_INNER + SSM_G * SSM_N
        c_g = xc(slice(c_off + g * SSM_N, c_off + (g + 1) * SSM_N)).astype(BF16)
        cb = _dot_nt(c_g, b_g)
        for jj in (2 * g, 2 * g + 1):
            cols = slice(jj * 128, (jj + 1) * 128)
            x_pair = xc(cols)
            x_bf = x_pair.astype(BF16)
            yw = []
            for hh in (2 * jj, 2 * jj + 1):
                seg = a_cum[:, hh:hh + 1] - a_cum_t[hh:hh + 1, :]
                w = cb * jnp.exp(jnp.where(causal, seg, NEG)) * dt_t[hh:hh + 1, :]
                yw.append(_dot(w.astype(BF16), x_bf))
            ha, hb = 2 * jj, 2 * jj + 1
            hp = h_scr[jj]
            y = jnp.where(first, yw[0], yw[1])
            y = y + _dot_nt(c_g, hp.astype(BF16)) * jnp.where(first, e_cum[:, ha:ha + 1], e_cum[:, hb:hb + 1])
            y = y + dskip_ref[:, cols] * x_pair
            te = jnp.where(first, to_end[:, ha:ha + 1], to_end[:, hb:hb + 1])
            upd = _dot_tn((x_pair * te).astype(BF16), b_g)
            h_scr[jj] = hp * jnp.where(row_first, e_last[:, ha:ha + 1], e_last[:, hb:hb + 1]) + upd
            yb[:, cols] = y[0:tv] * zs(cols)

    gw = SSM_INNER // SSM_G
    for g in range(SSM_G):
        cols = slice(g * gw, (g + 1) * gw)
        y_ref[:, cols] = _rms(yb[:, cols], gout_ref[:, cols]).astype(y_ref.dtype)
    hn_ref[...] = h_scr[...]


def _ssd(xbc_src, xbc_blk, z_src, z_blk, dt3, h0, c0, layer, cw, cb, alog, dskip, gout, out_dtype):
    b, t, _ = dt3.shape
    tv = min(SSM_L, t)
    nc = t // tv
    seq = lambda width, blk: pl.BlockSpec((None, tv, width), lambda b, c: (b, c, blk))
    per_b = lambda shape: pl.BlockSpec((None,) + shape, lambda b, c: (b,) + (0,) * len(shape))
    state = lambda shape: pl.BlockSpec((None, None) + shape, lambda b, c: (layer, b) + (0,) * len(shape))
    whole = lambda shape: pl.BlockSpec(shape, lambda b, c: (0,) * len(shape))
    y_shape = jax.ShapeDtypeStruct((b, t, SSM_INNER), out_dtype)
    h_shape = jax.ShapeDtypeStruct((b, N_PAIRS, 128, SSM_N), F32)
    tail = [whole((1, 128)), whole((1, SSM_INNER)), whole((1, SSM_INNER))]
    yb_h = [pltpu.VMEM((tv, SSM_INNER), F32), pltpu.VMEM((N_PAIRS, 128, SSM_N), F32)]
    return pl.pallas_call(
        functools.partial(_ssd_kernel, tv=tv),
        out_shape=(y_shape, h_shape, jax.ShapeDtypeStruct((b, CONV_K - 1, CONV_CH), F32)),
        grid=(b, nc),
        in_specs=[seq(CONV_CH, xbc_blk), seq(SSM_INNER, z_blk), seq(128, 0),
                  state((N_PAIRS, 128, SSM_N)), state((CONV_K - 1, CONV_CH)),
                  whole((CONV_K, CONV_CH)), whole((1, CONV_CH))] + tail,
        out_specs=(seq(SSM_INNER, 0), per_b((N_PAIRS, 128, SSM_N)), per_b((CONV_K - 1, CONV_CH))),
        scratch_shapes=[pltpu.VMEM((SSM_L + 8, CONV_CH), F32), pltpu.VMEM((SSM_L, CONV_CH), F32),
                        pltpu.VMEM((SSM_L, 128), F32)] + yb_h,
        compiler_params=_params(2),
        name="conv_ssd",
    )(xbc_src, z_src, dt3, h0, c0, cw, cb, alog, dskip, gout)


def _merge_kernel(x_ref, gmix_ref, wg_ref, oa_ref, ys_ref, oc_ref, wa_ref, ws_ref, wc_ref, wo_ref, o_ref):
    x = x_ref[...]
    h = _rms(x, gmix_ref[...]).astype(BF16)
    branches = ((oa_ref, wa_ref), (ys_ref, ws_ref), (oc_ref, wc_ref))
    merged = None
    for b, (act_ref, w_ref) in enumerate(branches):
        gate = _sigmoid(_dot_nt(h, wg_ref[b * D:(b + 1) * D, :]))
        term = gate * _dot(act_ref[...].astype(BF16), w_ref[...])
        merged = term if merged is None else merged + term
    o_ref[...] = x + _dot(merged.astype(BF16), wo_ref[...])


def _merge(x, gmix, wg, oa, ys, oc, wa, ws, wc, wo):
    n = x.shape[0]
    tm = min(512, n)
    row = lambda width: pl.BlockSpec((tm, width), lambda i: (i, 0))
    whole = lambda shape: pl.BlockSpec(shape, lambda i: (0, 0), pipeline_mode=pl.Buffered(1))
    return pl.pallas_call(
        _merge_kernel,
        out_shape=jax.ShapeDtypeStruct((n, D), F32),
        grid=(n // tm,),
        in_specs=[row(D), whole((1, D)), whole((3 * D, D)), row(D), row(SSM_INNER), row(D),
                  whole((D, D)), whole((SSM_INNER, D)), whole((D, D)), whole((D, D))],
        out_specs=row(D),
        compiler_params=_params(1),
        name="branch_merge",
    )(x, gmix, wg, oa, ys, oc, wa, ws, wc, wo)


MLP_TF = 1024


def _mlp_kernel(x_ref, g_ref, wu_ref, wd_ref, o_ref):
    x = x_ref[...]
    h = _rms(x, g_ref[...]).astype(BF16)
    acc = x
    for f in range(D_FF // MLP_TF):
        cols = slice(f * MLP_TF, (f + 1) * MLP_TF)
        a = jnp.maximum(_dot(h, wu_ref[:, cols]), 0.0)
        acc = acc + _dot((a * a).astype(BF16), wd_ref[cols, :])
    o_ref[...] = acc


def _mlp(x, g, wu, wd):
    n = x.shape[0]
    tm = min(512, n)
    row = pl.BlockSpec((tm, D), lambda i: (i, 0))
    whole = lambda shape: pl.BlockSpec(shape, lambda i: (0, 0), pipeline_mode=pl.Buffered(1))
    return pl.pallas_call(
        _mlp_kernel,
        out_shape=jax.ShapeDtypeStruct((n, D), F32),
        grid=(n // tm,),
        in_specs=[row, whole((1, D)), whole((D, D_FF)), whole((D_FF, D))],
        out_specs=row,
        compiler_params=_params(1),
        name="relu2_mlp",
    )(x, g, wu, wd)


def _lambda_init(layer):
    return 0.8 - 0.6 * math.exp(-0.3 * layer)


def _layer_weights(l, p):
    w_in_t = jnp.swapaxes(p['w_in'], 1, 2)[l]
    cols = lambda off, width: w_in_t[off:off + width]
    w_attn = jnp.concatenate([cols(OFF_Q, D), cols(OFF_CQ, D), cols(OFF_K, D), cols(OFF_V, D)], axis=0).astype(BF16)
    w_ssm = jnp.concatenate([cols(OFF_XBC, CONV_CH), cols(OFF_Z, SSM_INNER)], axis=0).astype(BF16)
    w_gate = cols(OFF_G, 3 * D).astype(BF16)
    w_dt = jnp.pad(cols(OFF_DT, SSM_HEADS), ((0, 128 - SSM_HEADS), (0, 0))).astype(BF16)
    row = lambda v: v.reshape(1, -1).astype(F32)
    return dict(
        g_mix=row(p['g_mix'][l]), w_attn=w_attn, w_ssm=w_ssm, w_dt=w_dt, w_gate=w_gate,
        dt_bias=jnp.pad(row(p['dt_bias'][l]), ((0, 0), (0, 128 - SSM_HEADS))),
        gq=row(jnp.tile(p['g_da_q'][l], TN // DK)), gk=row(jnp.tile(p['g_da_k'][l], TN // DK)),
        gcq=row(p['g_ca_q'][l]),
        lam=jnp.stack([p['lambda_q1'][l], p['lambda_k1'][l], p['lambda_q2'][l], p['lambda_k2'][l]]).astype(F32),
        g_da_out=row(p['g_da_out'][l]),
        conv_w=p['conv_w'][l].astype(F32), conv_b=row(p['conv_b'][l]),
        a_log=jnp.pad(row(p['a_log'][l]), ((0, 0), (0, 128 - SSM_HEADS))),
        d_skip=row(jnp.repeat(p['d_skip'][l], SSM_P)), g_ssm_out=row(p['g_ssm_out'][l]),
        w_br_attn=p['w_br_attn'][l].astype(BF16), w_br_ssm=p['w_br_ssm'][l].astype(BF16),
        w_br_cross=p['w_br_cross'][l].astype(BF16), w_out=p['w_out'][l].astype(BF16),
        g_mlp=row(p['g_mlp'][l]), w_up=p['w_up'][l].astype(BF16), w_down=p['w_down'][l].astype(BF16),
        g_mem=row(p['g_mem'][l]), w_mem_kv=p['w_mem_kv'][l].astype(BF16), g_ca_k=row(p['g_ca_k'][l]),
    )


def _seg64_matrix():
    i = jnp.arange(SEG_W) // DK
    return jnp.where(i[:, None] == i[None, :], 1.0 / DK, 0.0).astype(BF16)


def _layer(x, bsz, w, s64, attn_fn, mem_k, mem_v, h0, c0, state_layer, act_dtype):
    n = x.shape[0]
    t = n // bsz
    ub, k32, v32, uf, dt = _proj(x, w['g_mix'], w['w_attn'], w['w_ssm'], w['w_dt'], w['dt_bias'],
                                 w['gq'], w['gk'], w['gcq'], s64, t if act_dtype == BF16 else None)
    ub3 = ub.reshape(bsz, t, -1)
    o_a = attn_fn(ub3, k32, v32)
    uf3 = uf.reshape(bsz, t, -1)
    y_n, h_new, c_new = _ssd(uf3, 0, uf3, ((PB_Z - PB_XBC) * TN) // SSM_INNER, dt.reshape(bsz, t, 128), h0, c0,
                             state_layer, w['conv_w'], w['conv_b'], w['a_log'], w['d_skip'], w['g_ssm_out'], act_dtype)
    if act_dtype == BF16:
        cq3, cq_blk = ub3, (PB_CQ * TN) // D
    else:
        cq3, cq_blk = ub3[:, :, PB_CQ * TN:PB_K * TN].astype(F32), 0
    o_c = _cross(cq3, cq_blk, mem_k, mem_v, act_dtype)
    x = _merge(x, w['g_mix'], w['w_gate'], o_a.reshape(n, D), y_n.reshape(n, SSM_INNER), o_c.reshape(n, D),
               w['w_br_attn'], w['w_br_ssm'], w['w_br_cross'], w['w_out'])
    x = _mlp(x, w['g_mlp'], w['w_up'], w['w_down'])
    return x, k32, v32, h_new, c_new


def kernel(x_prompt, x_sample, cache_k, cache_v, cache_mem_k, cache_mem_v, state_ssm, state_conv, page_table, mem_prompt, rel_bias_table, g_mix, w_in, g_da_q, g_da_k, lambda_q1, lambda_k1, lambda_q2, lambda_k2, g_da_out, w_br_attn, conv_w, conv_b, dt_bias, a_log, d_skip, g_ssm_out, w_br_ssm, g_mem, w_mem_kv, g_ca_q, g_ca_k, w_br_cross, w_out, g_mlp, w_up, w_down):
    p = dict(g_mix=g_mix, w_in=w_in, g_da_q=g_da_q, g_da_k=g_da_k, lambda_q1=lambda_q1, lambda_k1=lambda_k1,
             lambda_q2=lambda_q2, lambda_k2=lambda_k2, g_da_out=g_da_out, w_br_attn=w_br_attn, conv_w=conv_w,
             conv_b=conv_b, dt_bias=dt_bias, a_log=a_log, d_skip=d_skip, g_ssm_out=g_ssm_out, w_br_ssm=w_br_ssm,
             g_mem=g_mem, w_mem_kv=w_mem_kv, g_ca_q=g_ca_q, g_ca_k=g_ca_k, w_br_cross=w_br_cross, w_out=w_out,
             g_mlp=g_mlp, w_up=w_up, w_down=w_down)
    depth = w_in.shape[0]
    bp, tp, _ = x_prompt.shape
    bs, ts, _ = x_sample.shape
    n_phys = cache_k.shape[1]
    past_len = page_table.shape[1] * PAGE
    tq = min(512, tp)
    table = rel_bias_table.astype(F32)
    s64 = _seg64_matrix()

    bias_diag, bias_near = _bias_tiles(table, 0, (tq, tq)), _bias_tiles(table, tq, (tq, tq))
    as_heads = lambda a: a.reshape(N_HEADS * 2 * ts, PAGE)
    bias_past = as_heads(_bias_tiles(table, PAGE, (ts, PAGE)))
    bias_new = as_heads(_bias_tiles(table, 0, (ts, PAGE)))

    cache_k4 = jnp.transpose(cache_k, (0, 1, 3, 4, 2)).reshape(depth, n_phys, D, PAGE)
    cache_v4 = cache_v.reshape(depth, n_phys, PAGE * N_HEADS, DV)
    mem_flat = mem_prompt.reshape(bp * MEM_LEN, D)
    zero_h = jnp.zeros((1, bp, N_PAIRS, 128, SSM_N), F32)
    zero_c = jnp.zeros((1, bp, CONV_K - 1, CONV_CH), F32)
    state_ssm5 = state_ssm.reshape(depth, bs, N_PAIRS, 128, SSM_N)

    xp = x_prompt.reshape(bp * tp, D)
    xs = x_sample.reshape(bs * ts, D)
    outs = {k: [] for k in ('kp', 'vp', 'mk', 'mv', 'hp', 'cp', 'ks', 'vs', 'hs', 'cs')}
    for l in range(depth):
        w = _layer_weights(l, p)
        lam0 = _lambda_init(l)

        mk, mv = _memkv(mem_flat, w['g_mem'], w['w_mem_kv'], w['g_ca_k'])
        attn_p = lambda ub3, k32, v32: _attn_prompt(ub3, w['lam'], bias_diag, bias_near, w['g_da_out'],
                                                    lam0=lam0, tq=tq)
        xp, k, v, hn, cn = _layer(xp, bp, w, s64, attn_p, mk.reshape(bp, MEM_LEN, D), mv.reshape(bp, MEM_LEN, D),
                                  zero_h, zero_c, 0, BF16)
        for key, val in zip(('kp', 'vp', 'mk', 'mv', 'hp', 'cp'), (k, v, mk, mv, hn, cn)):
            outs[key].append(val)

        def attn_s(ub3, k32, v32):
            q3 = ub3[:, :, 0:D].astype(F32)
            return _attn_sample(page_table, q3, k32.reshape(bs, ts, D), v32.reshape(bs, ts, D), cache_k4, cache_v4,
                                l, bias_past, bias_new, w['lam'], w['g_da_out'], lam0=lam0)

        xs, k, v, hn, cn = _layer(xs, bs, w, s64, attn_s, cache_mem_k[l].reshape(bs, MEM_LEN, D),
                                  cache_mem_v[l].reshape(bs, MEM_LEN, D), state_ssm5, state_conv, l, F32)
        for key, val in zip(('ks', 'vs', 'hs', 'cs'), (k, v, hn, cn)):
            outs[key].append(val)

    st = lambda key, shape: jnp.stack(outs[key]).reshape((depth,) + shape)
    return (xp.reshape(bp, tp, D), xs.reshape(bs, ts, D),
            jnp.transpose(st('kp', (bp, N_MAPS, DK, tp)), (0, 1, 4, 2, 3)), st('vp', (bp, tp, N_HEADS, DV)),
            st('mk', (bp, MEM_LEN, CA_HEADS, CA_D)), st('mv', (bp, MEM_LEN, CA_HEADS, CA_D)),
            st('hp', (bp, SSM_HEADS, SSM_P, SSM_N)), st('cp', (bp, CONV_K - 1, CONV_CH)),
            st('ks', (bs, ts, N_MAPS, DK)), st('vs', (bs, ts, N_HEADS, DV)),
            st('hs', (bs, SSM_HEADS, SSM_P, SSM_N)), st('cs', (bs, CONV_K - 1, CONV_CH)))
```

```python
import functools
import math

import jax
import jax.numpy as jnp
from jax import lax
from jax.experimental import pallas as pl
from jax.experimental.pallas import tpu as pltpu

F32, BF16 = jnp.float32, jnp.bfloat16
EPS = 1e-5
NEG = -1e30
LOG2E = math.log2(math.e)

D = 1024
N_MAPS, DK = 16, 64
N_HEADS, DV = 8, 128
SSM_INNER, SSM_P, SSM_N, SSM_G = 2048, 64, 128, 8
SSM_HEADS = SSM_INNER // SSM_P
N_PAIRS = SSM_HEADS // 2
CONV_K, CONV_CH = 4, 4096
MEM_LEN, CA_HEADS, CA_D = 256, 4, 256
D_FF = 4096
PAGE = 128
REL_BUCKETS, REL_MAX_DIST = 32, 128
SSM_L = 128
SSM_L_MIN = 16

OFF_Q, OFF_K, OFF_V, OFF_Z, OFF_XBC = 0, 1024, 2048, 3072, 5120
OFF_DT = OFF_XBC + CONV_CH
OFF_CQ = OFF_DT + SSM_HEADS
OFF_G = OFF_CQ + D

TN = 1024
SEG_W = 256
PB_Q, PB_CQ, PB_K, PB_V, PB_XBC, PB_Z, PB_END = 0, 1, 2, 3, 4, 8, 10
VMEM_LIMIT = 48 << 20


def _params(n_axes, vmem=VMEM_LIMIT):
    return pltpu.CompilerParams(dimension_semantics=("arbitrary",) * n_axes, vmem_limit_bytes=vmem)


def _rms(x, gain):
    ms = jnp.mean(x * x, axis=-1, keepdims=True)
    return x * lax.rsqrt(ms + EPS) * gain


def _sigmoid(x):
    return 1.0 / (1.0 + jnp.exp(-x))


def _softplus(x):
    return jnp.maximum(x, 0.0) + jnp.log1p(jnp.exp(-jnp.abs(x)))


def _dot(a, b):
    return jnp.dot(a, b, preferred_element_type=F32)


def _dot_nt(a, b):
    return lax.dot_general(a, b, (((1,), (1,)), ((), ())), preferred_element_type=F32)


def _dot_tn(a, b):
    return lax.dot_general(a, b, (((0,), (0,)), ((), ())), preferred_element_type=F32)


def _dot_split3(a01, x):
    hi = x.astype(BF16)
    r = x - hi.astype(F32)
    mid = r.astype(BF16)
    lo = (r - mid.astype(F32)).astype(BF16)
    return _dot(a01, hi) + _dot(a01, mid) + _dot(a01, lo)


def _bucket_thresholds():
    max_exact = REL_BUCKETS // 2
    ratio = REL_MAX_DIST / max_exact
    return [math.ceil(max_exact * ratio ** (k / (REL_BUCKETS - max_exact))) for k in range(1, REL_BUCKETS - max_exact)]


def _bias_kernel(tab_ref, o_ref, *, off, shape):
    m = pl.program_id(0)
    r = lax.broadcasted_iota(jnp.int32, shape, 0)
    c = lax.broadcasted_iota(jnp.int32, shape, 1)
    n = off + r - c
    nn = jnp.maximum(n, 0)
    max_exact = REL_BUCKETS // 2
    large = jnp.full(shape, max_exact, jnp.int32)
    for t in _bucket_thresholds():
        large = large + (nn >= t).astype(jnp.int32)
    bucket = jnp.where(nn < max_exact, nn, large)
    val = jnp.zeros(shape, F32)
    for b in range(REL_BUCKETS):
        val = jnp.where(bucket == b, tab_ref[b, m], val)
    val = (val - tab_ref[REL_BUCKETS - 1, m]) * LOG2E
    o_ref[0] = jnp.where(n >= 0, val, NEG)


def _bias_tiles(table, off, shape):
    return pl.pallas_call(
        functools.partial(_bias_kernel, off=off, shape=shape),
        out_shape=jax.ShapeDtypeStruct((N_MAPS,) + shape, F32),
        grid=(N_MAPS,),
        in_specs=[pl.BlockSpec(memory_space=pltpu.SMEM)],
        out_specs=pl.BlockSpec((1,) + shape, lambda m: (m, 0, 0)),
        compiler_params=_params(1),
        name="rel_bias_tiles",
    )(table)


def _proj_attn_kernel(x_ref, g_ref, w_ref, wdt_ref, dtb_ref, gq_ref, gk_ref, gcq_ref, s64_ref,
                      ub_ref, k32_ref, v32_ref, dt_ref):
    h = _rms(x_ref[...], g_ref[...]).astype(BF16)
    dt_ref[...] = _softplus(_dot_nt(h, wdt_ref[...]) + dtb_ref[...])

    def proj(blk):
        return _dot_nt(h, w_ref[blk * TN:(blk + 1) * TN, :])

    def seg64_norm(u, gain):
        sq = (u * u).astype(BF16)
        ms = jnp.concatenate([_dot(sq[:, c * SEG_W:(c + 1) * SEG_W], s64_ref[...]) for c in range(TN // SEG_W)],
                             axis=1)
        return u * lax.rsqrt(ms + EPS) * gain

    def out(blk):
        return slice(blk * TN, (blk + 1) * TN)

    ub_ref[:, out(PB_Q)] = (seg64_norm(proj(PB_Q), gq_ref[...]) * (DK ** -0.5 * LOG2E)).astype(BF16)
    u = proj(PB_CQ)
    for s in range(TN // CA_D):
        seg = u[:, s * CA_D:(s + 1) * CA_D]
        ub_ref[:, PB_CQ * TN + s * CA_D:PB_CQ * TN + (s + 1) * CA_D] = (
            _rms(seg, gcq_ref[...]) * (CA_D ** -0.5)).astype(BF16)
    kn = seg64_norm(proj(PB_K), gk_ref[...])
    k32_ref[...] = kn if k32_ref.shape == kn.shape else kn.T
    ub_ref[:, out(PB_K)] = kn.astype(BF16)
    u = proj(PB_V)
    v32_ref[...] = u
    ub_ref[:, out(PB_V)] = u.astype(BF16)


def _proj_ssm_kernel(x_ref, g_ref, w_ref, uf_ref):
    h = _rms(x_ref[...], g_ref[...]).astype(BF16)
    for blk in range(uf_ref.shape[1] // TN):
        uf_ref[:, blk * TN:(blk + 1) * TN] = _dot_nt(h, w_ref[blk * TN:(blk + 1) * TN, :])


def _proj(x, g, w_attn, w_ssm, wdt, dtb, gq, gk, gcq, s64, k_seq_len):
    n = x.shape[0]
    tm = min(512, n)
    if k_seq_len is None:
        k_shape, k_spec = (n, D), pl.BlockSpec((tm, D), lambda i: (i, 0))
    else:
        per_seq = k_seq_len // tm
        k_shape = (n // k_seq_len, D, k_seq_len)
        k_spec = pl.BlockSpec((None, D, tm), lambda i: (i // per_seq, 0, i % per_seq))
    row = lambda width: pl.BlockSpec((tm, width), lambda i: (i, 0))
    whole = lambda shape: pl.BlockSpec(shape, lambda i: (0, 0), pipeline_mode=pl.Buffered(1))
    ub, k32, v32, dt = pl.pallas_call(
        _proj_attn_kernel,
        out_shape=(jax.ShapeDtypeStruct((n, PB_XBC * TN), BF16),
                   jax.ShapeDtypeStruct(k_shape, F32),
                   jax.ShapeDtypeStruct((n, D), F32),
                   jax.ShapeDtypeStruct((n, 128), F32)),
        grid=(n // tm,),
        in_specs=[row(D), whole((1, D)), whole((PB_XBC * TN, D)), whole((128, D)), whole((1, 128)),
                  whole((1, TN)), whole((1, TN)), whole((1, CA_D)), whole((SEG_W, SEG_W))],
        out_specs=(row(PB_XBC * TN), k_spec, row(D), row(128)),
        compiler_params=_params(1),
        name="in_proj_attn",
    )(x, g, w_attn, wdt, dtb, gq, gk, gcq, s64)
    ssm_w = (PB_END - PB_XBC) * TN
    uf = pl.pallas_call(
        _proj_ssm_kernel,
        out_shape=jax.ShapeDtypeStruct((n, ssm_w), F32),
        grid=(n // tm,),
        in_specs=[row(D), whole((1, D)), whole((ssm_w, D))],
        out_specs=row(ssm_w),
        compiler_params=_params(1, 56 << 20),
        name="in_proj_ssm",
    )(x, g, w_ssm)
    return ub, k32, v32, uf, dt


def _memkv_kernel(m_ref, g_ref, w_ref, gk_ref, mk_ref, mv_ref, h_scr):
    j = pl.program_id(1)

    @pl.when(j == 0)
    def _():
        h_scr[...] = _rms(m_ref[...], g_ref[...]).astype(BF16)

    u = _dot(h_scr[...], w_ref[...])

    @pl.when(j == 0)
    def _():
        for s in range(CA_HEADS):
            mk_ref[:, s * CA_D:(s + 1) * CA_D] = _rms(u[:, s * CA_D:(s + 1) * CA_D], gk_ref[...])

    @pl.when(j == 1)
    def _():
        mv_ref[...] = u


def _memkv(mem, g, w, gk):
    n = mem.shape[0]
    tm = min(512, n)
    return pl.pallas_call(
        _memkv_kernel,
        out_shape=(jax.ShapeDtypeStruct((n, D), F32), jax.ShapeDtypeStruct((n, D), F32)),
        grid=(n // tm, 2),
        in_specs=[pl.BlockSpec((tm, D), lambda i, j: (i, 0)),
                  pl.BlockSpec((1, D), lambda i, j: (0, 0)),
                  pl.BlockSpec((D, D), lambda i, j: (0, j)),
                  pl.BlockSpec((1, CA_D), lambda i, j: (0, 0))],
        out_specs=(pl.BlockSpec((tm, D), lambda i, j: (i, 0)), pl.BlockSpec((tm, D), lambda i, j: (i, 0))),
        scratch_shapes=[pltpu.VMEM((tm, D), BF16)],
        compiler_params=_params(2),
        name="mem_kv",
    )(mem, g, w, gk)


def _lambda(lam_ref, lam0):
    lv = lam_ref[...]
    s1 = jnp.sum(lv[0:1] * lv[1:2], axis=-1, keepdims=True)
    s2 = jnp.sum(lv[2:3] * lv[3:4], axis=-1, keepdims=True)
    return jnp.exp(s1) - jnp.exp(s2) + lam0


def _softmax_step(state, s, v_bf16):
    m, l, acc = state
    m_new = jnp.maximum(m, jnp.max(s, axis=-1, keepdims=True))
    alpha = jnp.exp2(m - m_new)
    p = jnp.exp2(s - jnp.tile(m_new, (1, s.shape[1] // m_new.shape[1])))
    l = alpha * l + jnp.sum(p, axis=-1, keepdims=True)
    acc = alpha * acc + _dot(p.astype(BF16), v_bf16)
    return m_new, l, acc


def _attn_prompt_kernel(lam_ref, q_ref, k_ref, v_ref, bdiag_ref, bnear_ref, gout_ref, o_ref, m_ref, l_ref, acc_ref,
                        *, tq, lam0):
    qi = pl.program_id(2)
    q = q_ref[...]
    lane = lax.broadcasted_iota(jnp.int32, q.shape, 1)
    qm = jnp.concatenate([jnp.where(lane < DK, q, jnp.zeros_like(q)), jnp.where(lane >= DK, q, jnp.zeros_like(q))],
                         axis=0)
    m_ref[...] = jnp.full(m_ref.shape, NEG, F32)
    l_ref[...] = jnp.zeros(l_ref.shape, F32)
    acc_ref[...] = jnp.zeros(acc_ref.shape, F32)

    def update(kb, bias_ref):
        start = pl.multiple_of(kb * tq, tq)
        s = _dot_nt(qm, k_ref[pl.ds(start, tq), :])
        if bias_ref is not None:
            s = s + bias_ref[...]
        m_new, l_new, acc_new = _softmax_step((m_ref[...], l_ref[...], acc_ref[...]), s, v_ref[pl.ds(start, tq), :])
        m_ref[...] = m_new
        l_ref[...] = l_new
        acc_ref[...] = acc_new

    def loop(lo, hi, bias_ref):
        def body(kb, carry):
            update(kb, bias_ref)
            return carry
        lax.fori_loop(lo, hi, body, 0)

    loop(0, qi - 1, None)
    loop(jnp.maximum(qi - 1, 0), qi, bnear_ref)
    update(qi, bdiag_ref)

    lam = _lambda(lam_ref, lam0)
    o_maps = acc_ref[...] / l_ref[...]
    o = o_maps[0:tq] - lam * o_maps[tq:2 * tq]
    o_ref[...] = (_rms(o, gout_ref[...]) * (1.0 - lam0)).astype(o_ref.dtype)


def _attn_prompt(ub3, lam_vecs, bias_diag, bias_near, gout, *, lam0, tq):
    b, t, _ = ub3.shape
    kcol, vcol = (PB_K * TN) // DV, (PB_V * TN) // DV
    bias_spec = pl.BlockSpec((2 * tq, tq), lambda h, b, i: (h, 0))
    bias_diag, bias_near = (a.reshape(N_MAPS * tq, tq) for a in (bias_diag, bias_near))
    return pl.pallas_call(
        functools.partial(_attn_prompt_kernel, tq=tq, lam0=lam0),
        out_shape=jax.ShapeDtypeStruct((b, t, D), BF16),
        grid=(N_HEADS, b, t // tq),
        in_specs=[pl.BlockSpec((4, DK), lambda h, b, i: (0, 0)),
                  pl.BlockSpec((None, tq, DV), lambda h, b, i: (b, i, h)),
                  pl.BlockSpec((None, t, DV), lambda h, b, i: (b, 0, kcol + h)),
                  pl.BlockSpec((None, t, DV), lambda h, b, i: (b, 0, vcol + h)),
                  bias_spec, bias_spec,
                  pl.BlockSpec((1, DV), lambda h, b, i: (0, 0))],
        out_specs=pl.BlockSpec((None, tq, DV), lambda h, b, i: (b, i, h)),
        scratch_shapes=[pltpu.VMEM((2 * tq, DV), F32)] * 3,
        compiler_params=_params(3),
        name="diff_attn_prompt",
    )(lam_vecs, ub3, ub3, ub3, bias_diag, bias_near, gout)


def _head_queries(q, h):
    qh = q[:, h * DV:(h + 1) * DV]
    lane = lax.broadcasted_iota(jnp.int32, qh.shape, 1)
    return jnp.concatenate([jnp.where(lane < DK, qh, 0.0), jnp.where(lane >= DK, qh, 0.0)], axis=0).astype(BF16)


def _attn_sample_kernel(pt_ref, q_ref, kn_ref, vn_ref, bpast_ref, bnew_ref, lam_ref, gout_ref, *rest, npg, lam0):
    k_refs, v_refs = rest[:npg], rest[npg:2 * npg]
    o_ref, m_scr, l_scr, acc_scr = rest[2 * npg:]
    c = pl.program_id(1)
    last = c == pl.num_programs(1) - 1
    tdec = q_ref.shape[0]
    rows = 2 * tdec

    @pl.when(c == 0)
    def _():
        m_scr[...] = jnp.full(m_scr.shape, NEG, F32)
        l_scr[...] = jnp.zeros(l_scr.shape, F32)
        acc_scr[...] = jnp.zeros(acc_scr.shape, F32)

    q = q_ref[...]
    qms = [_head_queries(q, h) for h in range(N_HEADS)]

    def update(s, pv_fn):
        m_old = m_scr[...]
        m_new = jnp.maximum(m_old, jnp.max(s, axis=-1, keepdims=True))
        alpha = jnp.exp2(m_old - m_new)
        p = jnp.exp2(s - m_new[:, 0:1])
        l_scr[...] = alpha * l_scr[...] + jnp.sum(p, axis=-1, keepdims=True)
        acc_scr[...] = alpha * acc_scr[...] + pv_fn(p.astype(BF16))
        m_scr[...] = m_new

    near = last.astype(F32)
    s_pages = []
    for i in range(npg):
        s_i = jnp.concatenate([_dot(qms[h], k_refs[i][h * DV:(h + 1) * DV, :].astype(BF16))
                               for h in range(N_HEADS)], axis=0)
        if i == npg - 1:
            s_i = s_i + near * bpast_ref[...]
        s_pages.append(s_i)

    def pv_pages(p):
        heads = []
        for h in range(N_HEADS):
            out = None
            for i in range(npg):
                v_h = v_refs[i][pl.ds(h, PAGE, stride=N_HEADS), :].astype(BF16)
                part = _dot(p[h * rows:(h + 1) * rows, i * PAGE:(i + 1) * PAGE], v_h)
                out = part if out is None else out + part
            heads.append(out)
        return jnp.concatenate(heads, axis=0)

    update(jnp.concatenate(s_pages, axis=1), pv_pages)

    @pl.when(last)
    def _():
        pad = jnp.zeros((PAGE - tdec, D), F32)
        kn = jnp.concatenate([kn_ref[...], pad], axis=0).astype(BF16)
        vn = jnp.concatenate([vn_ref[...], pad], axis=0).astype(BF16)
        s_new = jnp.concatenate([_dot_nt(qms[h], kn[:, h * DV:(h + 1) * DV]) for h in range(N_HEADS)], axis=0)
        update(s_new + bnew_ref[...],
               lambda p: jnp.concatenate([_dot(p[h * rows:(h + 1) * rows], vn[:, h * DV:(h + 1) * DV])
                                          for h in range(N_HEADS)], axis=0))
        lam = _lambda(lam_ref, lam0)
        o_all = acc_scr[...] / l_scr[...]
        for h in range(N_HEADS):
            o = o_all[h * rows:h * rows + tdec] - lam * o_all[h * rows + tdec:(h + 1) * rows]
            o_ref[:, h * DV:(h + 1) * DV] = _rms(o, gout_ref[...]) * (1.0 - lam0)


def _attn_sample(page_table, q3, kn3, vn3, cache_k4, cache_v4, layer, bpast, bnew, lam_vecs, gout, *, lam0):
    bs, tdec, _ = q3.shape
    n_pages = page_table.shape[1]
    npg = math.gcd(n_pages, 16)
    pt = page_table.reshape(-1)

    def page_spec(i):
        return pl.BlockSpec((None, None, D, PAGE),
                            lambda b, c, pt_ref: (layer, pt_ref[b * n_pages + c * npg + i], 0, 0))

    per_b = pl.BlockSpec((None, tdec, D), lambda b, c, pt_ref: (b, 0, 0))
    whole = lambda shape: pl.BlockSpec(shape, lambda b, c, pt_ref: (0,) * len(shape))
    stat = pltpu.VMEM((N_HEADS * 2 * tdec, DV), F32)
    return pl.pallas_call(
        functools.partial(_attn_sample_kernel, npg=npg, lam0=lam0),
        out_shape=jax.ShapeDtypeStruct((bs, tdec, D), F32),
        grid_spec=pltpu.PrefetchScalarGridSpec(
            num_scalar_prefetch=1,
            grid=(bs, n_pages // npg),
            in_specs=[per_b, per_b, per_b,
                      whole((N_HEADS * 2 * tdec, PAGE)), whole((N_HEADS * 2 * tdec, PAGE)),
                      whole((4, DK)), whole((1, DV))]
                     + [page_spec(i) for i in range(npg)] + [page_spec(i) for i in range(npg)],
            out_specs=per_b,
            scratch_shapes=[stat, stat, stat]),
        compiler_params=_params(2),
        name="diff_attn_sample",
    )(pt, q3, kn3, vn3, bpast, bnew, lam_vecs, gout, *([cache_k4] * npg), *([cache_v4] * npg))


def _cross_kernel(cq_ref, mk_ref, mv_ref, o_ref):
    rows = cq_ref.shape[0]
    for h in range(CA_HEADS):
        cols = slice(h * CA_D, (h + 1) * CA_D)
        q = cq_ref[:, cols]
        if rows % 16:
            q = jnp.concatenate([q.astype(F32), jnp.zeros((16 - rows % 16, CA_D), F32)], axis=0)
        s = _dot_nt(q.astype(BF16), mk_ref[:, cols].astype(BF16))
        p = jnp.exp(s - jnp.max(s, axis=-1, keepdims=True))
        o = _dot(p.astype(BF16), mv_ref[:, cols].astype(BF16)) / jnp.sum(p, axis=-1, keepdims=True)
        o_ref[:, cols] = o[0:rows].astype(o_ref.dtype)


def _cross(cq3, col_block, mem_k, mem_v, out_dtype):
    b, t, _ = cq3.shape
    tm = min(512, t)
    mem_spec = pl.BlockSpec((None, MEM_LEN, D), lambda b, i: (b, 0, 0))
    return pl.pallas_call(
        _cross_kernel,
        out_shape=jax.ShapeDtypeStruct((b, t, D), out_dtype),
        grid=(b, t // tm),
        in_specs=[pl.BlockSpec((None, tm, D), lambda b, i: (b, i, col_block)), mem_spec, mem_spec],
        out_specs=pl.BlockSpec((None, tm, D), lambda b, i: (b, i, 0)),
        compiler_params=_params(2),
        name="mem_cross_attn",
    )(cq3, mem_k, mem_v)


def _conv_silu(window, cw, cb):
    acc = cb + cw[CONV_K - 1:CONV_K] * window(CONV_K - 1)
    for k in range(CONV_K - 1):
        acc = acc + cw[k:k + 1] * window(k)
    return acc * _sigmoid(acc)


def _ssd_kernel(xbc_ref, z_ref, dt_ref, h0_ref, c0_ref, cw_ref, cb_ref, alog_ref, dskip_ref, gout_ref,
                y_ref, hn_ref, cn_ref, buf, xc_scr, dt_scr, yb, h_scr, *, tv, L):
    c = pl.program_id(1)
    head = 8

    @pl.when(c == 0)
    def _():
        buf[0:head, :] = jnp.zeros((head, CONV_CH), F32)
        buf[head - (CONV_K - 1):head, :] = c0_ref[...]
        h_scr[...] = h0_ref[...]
        if tv < L:
            buf[head + tv:, :] = jnp.zeros((L - tv, CONV_CH), F32)
            dt_scr[...] = jnp.zeros(dt_scr.shape, F32)

    buf[head:head + tv, :] = xbc_ref[...]
    cn_ref[...] = buf[head + tv - (CONV_K - 1):head + tv, :]

    first_tap = head - (CONV_K - 1)
    for j in range(CONV_CH // 512):
        cols = slice(j * 512, (j + 1) * 512)
        xc_scr[:, cols] = _conv_silu(lambda k: buf[first_tap + k:first_tap + k + L, cols],
                                     cw_ref[:, cols], cb_ref[:, cols])

    if tv == L:
        buf[0:head, :] = buf[L:L + head, :]
        dt = dt_ref[...]
    else:
        dt_scr[0:tv, :] = dt_ref[...]
        dt = dt_scr[...]

    def zs(cols):
        z = z_ref[:, cols]
        return z * _sigmoid(z)

    _ssd_scan(lambda cols: xc_scr[:, cols], zs, dt, alog_ref, dskip_ref, gout_ref, y_ref, hn_ref, yb, h_scr, tv, L)


def _ssd_scan(xc, zs, dt, alog_ref, dskip_ref, gout_ref, y_ref, hn_ref, yb, h_scr, tv, L):
    a_neg = -jnp.exp(alog_ref[...])
    d_a = dt * a_neg
    row = lax.broadcasted_iota(jnp.int32, (L, L), 0)
    col = lax.broadcasted_iota(jnp.int32, (L, L), 1)
    causal = col <= row
    a_cum = _dot_split3(causal.astype(BF16), d_a)
    a_cum_t = a_cum.T
    dt_t = dt.T
    a_last = a_cum[L - 1:L, :]
    e_cum = jnp.exp(a_cum)
    to_end = jnp.exp(a_last - a_cum) * dt
    e_last = jnp.exp(a_last)
    lane = lax.broadcasted_iota(jnp.int32, (L, 128), 1)
    first = lane < SSM_P
    row_first = lax.broadcasted_iota(jnp.int32, (128, 128), 0) < SSM_P

    for g in range(SSM_G):
        b_g = xc(slice(SSM_INNER + g * SSM_N, SSM_INNER + (g + 1) * SSM_N)).astype(BF16)
        c_off = SSM_INNER + SSM_G * SSM_N
        c_g = xc(slice(c_off + g * SSM_N, c_off + (g + 1) * SSM_N)).astype(BF16)
        cb = _dot_nt(c_g, b_g)
        for jj in (2 * g, 2 * g + 1):
            cols = slice(jj * 128, (jj + 1) * 128)
            x_pair = xc(cols)
            x_bf = x_pair.astype(BF16)
            yw = []
            for hh in (2 * jj, 2 * jj + 1):
                seg = a_cum[:, hh:hh + 1] - a_cum_t[hh:hh + 1, :]
                w = cb * jnp.exp(jnp.where(causal, seg, NEG)) * dt_t[hh:hh + 1, :]
                yw.append(_dot(w.astype(BF16), x_bf))
            ha, hb = 2 * jj, 2 * jj + 1
            hp = h_scr[jj]
            y = jnp.where(first, yw[0], yw[1])
            y = y + _dot_nt(c_g, hp.astype(BF16)) * jnp.where(first, e_cum[:, ha:ha + 1], e_cum[:, hb:hb + 1])
            y = y + dskip_ref[:, cols] * x_pair
            te = jnp.where(first, to_end[:, ha:ha + 1], to_end[:, hb:hb + 1])
            upd = _dot_tn((x_pair * te).astype(BF16), b_g)
            h_scr[jj] = hp * jnp.where(row_first, e_last[:, ha:ha + 1], e_last[:, hb:hb + 1]) + upd
            yb[:, cols] = y[0:tv] * zs(cols)

    gw = SSM_INNER // SSM_G
    for g in range(SSM_G):
        cols = slice(g * gw, (g + 1) * gw)
        y_ref[:, cols] = _rms(yb[:, cols], gout_ref[:, cols]).astype(y_ref.dtype)
    hn_ref[...] = h_scr[...]


def _ssd(xbc_src, xbc_blk, z_src, z_blk, dt3, h0, c0, layer, cw, cb, alog, dskip, gout, out_dtype):
    b, t, _ = dt3.shape
    tv = min(SSM_L, t)
    chunk = max(tv, SSM_L_MIN)
    nc = t // tv
    seq = lambda width, blk: pl.BlockSpec((None, tv, width), lambda b, c: (b, c, blk))
    per_b = lambda shape: pl.BlockSpec((None,) + shape, lambda b, c: (b,) + (0,) * len(shape))
    state = lambda shape: pl.BlockSpec((None, None) + shape, lambda b, c: (layer, b) + (0,) * len(shape))
    whole = lambda shape: pl.BlockSpec(shape, lambda b, c: (0,) * len(shape))
    y_shape = jax.ShapeDtypeStruct((b, t, SSM_INNER), out_dtype)
    h_shape = jax.ShapeDtypeStruct((b, N_PAIRS, 128, SSM_N), F32)
    tail = [whole((1, 128)), whole((1, SSM_INNER)), whole((1, SSM_INNER))]
    yb_h = [pltpu.VMEM((tv, SSM_INNER), F32), pltpu.VMEM((N_PAIRS, 128, SSM_N), F32)]
    return pl.pallas_call(
        functools.partial(_ssd_kernel, tv=tv, L=chunk),
        out_shape=(y_shape, h_shape, jax.ShapeDtypeStruct((b, CONV_K - 1, CONV_CH), F32)),
        grid=(b, nc),
        in_specs=[seq(CONV_CH, xbc_blk), seq(SSM_INNER, z_blk), seq(128, 0),
                  state((N_PAIRS, 128, SSM_N)), state((CONV_K - 1, CONV_CH)),
                  whole((CONV_K, CONV_CH)), whole((1, CONV_CH))] + tail,
        out_specs=(seq(SSM_INNER, 0), per_b((N_PAIRS, 128, SSM_N)), per_b((CONV_K - 1, CONV_CH))),
        scratch_shapes=[pltpu.VMEM((chunk + 8, CONV_CH), F32), pltpu.VMEM((chunk, CONV_CH), F32),
                        pltpu.VMEM((chunk, 128), F32)] + yb_h,
        compiler_params=_params(2),
        name="conv_ssd",
    )(xbc_src, z_src, dt3, h0, c0, cw, cb, alog, dskip, gout)


def _merge_kernel(x_ref, gmix_ref, wg_ref, oa_ref, ys_ref, oc_ref, wa_ref, ws_ref, wc_ref, wo_ref, o_ref):
    x = x_ref[...]
    h = _rms(x, gmix_ref[...]).astype(BF16)
    branches = ((oa_ref, wa_ref), (ys_ref, ws_ref), (oc_ref, wc_ref))
    merged = None
    for b, (act_ref, w_ref) in enumerate(branches):
        gate = _sigmoid(_dot_nt(h, wg_ref[b * D:(b + 1) * D, :]))
        term = gate * _dot(act_ref[...].astype(BF16), w_ref[...])
        merged = term if merged is None else merged + term
    o_ref[...] = x + _dot(merged.astype(BF16), wo_ref[...])


def _merge(x, gmix, wg, oa, ys, oc, wa, ws, wc, wo):
    n = x.shape[0]
    tm = min(512, n)
    row = lambda width: pl.BlockSpec((tm, width), lambda i: (i, 0))
    whole = lambda shape: pl.BlockSpec(shape, lambda i: (0, 0), pipeline_mode=pl.Buffered(1))
    return pl.pallas_call(
        _merge_kernel,
        out_shape=jax.ShapeDtypeStruct((n, D), F32),
        grid=(n // tm,),
        in_specs=[row(D), whole((1, D)), whole((3 * D, D)), row(D), row(SSM_INNER), row(D),
                  whole((D, D)), whole((SSM_INNER, D)), whole((D, D)), whole((D, D))],
        out_specs=row(D),
        compiler_params=_params(1),
        name="branch_merge",
    )(x, gmix, wg, oa, ys, oc, wa, ws, wc, wo)


MLP_TF = 1024


def _mlp_kernel(x_ref, g_ref, wu_ref, wd_ref, o_ref):
    x = x_ref[...]
    h = _rms(x, g_ref[...]).astype(BF16)
    acc = x
    for f in range(D_FF // MLP_TF):
        cols = slice(f * MLP_TF, (f + 1) * MLP_TF)
        a = jnp.maximum(_dot(h, wu_ref[:, cols]), 0.0)
        acc = acc + _dot((a * a).astype(BF16), wd_ref[cols, :])
    o_ref[...] = acc


def _mlp(x, g, wu, wd):
    n = x.shape[0]
    tm = min(512, n)
    row = pl.BlockSpec((tm, D), lambda i: (i, 0))
    whole = lambda shape: pl.BlockSpec(shape, lambda i: (0, 0), pipeline_mode=pl.Buffered(1))
    return pl.pallas_call(
        _mlp_kernel,
        out_shape=jax.ShapeDtypeStruct((n, D), F32),
        grid=(n // tm,),
        in_specs=[row, whole((1, D)), whole((D, D_FF)), whole((D_FF, D))],
        out_specs=row,
        compiler_params=_params(1),
        name="relu2_mlp",
    )(x, g, wu, wd)


def _lambda_init(layer):
    return 0.8 - 0.6 * math.exp(-0.3 * layer)


def _layer_weights(l, p):
    w_in_t = jnp.swapaxes(p['w_in'], 1, 2)[l]
    cols = lambda off, width: w_in_t[off:off + width]
    w_attn = jnp.concatenate([cols(OFF_Q, D), cols(OFF_CQ, D), cols(OFF_K, D), cols(OFF_V, D)], axis=0).astype(BF16)
    w_ssm = jnp.concatenate([cols(OFF_XBC, CONV_CH), cols(OFF_Z, SSM_INNER)], axis=0).astype(BF16)
    w_gate = cols(OFF_G, 3 * D).astype(BF16)
    w_dt = jnp.pad(cols(OFF_DT, SSM_HEADS), ((0, 128 - SSM_HEADS), (0, 0))).astype(BF16)
    row = lambda v: v.reshape(1, -1).astype(F32)
    return dict(
        g_mix=row(p['g_mix'][l]), w_attn=w_attn, w_ssm=w_ssm, w_dt=w_dt, w_gate=w_gate,
        dt_bias=jnp.pad(row(p['dt_bias'][l]), ((0, 0), (0, 128 - SSM_HEADS))),
        gq=row(jnp.tile(p['g_da_q'][l], TN // DK)), gk=row(jnp.tile(p['g_da_k'][l], TN // DK)),
        gcq=row(p['g_ca_q'][l]),
        lam=jnp.stack([p['lambda_q1'][l], p['lambda_k1'][l], p['lambda_q2'][l], p['lambda_k2'][l]]).astype(F32),
        g_da_out=row(p['g_da_out'][l]),
        conv_w=p['conv_w'][l].astype(F32), conv_b=row(p['conv_b'][l]),
        a_log=jnp.pad(row(p['a_log'][l]), ((0, 0), (0, 128 - SSM_HEADS))),
        d_skip=row(jnp.repeat(p['d_skip'][l], SSM_P)), g_ssm_out=row(p['g_ssm_out'][l]),
        w_br_attn=p['w_br_attn'][l].astype(BF16), w_br_ssm=p['w_br_ssm'][l].astype(BF16),
        w_br_cross=p['w_br_cross'][l].astype(BF16), w_out=p['w_out'][l].astype(BF16),
        g_mlp=row(p['g_mlp'][l]), w_up=p['w_up'][l].astype(BF16), w_down=p['w_down'][l].astype(BF16),
        g_mem=row(p['g_mem'][l]), w_mem_kv=p['w_mem_kv'][l].astype(BF16), g_ca_k=row(p['g_ca_k'][l]),
    )


def _seg64_matrix():
    i = jnp.arange(SEG_W) // DK
    return jnp.where(i[:, None] == i[None, :], 1.0 / DK, 0.0).astype(BF16)


def _layer(x, bsz, w, s64, attn_fn, mem_k, mem_v, h0, c0, state_layer, act_dtype):
    n = x.shape[0]
    t = n // bsz
    ub, k32, v32, uf, dt = _proj(x, w['g_mix'], w['w_attn'], w['w_ssm'], w['w_dt'], w['dt_bias'],
                                 w['gq'], w['gk'], w['gcq'], s64, t if act_dtype == BF16 else None)
    ub3 = ub.reshape(bsz, t, -1)
    o_a = attn_fn(ub3, k32, v32)
    uf3 = uf.reshape(bsz, t, -1)
    y_n, h_new, c_new = _ssd(uf3, 0, uf3, ((PB_Z - PB_XBC) * TN) // SSM_INNER, dt.reshape(bsz, t, 128), h0, c0,
                             state_layer, w['conv_w'], w['conv_b'], w['a_log'], w['d_skip'], w['g_ssm_out'], act_dtype)
    if act_dtype == BF16:
        cq3, cq_blk = ub3, (PB_CQ * TN) // D
    else:
        cq3, cq_blk = ub3[:, :, PB_CQ * TN:PB_K * TN].astype(F32), 0
    o_c = _cross(cq3, cq_blk, mem_k, mem_v, act_dtype)
    x = _merge(x, w['g_mix'], w['w_gate'], o_a.reshape(n, D), y_n.reshape(n, SSM_INNER), o_c.reshape(n, D),
               w['w_br_attn'], w['w_br_ssm'], w['w_br_cross'], w['w_out'])
    x = _mlp(x, w['g_mlp'], w['w_up'], w['w_down'])
    return x, k32, v32, h_new, c_new


def kernel(x_prompt, x_sample, cache_k, cache_v, cache_mem_k, cache_mem_v, state_ssm, state_conv, page_table, mem_prompt, rel_bias_table, g_mix, w_in, g_da_q, g_da_k, lambda_q1, lambda_k1, lambda_q2, lambda_k2, g_da_out, w_br_attn, conv_w, conv_b, dt_bias, a_log, d_skip, g_ssm_out, w_br_ssm, g_mem, w_mem_kv, g_ca_q, g_ca_k, w_br_cross, w_out, g_mlp, w_up, w_down):
    p = dict(g_mix=g_mix, w_in=w_in, g_da_q=g_da_q, g_da_k=g_da_k, lambda_q1=lambda_q1, lambda_k1=lambda_k1,
             lambda_q2=lambda_q2, lambda_k2=lambda_k2, g_da_out=g_da_out, w_br_attn=w_br_attn, conv_w=conv_w,
             conv_b=conv_b, dt_bias=dt_bias, a_log=a_log, d_skip=d_skip, g_ssm_out=g_ssm_out, w_br_ssm=w_br_ssm,
             g_mem=g_mem, w_mem_kv=w_mem_kv, g_ca_q=g_ca_q, g_ca_k=g_ca_k, w_br_cross=w_br_cross, w_out=w_out,
             g_mlp=g_mlp, w_up=w_up, w_down=w_down)
    depth = w_in.shape[0]
    bp, tp, _ = x_prompt.shape
    bs, ts, _ = x_sample.shape
    n_phys = cache_k.shape[1]
    past_len = page_table.shape[1] * PAGE
    tq = min(512, tp)
    table = rel_bias_table.astype(F32)
    s64 = _seg64_matrix()

    bias_diag, bias_near = _bias_tiles(table, 0, (tq, tq)), _bias_tiles(table, tq, (tq, tq))
    as_heads = lambda a: a.reshape(N_HEADS * 2 * ts, PAGE)
    bias_past = as_heads(_bias_tiles(table, PAGE, (ts, PAGE)))
    bias_new = as_heads(_bias_tiles(table, 0, (ts, PAGE)))

    cache_k4 = jnp.transpose(cache_k, (0, 1, 3, 4, 2)).reshape(depth, n_phys, D, PAGE)
    cache_v4 = cache_v.reshape(depth, n_phys, PAGE * N_HEADS, DV)
    mem_flat = mem_prompt.reshape(bp * MEM_LEN, D)
    zero_h = jnp.zeros((1, bp, N_PAIRS, 128, SSM_N), F32)
    zero_c = jnp.zeros((1, bp, CONV_K - 1, CONV_CH), F32)
    state_ssm5 = state_ssm.reshape(depth, bs, N_PAIRS, 128, SSM_N)

    xp = x_prompt.reshape(bp * tp, D)
    xs = x_sample.reshape(bs * ts, D)
    outs = {k: [] for k in ('kp', 'vp', 'mk', 'mv', 'hp', 'cp', 'ks', 'vs', 'hs', 'cs')}
    for l in range(depth):
        w = _layer_weights(l, p)
        lam0 = _lambda_init(l)

        mk, mv = _memkv(mem_flat, w['g_mem'], w['w_mem_kv'], w['g_ca_k'])
        attn_p = lambda ub3, k32, v32: _attn_prompt(ub3, w['lam'], bias_diag, bias_near, w['g_da_out'],
                                                    lam0=lam0, tq=tq)
        xp, k, v, hn, cn = _layer(xp, bp, w, s64, attn_p, mk.reshape(bp, MEM_LEN, D), mv.reshape(bp, MEM_LEN, D),
                                  zero_h, zero_c, 0, BF16)
        for key, val in zip(('kp', 'vp', 'mk', 'mv', 'hp', 'cp'), (k, v, mk, mv, hn, cn)):
            outs[key].append(val)

        def attn_s(ub3, k32, v32):
            q3 = ub3[:, :, 0:D].astype(F32)
            return _attn_sample(page_table, q3, k32.reshape(bs, ts, D), v32.reshape(bs, ts, D), cache_k4, cache_v4,
                                l, bias_past, bias_new, w['lam'], w['g_da_out'], lam0=lam0)

        xs, k, v, hn, cn = _layer(xs, bs, w, s64, attn_s, cache_mem_k[l].reshape(bs, MEM_LEN, D),
                                  cache_mem_v[l].reshape(bs, MEM_LEN, D), state_ssm5, state_conv, l, F32)
        for key, val in zip(('ks', 'vs', 'hs', 'cs'), (k, v, hn, cn)):
            outs[key].append(val)

    st = lambda key, shape: jnp.stack(outs[key]).reshape((depth,) + shape)
    return (xp.reshape(bp, tp, D), xs.reshape(bs, ts, D),
            jnp.transpose(st('kp', (bp, N_MAPS, DK, tp)), (0, 1, 4, 2, 3)), st('vp', (bp, tp, N_HEADS, DV)),
            st('mk', (bp, MEM_LEN, CA_HEADS, CA_D)), st('mv', (bp, MEM_LEN, CA_HEADS, CA_D)),
            st('hp', (bp, SSM_HEADS, SSM_P, SSM_N)), st('cp', (bp, CONV_K - 1, CONV_CH)),
            st('ks', (bs, ts, N_MAPS, DK)), st('vs', (bs, ts, N_HEADS, DV)),
            st('hs', (bs, SSM_HEADS, SSM_P, SSM_N)), st('cs', (bs, CONV_K - 1, CONV_CH)))
```

```python
import functools
import math

import jax
import jax.numpy as jnp
from jax import lax
from jax.experimental import pallas as pl
from jax.experimental.pallas import tpu as pltpu

F32, BF16 = jnp.float32, jnp.bfloat16
EPS = 1e-5
NEG = -1e30
LOG2E = math.log2(math.e)

D = 1024
N_MAPS, DK = 16, 64
N_HEADS, DV = 8, 128
SSM_INNER, SSM_P, SSM_N, SSM_G = 2048, 64, 128, 8
SSM_HEADS = SSM_INNER // SSM_P
N_PAIRS = SSM_HEADS // 2
CONV_K, CONV_CH = 4, 4096
MEM_LEN, CA_HEADS, CA_D = 256, 4, 256
D_FF = 4096
PAGE = 128
REL_BUCKETS, REL_MAX_DIST = 32, 128
SSM_L = 128
SSM_L_MIN = 16

OFF_Q, OFF_K, OFF_V, OFF_Z, OFF_XBC = 0, 1024, 2048, 3072, 5120
OFF_DT = OFF_XBC + CONV_CH
OFF_CQ = OFF_DT + SSM_HEADS
OFF_G = OFF_CQ + D

TN = 1024
SEG_W = 256
PB_Q, PB_CQ, PB_K, PB_V, PB_XBC, PB_Z, PB_END = 0, 1, 2, 3, 4, 8, 10
VMEM_LIMIT = 48 << 20


def _params(n_axes, vmem=VMEM_LIMIT):
    return pltpu.CompilerParams(dimension_semantics=("arbitrary",) * n_axes, vmem_limit_bytes=vmem)


def _rms(x, gain):
    ms = jnp.mean(x * x, axis=-1, keepdims=True)
    return x * lax.rsqrt(ms + EPS) * gain


def _sigmoid(x):
    return 1.0 / (1.0 + jnp.exp(-x))


def _softplus(x):
    return jnp.maximum(x, 0.0) + jnp.log1p(jnp.exp(-jnp.abs(x)))


def _dot(a, b):
    return jnp.dot(a, b, preferred_element_type=F32)


def _dot_nt(a, b):
    return lax.dot_general(a, b, (((1,), (1,)), ((), ())), preferred_element_type=F32)


def _dot_tn(a, b):
    return lax.dot_general(a, b, (((0,), (0,)), ((), ())), preferred_element_type=F32)


def _dot_split3(a01, x):
    hi = x.astype(BF16)
    r = x - hi.astype(F32)
    mid = r.astype(BF16)
    lo = (r - mid.astype(F32)).astype(BF16)
    return _dot(a01, hi) + _dot(a01, mid) + _dot(a01, lo)


def _bucket_thresholds():
    max_exact = REL_BUCKETS // 2
    ratio = REL_MAX_DIST / max_exact
    return [math.ceil(max_exact * ratio ** (k / (REL_BUCKETS - max_exact))) for k in range(1, REL_BUCKETS - max_exact)]


def _bias_kernel(tab_ref, o_ref, *, off, shape):
    m = pl.program_id(0)
    r = lax.broadcasted_iota(jnp.int32, shape, 0)
    c = lax.broadcasted_iota(jnp.int32, shape, 1)
    n = off + r - c
    nn = jnp.maximum(n, 0)
    max_exact = REL_BUCKETS // 2
    large = jnp.full(shape, max_exact, jnp.int32)
    for t in _bucket_thresholds():
        large = large + (nn >= t).astype(jnp.int32)
    bucket = jnp.where(nn < max_exact, nn, large)
    val = jnp.zeros(shape, F32)
    for b in range(REL_BUCKETS):
        val = jnp.where(bucket == b, tab_ref[b, m], val)
    val = (val - tab_ref[REL_BUCKETS - 1, m]) * LOG2E
    o_ref[0] = jnp.where(n >= 0, val, NEG)


def _bias_tiles(table, off, shape):
    return pl.pallas_call(
        functools.partial(_bias_kernel, off=off, shape=shape),
        out_shape=jax.ShapeDtypeStruct((N_MAPS,) + shape, F32),
        grid=(N_MAPS,),
        in_specs=[pl.BlockSpec(memory_space=pltpu.SMEM)],
        out_specs=pl.BlockSpec((1,) + shape, lambda m: (m, 0, 0)),
        compiler_params=_params(1),
        name="rel_bias_tiles",
    )(table)


def _proj_attn_kernel(x_ref, g_ref, w_ref, wdt_ref, dtb_ref, gq_ref, gk_ref, gcq_ref, s64_ref,
                      ub_ref, k32_ref, v32_ref, dt_ref):
    h = _rms(x_ref[...], g_ref[...]).astype(BF16)
    dt_ref[...] = _softplus(_dot_nt(h, wdt_ref[...]) + dtb_ref[...])

    def proj(blk):
        return _dot_nt(h, w_ref[blk * TN:(blk + 1) * TN, :])

    def seg64_norm(u, gain):
        sq = (u * u).astype(BF16)
        ms = jnp.concatenate([_dot(sq[:, c * SEG_W:(c + 1) * SEG_W], s64_ref[...]) for c in range(TN // SEG_W)],
                             axis=1)
        return u * lax.rsqrt(ms + EPS) * gain

    def out(blk):
        return slice(blk * TN, (blk + 1) * TN)

    ub_ref[:, out(PB_Q)] = (seg64_norm(proj(PB_Q), gq_ref[...]) * (DK ** -0.5 * LOG2E)).astype(BF16)
    u = proj(PB_CQ)
    for s in range(TN // CA_D):
        seg = u[:, s * CA_D:(s + 1) * CA_D]
        ub_ref[:, PB_CQ * TN + s * CA_D:PB_CQ * TN + (s + 1) * CA_D] = (
            _rms(seg, gcq_ref[...]) * (CA_D ** -0.5)).astype(BF16)
    kn = seg64_norm(proj(PB_K), gk_ref[...])
    k32_ref[...] = kn if k32_ref.shape == kn.shape else kn.T
    ub_ref[:, out(PB_K)] = kn.astype(BF16)
    u = proj(PB_V)
    v32_ref[...] = u
    ub_ref[:, out(PB_V)] = u.astype(BF16)


def _proj_ssm_kernel(x_ref, g_ref, w_ref, uf_ref):
    h = _rms(x_ref[...], g_ref[...]).astype(BF16)
    for blk in range(uf_ref.shape[1] // TN):
        uf_ref[:, blk * TN:(blk + 1) * TN] = _dot_nt(h, w_ref[blk * TN:(blk + 1) * TN, :])


def _proj(x, g, w_attn, w_ssm, wdt, dtb, gq, gk, gcq, s64, k_seq_len):
    n = x.shape[0]
    tm = min(512, n)
    if k_seq_len is None:
        k_shape, k_spec = (n, D), pl.BlockSpec((tm, D), lambda i: (i, 0))
    else:
        per_seq = k_seq_len // tm
        k_shape = (n // k_seq_len, D, k_seq_len)
        k_spec = pl.BlockSpec((None, D, tm), lambda i: (i // per_seq, 0, i % per_seq))
    row = lambda width: pl.BlockSpec((tm, width), lambda i: (i, 0))
    whole = lambda shape: pl.BlockSpec(shape, lambda i: (0, 0), pipeline_mode=pl.Buffered(1))
    ub, k32, v32, dt = pl.pallas_call(
        _proj_attn_kernel,
        out_shape=(jax.ShapeDtypeStruct((n, PB_XBC * TN), BF16),
                   jax.ShapeDtypeStruct(k_shape, F32),
                   jax.ShapeDtypeStruct((n, D), F32),
                   jax.ShapeDtypeStruct((n, 128), F32)),
        grid=(n // tm,),
        in_specs=[row(D), whole((1, D)), whole((PB_XBC * TN, D)), whole((128, D)), whole((1, 128)),
                  whole((1, TN)), whole((1, TN)), whole((1, CA_D)), whole((SEG_W, SEG_W))],
        out_specs=(row(PB_XBC * TN), k_spec, row(D), row(128)),
        compiler_params=_params(1),
        name="in_proj_attn",
    )(x, g, w_attn, wdt, dtb, gq, gk, gcq, s64)
    ssm_w = (PB_END - PB_XBC) * TN
    uf = pl.pallas_call(
        _proj_ssm_kernel,
        out_shape=jax.ShapeDtypeStruct((n, ssm_w), F32),
        grid=(n // tm,),
        in_specs=[row(D), whole((1, D)), whole((ssm_w, D))],
        out_specs=row(ssm_w),
        compiler_params=_params(1, 56 << 20),
        name="in_proj_ssm",
    )(x, g, w_ssm)
    return ub, k32, v32, uf, dt


def _memkv_kernel(m_ref, g_ref, w_ref, gk_ref, mk_ref, mv_ref, h_scr):
    j = pl.program_id(1)

    @pl.when(j == 0)
    def _():
        h_scr[...] = _rms(m_ref[...], g_ref[...]).astype(BF16)

    u = _dot(h_scr[...], w_ref[...])

    @pl.when(j == 0)
    def _():
        for s in range(CA_HEADS):
            mk_ref[:, s * CA_D:(s + 1) * CA_D] = _rms(u[:, s * CA_D:(s + 1) * CA_D], gk_ref[...])

    @pl.when(j == 1)
    def _():
        mv_ref[...] = u


def _memkv(mem, g, w, gk):
    n = mem.shape[0]
    tm = min(512, n)
    return pl.pallas_call(
        _memkv_kernel,
        out_shape=(jax.ShapeDtypeStruct((n, D), F32), jax.ShapeDtypeStruct((n, D), F32)),
        grid=(n // tm, 2),
        in_specs=[pl.BlockSpec((tm, D), lambda i, j: (i, 0)),
                  pl.BlockSpec((1, D), lambda i, j: (0, 0)),
                  pl.BlockSpec((D, D), lambda i, j: (0, j)),
                  pl.BlockSpec((1, CA_D), lambda i, j: (0, 0))],
        out_specs=(pl.BlockSpec((tm, D), lambda i, j: (i, 0)), pl.BlockSpec((tm, D), lambda i, j: (i, 0))),
        scratch_shapes=[pltpu.VMEM((tm, D), BF16)],
        compiler_params=_params(2),
        name="mem_kv",
    )(mem, g, w, gk)


def _lambda(lam_ref, lam0):
    lv = lam_ref[...]
    s1 = jnp.sum(lv[0:1] * lv[1:2], axis=-1, keepdims=True)
    s2 = jnp.sum(lv[2:3] * lv[3:4], axis=-1, keepdims=True)
    return jnp.exp(s1) - jnp.exp(s2) + lam0


def _softmax_step(state, s, v_bf16):
    m, l, acc = state
    m_new = jnp.maximum(m, jnp.max(s, axis=-1, keepdims=True))
    alpha = jnp.exp2(m - m_new)
    p = jnp.exp2(s - jnp.tile(m_new, (1, s.shape[1] // m_new.shape[1])))
    l = alpha * l + jnp.sum(p, axis=-1, keepdims=True)
    acc = alpha * acc + _dot(p.astype(BF16), v_bf16)
    return m_new, l, acc


def _attn_prompt_kernel(lam_ref, q_ref, k_ref, v_ref, bdiag_ref, bnear_ref, gout_ref, o_ref, m_ref, l_ref, acc_ref,
                        *, tq, lam0):
    qi = pl.program_id(2)
    q = q_ref[...]
    lane = lax.broadcasted_iota(jnp.int32, q.shape, 1)
    qm = jnp.concatenate([jnp.where(lane < DK, q, jnp.zeros_like(q)), jnp.where(lane >= DK, q, jnp.zeros_like(q))],
                         axis=0)
    m_ref[...] = jnp.full(m_ref.shape, NEG, F32)
    l_ref[...] = jnp.zeros(l_ref.shape, F32)
    acc_ref[...] = jnp.zeros(acc_ref.shape, F32)

    def update(kb, bias_ref):
        start = pl.multiple_of(kb * tq, tq)
        s = _dot_nt(qm, k_ref[pl.ds(start, tq), :])
        if bias_ref is not None:
            s = s + bias_ref[...]
        m_new, l_new, acc_new = _softmax_step((m_ref[...], l_ref[...], acc_ref[...]), s, v_ref[pl.ds(start, tq), :])
        m_ref[...] = m_new
        l_ref[...] = l_new
        acc_ref[...] = acc_new

    def loop(lo, hi, bias_ref):
        def body(kb, carry):
            update(kb, bias_ref)
            return carry
        lax.fori_loop(lo, hi, body, 0)

    loop(0, qi - 1, None)
    loop(jnp.maximum(qi - 1, 0), qi, bnear_ref)
    update(qi, bdiag_ref)

    lam = _lambda(lam_ref, lam0)
    o_maps = acc_ref[...] / l_ref[...]
    o = o_maps[0:tq] - lam * o_maps[tq:2 * tq]
    o_ref[...] = (_rms(o, gout_ref[...]) * (1.0 - lam0)).astype(o_ref.dtype)


def _attn_prompt(ub3, lam_vecs, bias_diag, bias_near, gout, *, lam0, tq):
    b, t, _ = ub3.shape
    kcol, vcol = (PB_K * TN) // DV, (PB_V * TN) // DV
    bias_spec = pl.BlockSpec((2 * tq, tq), lambda h, b, i: (h, 0))
    bias_diag, bias_near = (a.reshape(N_MAPS * tq, tq) for a in (bias_diag, bias_near))
    return pl.pallas_call(
        functools.partial(_attn_prompt_kernel, tq=tq, lam0=lam0),
        out_shape=jax.ShapeDtypeStruct((b, t, D), BF16),
        grid=(N_HEADS, b, t // tq),
        in_specs=[pl.BlockSpec((4, DK), lambda h, b, i: (0, 0)),
                  pl.BlockSpec((None, tq, DV), lambda h, b, i: (b, i, h)),
                  pl.BlockSpec((None, t, DV), lambda h, b, i: (b, 0, kcol + h)),
                  pl.BlockSpec((None, t, DV), lambda h, b, i: (b, 0, vcol + h)),
                  bias_spec, bias_spec,
                  pl.BlockSpec((1, DV), lambda h, b, i: (0, 0))],
        out_specs=pl.BlockSpec((None, tq, DV), lambda h, b, i: (b, i, h)),
        scratch_shapes=[pltpu.VMEM((2 * tq, DV), F32)] * 3,
        compiler_params=_params(3),
        name="diff_attn_prompt",
    )(lam_vecs, ub3, ub3, ub3, bias_diag, bias_near, gout)


def _head_queries(q, h):
    qh = q[:, h * DV:(h + 1) * DV]
    lane = lax.broadcasted_iota(jnp.int32, qh.shape, 1)
    return jnp.concatenate([jnp.where(lane < DK, qh, 0.0), jnp.where(lane >= DK, qh, 0.0)], axis=0).astype(BF16)


def _attn_sample_kernel(pt_ref, q_ref, kn_ref, vn_ref, bpast_ref, bnew_ref, lam_ref, gout_ref, *rest, npg, lam0):
    k_refs, v_refs = rest[:npg], rest[npg:2 * npg]
    o_ref, m_scr, l_scr, acc_scr = rest[2 * npg:]
    c = pl.program_id(1)
    last = c == pl.num_programs(1) - 1
    tdec = q_ref.shape[0]
    rows = 2 * tdec

    @pl.when(c == 0)
    def _():
        m_scr[...] = jnp.full(m_scr.shape, NEG, F32)
        l_scr[...] = jnp.zeros(l_scr.shape, F32)
        acc_scr[...] = jnp.zeros(acc_scr.shape, F32)

    q = q_ref[...]
    qms = [_head_queries(q, h) for h in range(N_HEADS)]

    def update(s, pv_fn):
        m_old = m_scr[...]
        m_new = jnp.maximum(m_old, jnp.max(s, axis=-1, keepdims=True))
        alpha = jnp.exp2(m_old - m_new)
        p = jnp.exp2(s - m_new[:, 0:1])
        l_scr[...] = alpha * l_scr[...] + jnp.sum(p, axis=-1, keepdims=True)
        acc_scr[...] = alpha * acc_scr[...] + pv_fn(p.astype(BF16))
        m_scr[...] = m_new

    near = last.astype(F32)
    s_pages = []
    for i in range(npg):
        s_i = jnp.concatenate([_dot(qms[h], k_refs[i][h * DV:(h + 1) * DV, :].astype(BF16))
                               for h in range(N_HEADS)], axis=0)
        if i == npg - 1:
            s_i = s_i + near * bpast_ref[...]
        s_pages.append(s_i)

    def pv_pages(p):
        heads = []
        for h in range(N_HEADS):
            out = None
            for i in range(npg):
                v_h = v_refs[i][pl.ds(h, PAGE, stride=N_HEADS), :].astype(BF16)
                part = _dot(p[h * rows:(h + 1) * rows, i * PAGE:(i + 1) * PAGE], v_h)
                out = part if out is None else out + part
            heads.append(out)
        return jnp.concatenate(heads, axis=0)

    update(jnp.concatenate(s_pages, axis=1), pv_pages)

    @pl.when(last)
    def _():
        pad = jnp.zeros((PAGE - tdec, D), F32)
        kn = jnp.concatenate([kn_ref[...], pad], axis=0).astype(BF16)
        vn = jnp.concatenate([vn_ref[...], pad], axis=0).astype(BF16)
        s_new = jnp.concatenate([_dot_nt(qms[h], kn[:, h * DV:(h + 1) * DV]) for h in range(N_HEADS)], axis=0)
        update(s_new + bnew_ref[...],
               lambda p: jnp.concatenate([_dot(p[h * rows:(h + 1) * rows], vn[:, h * DV:(h + 1) * DV])
                                          for h in range(N_HEADS)], axis=0))
        lam = _lambda(lam_ref, lam0)
        o_all = acc_scr[...] / l_scr[...]
        for h in range(N_HEADS):
            o = o_all[h * rows:h * rows + tdec] - lam * o_all[h * rows + tdec:(h + 1) * rows]
            o_ref[:, h * DV:(h + 1) * DV] = _rms(o, gout_ref[...]) * (1.0 - lam0)


def _attn_sample(page_table, q3, kn3, vn3, cache_k4, cache_v4, layer, bpast, bnew, lam_vecs, gout, *, lam0):
    bs, tdec, _ = q3.shape
    n_pages = page_table.shape[1]
    npg = math.gcd(n_pages, 16)
    pt = page_table.reshape(-1)

    def page_spec(i):
        return pl.BlockSpec((None, None, D, PAGE),
                            lambda b, c, pt_ref: (layer, pt_ref[b * n_pages + c * npg + i], 0, 0))

    per_b = pl.BlockSpec((None, tdec, D), lambda b, c, pt_ref: (b, 0, 0))
    whole = lambda shape: pl.BlockSpec(shape, lambda b, c, pt_ref: (0,) * len(shape))
    stat = pltpu.VMEM((N_HEADS * 2 * tdec, DV), F32)
    return pl.pallas_call(
        functools.partial(_attn_sample_kernel, npg=npg, lam0=lam0),
        out_shape=jax.ShapeDtypeStruct((bs, tdec, D), F32),
        grid_spec=pltpu.PrefetchScalarGridSpec(
            num_scalar_prefetch=1,
            grid=(bs, n_pages // npg),
            in_specs=[per_b, per_b, per_b,
                      whole((N_HEADS * 2 * tdec, PAGE)), whole((N_HEADS * 2 * tdec, PAGE)),
                      whole((4, DK)), whole((1, DV))]
                     + [page_spec(i) for i in range(npg)] + [page_spec(i) for i in range(npg)],
            out_specs=per_b,
            scratch_shapes=[stat, stat, stat]),
        compiler_params=_params(2),
        name="diff_attn_sample",
    )(pt, q3, kn3, vn3, bpast, bnew, lam_vecs, gout, *([cache_k4] * npg), *([cache_v4] * npg))


def _cross_kernel(cq_ref, mk_ref, mv_ref, o_ref):
    rows = cq_ref.shape[0]
    for h in range(CA_HEADS):
        cols = slice(h * CA_D, (h + 1) * CA_D)
        q = cq_ref[:, cols]
        if rows % 16:
            q = jnp.concatenate([q.astype(F32), jnp.zeros((16 - rows % 16, CA_D), F32)], axis=0)
        s = _dot_nt(q.astype(BF16), mk_ref[:, cols].astype(BF16))
        p = jnp.exp(s - jnp.max(s, axis=-1, keepdims=True))
        o = _dot(p.astype(BF16), mv_ref[:, cols].astype(BF16)) / jnp.sum(p, axis=-1, keepdims=True)
        o_ref[:, cols] = o[0:rows].astype(o_ref.dtype)


def _cross(cq3, col_block, mem_k, mem_v, out_dtype):
    b, t, _ = cq3.shape
    tm = min(2048, t)
    mem_spec = pl.BlockSpec((None, MEM_LEN, D), lambda b, i: (b, 0, 0))
    return pl.pallas_call(
        _cross_kernel,
        out_shape=jax.ShapeDtypeStruct((b, t, D), out_dtype),
        grid=(b, t // tm),
        in_specs=[pl.BlockSpec((None, tm, D), lambda b, i: (b, i, col_block)), mem_spec, mem_spec],
        out_specs=pl.BlockSpec((None, tm, D), lambda b, i: (b, i, 0)),
        compiler_params=_params(2),
        name="mem_cross_attn",
    )(cq3, mem_k, mem_v)


def _conv_silu(window, cw, cb):
    acc = cb + cw[CONV_K - 1:CONV_K] * window(CONV_K - 1)
    for k in range(CONV_K - 1):
        acc = acc + cw[k:k + 1] * window(k)
    return acc * _sigmoid(acc)


def _ssd_kernel(xbc_ref, z_ref, dt_ref, h0_ref, c0_ref, cw_ref, cb_ref, alog_ref, dskip_ref, gout_ref,
                y_ref, hn_ref, cn_ref, buf, xc_scr, dt_scr, yb, h_scr, *, tv, L):
    c = pl.program_id(1)
    head = 8

    @pl.when(c == 0)
    def _():
        buf[0:head, :] = jnp.zeros((head, CONV_CH), F32)
        buf[head - (CONV_K - 1):head, :] = c0_ref[...]
        h_scr[...] = h0_ref[...]
        if tv < L:
            buf[head + tv:, :] = jnp.zeros((L - tv, CONV_CH), F32)
            dt_scr[...] = jnp.zeros(dt_scr.shape, F32)

    buf[head:head + tv, :] = xbc_ref[...]
    cn_ref[...] = buf[head + tv - (CONV_K - 1):head + tv, :]

    first_tap = head - (CONV_K - 1)
    for j in range(CONV_CH // 512):
        cols = slice(j * 512, (j + 1) * 512)
        xc_scr[:, cols] = _conv_silu(lambda k: buf[first_tap + k:first_tap + k + L, cols],
                                     cw_ref[:, cols], cb_ref[:, cols])

    if tv == L:
        buf[0:head, :] = buf[L:L + head, :]
        dt = dt_ref[...]
    else:
        dt_scr[0:tv, :] = dt_ref[...]
        dt = dt_scr[...]

    def zs(cols):
        z = z_ref[:, cols]
        return z * _sigmoid(z)

    _ssd_scan(lambda cols: xc_scr[:, cols], zs, dt, alog_ref, dskip_ref, gout_ref, y_ref, hn_ref, yb, h_scr, tv, L)


def _ssd_scan(xc, zs, dt, alog_ref, dskip_ref, gout_ref, y_ref, hn_ref, yb, h_scr, tv, L):
    a_neg = -jnp.exp(alog_ref[...])
    d_a = dt * a_neg
    row = lax.broadcasted_iota(jnp.int32, (L, L), 0)
    col = lax.broadcasted_iota(jnp.int32, (L, L), 1)
    causal = col <= row
    a_cum = _dot_split3(causal.astype(BF16), d_a)
    a_cum_t = a_cum.T
    dt_t = dt.T
    a_last = a_cum[L - 1:L, :]
    e_cum = jnp.exp(a_cum)
    to_end = jnp.exp(a_last - a_cum) * dt
    e_last = jnp.exp(a_last)
    lane = lax.broadcasted_iota(jnp.int32, (L, 128), 1)
    first = lane < SSM_P
    row_first = lax.broadcasted_iota(jnp.int32, (128, 128), 0) < SSM_P

    for g in range(SSM_G):
        b_g = xc(slice(SSM_INNER + g * SSM_N, SSM_INNER + (g + 1) * SSM_N)).astype(BF16)
        c_off = SSM_INNER + SSM_G * SSM_N
        c_g = xc(slice(c_off + g * SSM_N, c_off + (g + 1) * SSM_N)).astype(BF16)
        cb = _dot_nt(c_g, b_g)
        for jj in (2 * g, 2 * g + 1):
            cols = slice(jj * 128, (jj + 1) * 128)
            x_pair = xc(cols)
            x_bf = x_pair.astype(BF16)
            yw = []
            for hh in (2 * jj, 2 * jj + 1):
                seg = a_cum[:, hh:hh + 1] - a_cum_t[hh:hh + 1, :]
                w = cb * jnp.exp(jnp.where(causal, seg, NEG)) * dt_t[hh:hh + 1, :]
                yw.append(_dot(w.astype(BF16), x_bf))
            ha, hb = 2 * jj, 2 * jj + 1
            hp = h_scr[jj]
            y = jnp.where(first, yw[0], yw[1])
            y = y + _dot_nt(c_g, hp.astype(BF16)) * jnp.where(first, e_cum[:, ha:ha + 1], e_cum[:, hb:hb + 1])
            y = y + dskip_ref[:, cols] * x_pair
            te = jnp.where(first, to_end[:, ha:ha + 1], to_end[:, hb:hb + 1])
            upd = _dot_tn((x_pair * te).astype(BF16), b_g)
            h_scr[jj] = hp * jnp.where(row_first, e_last[:, ha:ha + 1], e_last[:, hb:hb + 1]) + upd
            yb[:, cols] = y[0:tv] * zs(cols)

    gw = SSM_INNER // SSM_G
    for g in range(SSM_G):
        cols = slice(g * gw, (g + 1) * gw)
        y_ref[:, cols] = _rms(yb[:, cols], gout_ref[:, cols]).astype(y_ref.dtype)
    hn_ref[...] = h_scr[...]


def _ssd(xbc_src, xbc_blk, z_src, z_blk, dt3, h0, c0, layer, cw, cb, alog, dskip, gout, out_dtype):
    b, t, _ = dt3.shape
    tv = min(SSM_L, t)
    chunk = max(tv, SSM_L_MIN)
    nc = t // tv
    seq = lambda width, blk: pl.BlockSpec((None, tv, width), lambda b, c: (b, c, blk))
    per_b = lambda shape: pl.BlockSpec((None,) + shape, lambda b, c: (b,) + (0,) * len(shape))
    state = lambda shape: pl.BlockSpec((None, None) + shape, lambda b, c: (layer, b) + (0,) * len(shape))
    whole = lambda shape: pl.BlockSpec(shape, lambda b, c: (0,) * len(shape))
    y_shape = jax.ShapeDtypeStruct((b, t, SSM_INNER), out_dtype)
    h_shape = jax.ShapeDtypeStruct((b, N_PAIRS, 128, SSM_N), F32)
    tail = [whole((1, 128)), whole((1, SSM_INNER)), whole((1, SSM_INNER))]
    yb_h = [pltpu.VMEM((tv, SSM_INNER), F32), pltpu.VMEM((N_PAIRS, 128, SSM_N), F32)]
    return pl.pallas_call(
        functools.partial(_ssd_kernel, tv=tv, L=chunk),
        out_shape=(y_shape, h_shape, jax.ShapeDtypeStruct((b, CONV_K - 1, CONV_CH), F32)),
        grid=(b, nc),
        in_specs=[seq(CONV_CH, xbc_blk), seq(SSM_INNER, z_blk), seq(128, 0),
                  state((N_PAIRS, 128, SSM_N)), state((CONV_K - 1, CONV_CH)),
                  whole((CONV_K, CONV_CH)), whole((1, CONV_CH))] + tail,
        out_specs=(seq(SSM_INNER, 0), per_b((N_PAIRS, 128, SSM_N)), per_b((CONV_K - 1, CONV_CH))),
        scratch_shapes=[pltpu.VMEM((chunk + 8, CONV_CH), F32), pltpu.VMEM((chunk, CONV_CH), F32),
                        pltpu.VMEM((chunk, 128), F32)] + yb_h,
        compiler_params=_params(2),
        name="conv_ssd",
    )(xbc_src, z_src, dt3, h0, c0, cw, cb, alog, dskip, gout)


def _merge_kernel(x_ref, gmix_ref, wg_ref, oa_ref, ys_ref, oc_ref, wa_ref, ws_ref, wc_ref, wo_ref, o_ref):
    x = x_ref[...]
    h = _rms(x, gmix_ref[...]).astype(BF16)
    branches = ((oa_ref, wa_ref), (ys_ref, ws_ref), (oc_ref, wc_ref))
    merged = None
    for b, (act_ref, w_ref) in enumerate(branches):
        gate = _sigmoid(_dot_nt(h, wg_ref[b * D:(b + 1) * D, :]))
        term = gate * _dot(act_ref[...].astype(BF16), w_ref[...])
        merged = term if merged is None else merged + term
    o_ref[...] = x + _dot(merged.astype(BF16), wo_ref[...])


def _merge(x, gmix, wg, oa, ys, oc, wa, ws, wc, wo):
    n = x.shape[0]
    tm = min(512, n)
    row = lambda width: pl.BlockSpec((tm, width), lambda i: (i, 0))
    whole = lambda shape: pl.BlockSpec(shape, lambda i: (0, 0), pipeline_mode=pl.Buffered(1))
    return pl.pallas_call(
        _merge_kernel,
        out_shape=jax.ShapeDtypeStruct((n, D), F32),
        grid=(n // tm,),
        in_specs=[row(D), whole((1, D)), whole((3 * D, D)), row(D), row(SSM_INNER), row(D),
                  whole((D, D)), whole((SSM_INNER, D)), whole((D, D)), whole((D, D))],
        out_specs=row(D),
        compiler_params=_params(1),
        name="branch_merge",
    )(x, gmix, wg, oa, ys, oc, wa, ws, wc, wo)


MLP_TF = 1024


def _mlp_kernel(x_ref, g_ref, wu_ref, wd_ref, o_ref):
    x = x_ref[...]
    h = _rms(x, g_ref[...]).astype(BF16)
    acc = x
    for f in range(D_FF // MLP_TF):
        cols = slice(f * MLP_TF, (f + 1) * MLP_TF)
        a = jnp.maximum(_dot(h, wu_ref[:, cols]), 0.0)
        acc = acc + _dot((a * a).astype(BF16), wd_ref[cols, :])
    o_ref[...] = acc


def _mlp(x, g, wu, wd):
    n = x.shape[0]
    tm = min(512, n)
    row = pl.BlockSpec((tm, D), lambda i: (i, 0))
    whole = lambda shape: pl.BlockSpec(shape, lambda i: (0, 0), pipeline_mode=pl.Buffered(1))
    return pl.pallas_call(
        _mlp_kernel,
        out_shape=jax.ShapeDtypeStruct((n, D), F32),
        grid=(n // tm,),
        in_specs=[row, whole((1, D)), whole((D, D_FF)), whole((D_FF, D))],
        out_specs=row,
        compiler_params=_params(1),
        name="relu2_mlp",
    )(x, g, wu, wd)


def _lambda_init(layer):
    return 0.8 - 0.6 * math.exp(-0.3 * layer)


def _layer_weights(l, p):
    w_in_t = jnp.swapaxes(p['w_in'], 1, 2)[l]
    cols = lambda off, width: w_in_t[off:off + width]
    w_attn = jnp.concatenate([cols(OFF_Q, D), cols(OFF_CQ, D), cols(OFF_K, D), cols(OFF_V, D)], axis=0).astype(BF16)
    w_ssm = jnp.concatenate([cols(OFF_XBC, CONV_CH), cols(OFF_Z, SSM_INNER)], axis=0).astype(BF16)
    w_gate = cols(OFF_G, 3 * D).astype(BF16)
    w_dt = jnp.pad(cols(OFF_DT, SSM_HEADS), ((0, 128 - SSM_HEADS), (0, 0))).astype(BF16)
    row = lambda v: v.reshape(1, -1).astype(F32)
    return dict(
        g_mix=row(p['g_mix'][l]), w_attn=w_attn, w_ssm=w_ssm, w_dt=w_dt, w_gate=w_gate,
        dt_bias=jnp.pad(row(p['dt_bias'][l]), ((0, 0), (0, 128 - SSM_HEADS))),
        gq=row(jnp.tile(p['g_da_q'][l], TN // DK)), gk=row(jnp.tile(p['g_da_k'][l], TN // DK)),
        gcq=row(p['g_ca_q'][l]),
        lam=jnp.stack([p['lambda_q1'][l], p['lambda_k1'][l], p['lambda_q2'][l], p['lambda_k2'][l]]).astype(F32),
        g_da_out=row(p['g_da_out'][l]),
        conv_w=p['conv_w'][l].astype(F32), conv_b=row(p['conv_b'][l]),
        a_log=jnp.pad(row(p['a_log'][l]), ((0, 0), (0, 128 - SSM_HEADS))),
        d_skip=row(jnp.repeat(p['d_skip'][l], SSM_P)), g_ssm_out=row(p['g_ssm_out'][l]),
        w_br_attn=p['w_br_attn'][l].astype(BF16), w_br_ssm=p['w_br_ssm'][l].astype(BF16),
        w_br_cross=p['w_br_cross'][l].astype(BF16), w_out=p['w_out'][l].astype(BF16),
        g_mlp=row(p['g_mlp'][l]), w_up=p['w_up'][l].astype(BF16), w_down=p['w_down'][l].astype(BF16),
        g_mem=row(p['g_mem'][l]), w_mem_kv=p['w_mem_kv'][l].astype(BF16), g_ca_k=row(p['g_ca_k'][l]),
    )


def _seg64_matrix():
    i = jnp.arange(SEG_W) // DK
    return jnp.where(i[:, None] == i[None, :], 1.0 / DK, 0.0).astype(BF16)


def _layer(x, bsz, w, s64, attn_fn, mem_k, mem_v, h0, c0, state_layer, act_dtype):
    n = x.shape[0]
    t = n // bsz
    ub, k32, v32, uf, dt = _proj(x, w['g_mix'], w['w_attn'], w['w_ssm'], w['w_dt'], w['dt_bias'],
                                 w['gq'], w['gk'], w['gcq'], s64, t if act_dtype == BF16 else None)
    ub3 = ub.reshape(bsz, t, -1)
    o_a = attn_fn(ub3, k32, v32)
    uf3 = uf.reshape(bsz, t, -1)
    y_n, h_new, c_new = _ssd(uf3, 0, uf3, ((PB_Z - PB_XBC) * TN) // SSM_INNER, dt.reshape(bsz, t, 128), h0, c0,
                             state_layer, w['conv_w'], w['conv_b'], w['a_log'], w['d_skip'], w['g_ssm_out'], act_dtype)
    if act_dtype == BF16:
        cq3, cq_blk = ub3, (PB_CQ * TN) // D
    else:
        cq3, cq_blk = ub3[:, :, PB_CQ * TN:PB_K * TN].astype(F32), 0
    o_c = _cross(cq3, cq_blk, mem_k, mem_v, act_dtype)
    x = _merge(x, w['g_mix'], w['w_gate'], o_a.reshape(n, D), y_n.reshape(n, SSM_INNER), o_c.reshape(n, D),
               w['w_br_attn'], w['w_br_ssm'], w['w_br_cross'], w['w_out'])
    x = _mlp(x, w['g_mlp'], w['w_up'], w['w_down'])
    return x, k32, v32, h_new, c_new


def kernel(x_prompt, x_sample, cache_k, cache_v, cache_mem_k, cache_mem_v, state_ssm, state_conv, page_table, mem_prompt, rel_bias_table, g_mix, w_in, g_da_q, g_da_k, lambda_q1, lambda_k1, lambda_q2, lambda_k2, g_da_out, w_br_attn, conv_w, conv_b, dt_bias, a_log, d_skip, g_ssm_out, w_br_ssm, g_mem, w_mem_kv, g_ca_q, g_ca_k, w_br_cross, w_out, g_mlp, w_up, w_down):
    p = dict(g_mix=g_mix, w_in=w_in, g_da_q=g_da_q, g_da_k=g_da_k, lambda_q1=lambda_q1, lambda_k1=lambda_k1,
             lambda_q2=lambda_q2, lambda_k2=lambda_k2, g_da_out=g_da_out, w_br_attn=w_br_attn, conv_w=conv_w,
             conv_b=conv_b, dt_bias=dt_bias, a_log=a_log, d_skip=d_skip, g_ssm_out=g_ssm_out, w_br_ssm=w_br_ssm,
             g_mem=g_mem, w_mem_kv=w_mem_kv, g_ca_q=g_ca_q, g_ca_k=g_ca_k, w_br_cross=w_br_cross, w_out=w_out,
             g_mlp=g_mlp, w_up=w_up, w_down=w_down)
    depth = w_in.shape[0]
    bp, tp, _ = x_prompt.shape
    bs, ts, _ = x_sample.shape
    n_phys = cache_k.shape[1]
    past_len = page_table.shape[1] * PAGE
    tq = min(512, tp)
    table = rel_bias_table.astype(F32)
    s64 = _seg64_matrix()

    bias_diag, bias_near = _bias_tiles(table, 0, (tq, tq)), _bias_tiles(table, tq, (tq, tq))
    as_heads = lambda a: a.reshape(N_HEADS * 2 * ts, PAGE)
    bias_past = as_heads(_bias_tiles(table, PAGE, (ts, PAGE)))
    bias_new = as_heads(_bias_tiles(table, 0, (ts, PAGE)))

    cache_k4 = jnp.transpose(cache_k, (0, 1, 3, 4, 2)).reshape(depth, n_phys, D, PAGE)
    cache_v4 = cache_v.reshape(depth, n_phys, PAGE * N_HEADS, DV)
    mem_flat = mem_prompt.reshape(bp * MEM_LEN, D)
    zero_h = jnp.zeros((1, bp, N_PAIRS, 128, SSM_N), F32)
    zero_c = jnp.zeros((1, bp, CONV_K - 1, CONV_CH), F32)
    state_ssm5 = state_ssm.reshape(depth, bs, N_PAIRS, 128, SSM_N)

    xp = x_prompt.reshape(bp * tp, D)
    xs = x_sample.reshape(bs * ts, D)
    outs = {k: [] for k in ('kp', 'vp', 'mk', 'mv', 'hp', 'cp', 'ks', 'vs', 'hs', 'cs')}
    for l in range(depth):
        w = _layer_weights(l, p)
        lam0 = _lambda_init(l)

        mk, mv = _memkv(mem_flat, w['g_mem'], w['w_mem_kv'], w['g_ca_k'])
        attn_p = lambda ub3, k32, v32: _attn_prompt(ub3, w['lam'], bias_diag, bias_near, w['g_da_out'],
                                                    lam0=lam0, tq=tq)
        xp, k, v, hn, cn = _layer(xp, bp, w, s64, attn_p, mk.reshape(bp, MEM_LEN, D), mv.reshape(bp, MEM_LEN, D),
                                  zero_h, zero_c, 0, BF16)
        for key, val in zip(('kp', 'vp', 'mk', 'mv', 'hp', 'cp'), (k, v, mk, mv, hn, cn)):
            outs[key].append(val)

        def attn_s(ub3, k32, v32):
            q3 = ub3[:, :, 0:D].astype(F32)
            return _attn_sample(page_table, q3, k32.reshape(bs, ts, D), v32.reshape(bs, ts, D), cache_k4, cache_v4,
                                l, bias_past, bias_new, w['lam'], w['g_da_out'], lam0=lam0)

        xs, k, v, hn, cn = _layer(xs, bs, w, s64, attn_s, cache_mem_k[l].reshape(bs, MEM_LEN, D),
                                  cache_mem_v[l].reshape(bs, MEM_LEN, D), state_ssm5, state_conv, l, F32)
        for key, val in zip(('ks', 'vs', 'hs', 'cs'), (k, v, hn, cn)):
            outs[key].append(val)

    st = lambda key, shape: jnp.stack(outs[key]).reshape((depth,) + shape)
    return (xp.reshape(bp, tp, D), xs.reshape(bs, ts, D),
            jnp.transpose(st('kp', (bp, N_MAPS, DK, tp)), (0, 1, 4, 2, 3)), st('vp', (bp, tp, N_HEADS, DV)),
            st('mk', (bp, MEM_LEN, CA_HEADS, CA_D)), st('mv', (bp, MEM_LEN, CA_HEADS, CA_D)),
            st('hp', (bp, SSM_HEADS, SSM_P, SSM_N)), st('cp', (bp, CONV_K - 1, CONV_CH)),
            st('ks', (bs, ts, N_MAPS, DK)), st('vs', (bs, ts, N_HEADS, DV)),
            st('hs', (bs, SSM_HEADS, SSM_P, SSM_N)), st('cs', (bs, CONV_K - 1, CONV_CH)))
```
